```python
import math
import jax, jax.numpy as jnp
from jax import lax
import numpy as np

D_MODEL = 2048
BATCH = 2
SEQ = 16384
DEPTH = 2

GRID_W = 64
CTX_LEN = 256
N_BRANCH = 4
MIX_W = D_MODEL // N_BRANCH
HEAD_DIM = 128
EPS = 1e-6
ROPE_BASE = 10000.0
HY_W = MIX_W
HY_ORDER = 2
HY_CONV = 3
HY_BANDS = 16
HY_EMB_DIM = 2 * HY_BANDS + 1
HY_FFN = 64
HY_DECAY_TARGET = 1e-2
HY_FAST_PCT = 0.3
HY_SLOW_PCT = 1.5
NA_HEADS = MIX_W // HEAD_DIM
NA_WIN_R = 8
NA_WIN_C = 16
RET_HEADS = MIX_W // HEAD_DIM
RET_DK = HEAD_DIM // 2
RET_DV = HEAD_DIM
RET_CHUNK = 128
RET_ROPE_BASE = 10000.0
GQA_Q_HEADS = MIX_W // HEAD_DIM
GQA_KV_HEADS = max(GQA_Q_HEADS // 2, 1)
GQA_BLOCK = 128
FFN_HIDDEN = ((8 * D_MODEL + 3 * 256 - 1) // (3 * 256)) * 256
SPLIT_SIZES = (3 * HY_W,
               NA_HEADS * HEAD_DIM, NA_HEADS * HEAD_DIM, NA_HEADS * HEAD_DIM,
               RET_HEADS * RET_DK, RET_HEADS * RET_DK, RET_HEADS * RET_DV, RET_HEADS * RET_DV,
               GQA_Q_HEADS * HEAD_DIM, GQA_KV_HEADS * HEAD_DIM, GQA_KV_HEADS * HEAD_DIM,
               N_BRANCH * D_MODEL)
IN_COLS = (3 * HY_W + 3 * NA_HEADS * HEAD_DIM + 2 * RET_HEADS * RET_DK + 2 * RET_HEADS * RET_DV
           + (GQA_Q_HEADS + 2 * GQA_KV_HEADS) * HEAD_DIM + N_BRANCH * D_MODEL)

kernel_name = 'hybrid_flow_backbone'


def rms_norm(x, w):
    xf = x.astype(jnp.float32)
    y = xf * lax.rsqrt(jnp.mean(xf * xf, axis=-1, keepdims=True) + EPS)
    return (y * w.astype(jnp.float32)).astype(x.dtype)


def modulate(xn, shift, scale):
    return xn * (1 + scale) + shift


def rotate_pairs(x, ang):
    m = x.shape[-1] // 2
    x1, x2 = x[..., :m], x[..., m:]
    cos, sin = jnp.cos(ang), jnp.sin(ang)
    return jnp.concatenate([x1 * cos - x2 * sin, x1 * sin + x2 * cos], axis=-1)


def axial_rope(x):
    n, d = x.shape[1], x.shape[-1]
    nf = d // 4
    t = jnp.arange(n)
    inv = ROPE_BASE ** (-jnp.arange(nf, dtype=jnp.float32) / nf)
    ang_r = (t // GRID_W).astype(jnp.float32)[:, None] * inv
    ang_c = (t % GRID_W).astype(jnp.float32)[:, None] * inv
    xf = x.astype(jnp.float32)
    out = jnp.concatenate([rotate_pairs(xf[..., :d // 2], ang_r[:, None, :]),
                           rotate_pairs(xf[..., d // 2:], ang_c[:, None, :])], axis=-1)
    return out.astype(x.dtype)


def retnet_rope(x):
    n, d = x.shape[1], x.shape[-1]
    inv = RET_ROPE_BASE ** (-jnp.linspace(0.0, 1.0, d // 2, dtype=jnp.float32))
    ang = jnp.arange(n, dtype=jnp.float32)[:, None] * inv
    return rotate_pairs(x.astype(jnp.float32), ang[:, None, :])


def split_proj(p):
    idx = np.cumsum(np.array(SPLIT_SIZES))[:-1].tolist()
    return jnp.split(p, idx, axis=-1)


def ctx_attention(q, k, v):
    B, L, Hq, d = q.shape
    Hkv = k.shape[2]
    qg = q.reshape(B, L, Hkv, Hq // Hkv, d)
    s = jnp.einsum('bqkgd,bskd->bkgqs', qg, k).astype(jnp.float32) * d ** -0.5
    p = jax.nn.softmax(s, axis=-1).astype(v.dtype)
    return jnp.einsum('bkgqs,bskd->bqkgd', p, v).reshape(B, L, Hq * d)


def hyena_filter_spectra(L, w1, b1, w2, b2, w3, sin_freq):
    t = jnp.arange(L, dtype=jnp.float32)
    t_norm = t / max(L - 1, 1)
    w = 2.0 * math.pi * t / L
    f = jnp.linspace(1e-4, HY_BANDS - 1, HY_BANDS, dtype=jnp.float32)
    feat = jnp.concatenate([t_norm[:, None], jnp.cos(w[:, None] * f), -jnp.sin(w[:, None] * f)], axis=-1)
    z = jnp.sin(sin_freq[0] * (feat @ w1 + b1))
    z = jnp.sin(sin_freq[1] * (z @ w2 + b2))
    h = (z @ w3).astype(jnp.float32).reshape(L, HY_ORDER, 2, HY_W)
    deltas = jnp.abs(jnp.linspace(math.log(HY_DECAY_TARGET) / HY_SLOW_PCT,
                                  math.log(HY_DECAY_TARGET) / HY_FAST_PCT, HY_W, dtype=jnp.float32))
    h = h * jnp.exp(-t_norm[:, None] * deltas)[:, None, None, :]
    h_fwd, h_bwd = h[:, :, 0], h[:, :, 1]
    filt = jnp.concatenate([h_fwd, jnp.zeros((1, HY_ORDER, HY_W), jnp.float32), h_bwd[:0:-1]], axis=0)
    filt = filt * lax.rsqrt(jnp.sum(filt * filt, axis=0, keepdims=True) + EPS)
    return jnp.fft.rfft(filt, axis=0)


def long_conv(z, spec):
    L = z.shape[1]
    zf = jnp.fft.rfft(z, n=2 * L, axis=1)
    return jnp.fft.irfft(zf * spec[None], n=2 * L, axis=1)[:, :L]


def short_conv(u, w, b):
    C = u.shape[-1]
    y = lax.conv_general_dilated(u, w[:, None, :].astype(u.dtype), window_strides=(1,),
                                 padding=((HY_CONV // 2, HY_CONV // 2),),
                                 dimension_numbers=('NWC', 'WIO', 'NWC'), feature_group_count=C)
    return y + b


def hyena_mixer(p, conv_w, conv_b, w1, b1, w2, b2, w3, sin_freq, bias):
    L = p.shape[1]
    u = short_conv(p, conv_w, conv_b).astype(jnp.float32)
    v, x1, x2 = jnp.split(u, 3, axis=-1)
    spec = hyena_filter_spectra(L, w1, b1, w2, b2, w3, sin_freq)
    bias = bias.astype(jnp.float32)
    z = v
    for o, gate in enumerate((x1, x2)):
        z = gate * (long_conv(z, spec[:, o]) + bias[o] * z)
    return z.astype(p.dtype)


def na_mixer(q_l, k_l, v_l, q_c, k_c, v_c, rpb, with_ctx_out):
    B, N = q_l.shape[:2]
    Lc = q_c.shape[1]
    R = N // GRID_W
    WR = min(NA_WIN_R, R)
    H, d = NA_HEADS, HEAD_DIM
    scale = d ** -0.5
    qg = q_l.reshape(B, R, GRID_W, H, d)
    kg = k_l.reshape(B, R, GRID_W, H, d)
    vg = v_l.reshape(B, R, GRID_W, H, d)
    kc = k_c.reshape(B, Lc, H, d)
    vc = v_c.reshape(B, Lc, H, d)
    cols = jnp.arange(GRID_W)
    c0 = jnp.clip(cols - NA_WIN_C // 2, 0, GRID_W - NA_WIN_C)
    col_ok = (cols[None, :] >= c0[:, None]) & (cols[None, :] < c0[:, None] + NA_WIN_C)
    col_idx = jnp.clip(cols[None, :] - cols[:, None] + NA_WIN_C - 1, 0, 2 * NA_WIN_C - 2)
    rpb_cols = rpb[:, :, col_idx].astype(jnp.float32)

    def row(args):
        q_r, r = args
        r0 = jnp.clip(r - WR // 2, 0, R - WR)
        k_rows = lax.dynamic_slice_in_dim(kg, r0, WR, axis=1)
        v_rows = lax.dynamic_slice_in_dim(vg, r0, WR, axis=1)
        s_lat = jnp.einsum('bqhd,bwkhd->bhqwk', q_r, k_rows).astype(jnp.float32) * scale
        bias = jnp.take(rpb_cols, r0 + jnp.arange(WR) - r + NA_WIN_R - 1, axis=1)
        s_lat = jnp.where(col_ok[:, None, :], s_lat + jnp.transpose(bias, (0, 2, 1, 3)), -jnp.inf)
        s_ctx = jnp.einsum('bqhd,bchd->bhqc', q_r, kc).astype(jnp.float32) * scale
        s = jnp.concatenate([s_lat.reshape(B, H, GRID_W, WR * GRID_W), s_ctx], axis=-1)
        p = jax.nn.softmax(s, axis=-1).astype(v_l.dtype)
        p_lat = p[..., :WR * GRID_W].reshape(B, H, GRID_W, WR, GRID_W)
        return (jnp.einsum('bhqwk,bwkhd->bqhd', p_lat, v_rows)
                + jnp.einsum('bhqc,bchd->bqhd', p[..., WR * GRID_W:], vc))

    y = lax.map(row, (jnp.moveaxis(qg, 1, 0), jnp.arange(R)))
    y_l = jnp.moveaxis(y, 0, 1).reshape(B, N, H * d)
    y_c = ctx_attention(q_c.reshape(B, Lc, H, d), kc, vc) if with_ctx_out else None
    return y_l, y_c


def retention_chunkwise(q, k, v, log_g, s0):
    B, H, L, dk = q.shape
    dv = v.shape[-1]
    C = RET_CHUNK
    n = L // C
    qc, kc, vc = q.reshape(B, H, n, C, dk), k.reshape(B, H, n, C, dk), v.reshape(B, H, n, C, dv)
    j = jnp.arange(C, dtype=jnp.float32)
    rel = j[:, None] - j[None, :]
    intra = jnp.where(rel >= 0, jnp.exp(log_g[:, None, None] * jnp.maximum(rel, 0.0)), 0.0)
    scores = jnp.einsum('bhncd,bhnmd->bhncm', qc, kc) * intra[None, :, None]
    inner = jnp.einsum('bhncm,bhnme->bhnce', scores, vc)
    k_dec = kc * jnp.exp(log_g[:, None] * (C - 1 - j))[None, :, None, :, None]
    kv = jnp.einsum('bhncd,bhnce->bhnde', k_dec, vc)
    chunk_decay = jnp.exp(log_g * C)[None, :, None, None]

    def step(state, kv_i):
        return chunk_decay * state + kv_i, state

    s_final, s_prev = lax.scan(step, s0, jnp.moveaxis(kv, 2, 0))
    cross = (jnp.einsum('bhncd,nbhde->bhnce', qc, s_prev)
             * jnp.exp(log_g[:, None] * (j + 1))[None, :, None, :, None])
    return (inner + cross).reshape(B, H, L, dv), s_final


def retention_out(o, g, gn_w):
    o = o * lax.rsqrt(jnp.mean(o * o, axis=-1, keepdims=True) + EPS)
    B, H, L, dv = o.shape
    o = jnp.moveaxis(o, 1, 2).reshape(B, L, H * dv) * gn_w.astype(jnp.float32)
    return (o * jax.nn.silu(g.astype(jnp.float32))).astype(g.dtype)


def retention_mixer(q_l, k_l, v_l, g_l, q_c, k_c, v_c, g_c, log_decay, gn_w, with_ctx_out):
    def heads(t, d):
        return t.reshape(t.shape[0], t.shape[1], RET_HEADS, d)

    def bhld(t):
        return jnp.moveaxis(t, 2, 1).astype(jnp.float32)

    ksc = RET_DK ** -0.5
    ql = bhld(retnet_rope(heads(q_l, RET_DK)))
    kl = bhld(retnet_rope(heads(k_l, RET_DK))) * ksc
    vl = bhld(heads(v_l, RET_DV))
    qc = bhld(heads(q_c, RET_DK))
    kc = bhld(heads(k_c, RET_DK)) * ksc
    vc = bhld(heads(v_c, RET_DV))
    log_g = -jnp.abs(log_decay.astype(jnp.float32))
    s0 = jnp.zeros((q_l.shape[0], RET_HEADS, RET_DK, RET_DV), jnp.float32)

    def flip(t):
        return jnp.flip(t, axis=2)

    o_cf, s_fwd = retention_chunkwise(qc, kc, vc, log_g[0], s0)
    o_cb, s_bwd = retention_chunkwise(flip(qc), flip(kc), flip(vc), log_g[1], s0)
    o_lf, _ = retention_chunkwise(ql, kl, vl, log_g[0], s_fwd)
    o_lb, _ = retention_chunkwise(flip(ql), flip(kl), flip(vl), log_g[1], s_bwd)
    y_l = retention_out(o_lf + flip(o_lb), g_l, gn_w)
    y_c = retention_out(o_cf + flip(o_cb), g_c, gn_w) if with_ctx_out else None
    return y_l, y_c


def gqa_mixer(q_l, k_l, v_l, q_c, k_c, v_c, qn_w, kn_w, with_ctx_out):
    B, N = q_l.shape[:2]
    Lc = q_c.shape[1]
    G = GQA_Q_HEADS // GQA_KV_HEADS
    ql = axial_rope(rms_norm(q_l.reshape(B, N, GQA_Q_HEADS, HEAD_DIM), qn_w))
    kl = axial_rope(rms_norm(k_l.reshape(B, N, GQA_KV_HEADS, HEAD_DIM), kn_w))
    vl = v_l.reshape(B, N, GQA_KV_HEADS, HEAD_DIM)
    qc = rms_norm(q_c.reshape(B, Lc, GQA_Q_HEADS, HEAD_DIM), qn_w)
    kc = rms_norm(k_c.reshape(B, Lc, GQA_KV_HEADS, HEAD_DIM), kn_w)
    vc = v_c.reshape(B, Lc, GQA_KV_HEADS, HEAD_DIM)
    keys = jnp.concatenate([kl, kc], axis=1)
    vals = jnp.concatenate([vl, vc], axis=1)
    nb = N // GQA_BLOCK
    qb = jnp.moveaxis(ql.reshape(B, nb, GQA_BLOCK, GQA_KV_HEADS, G, HEAD_DIM), 1, 0)

    def block(qi):
        s = jnp.einsum('bqkgd,bskd->bkgqs', qi, keys).astype(jnp.float32) * HEAD_DIM ** -0.5
        p = jax.nn.softmax(s, axis=-1).astype(vals.dtype)
        return jnp.einsum('bkgqs,bskd->bqkgd', p, vals)

    y_l = jnp.moveaxis(lax.map(block, qb), 0, 1).reshape(B, N, GQA_Q_HEADS * HEAD_DIM)
    y_c = ctx_attention(qc, kc, vc) if with_ctx_out else None
    return y_l, y_c


def merge_branches(branches, gate_pre, w_branch, w_out):
    B, L = gate_pre.shape[:2]
    gates = jax.nn.sigmoid(gate_pre.astype(jnp.float32)).astype(gate_pre.dtype)
    gates = gates.reshape(B, L, N_BRANCH, D_MODEL)
    m = gates[:, :, 0] * (branches[0] @ w_branch[0])
    for i in range(1, N_BRANCH):
        m = m + gates[:, :, i] * (branches[i] @ w_branch[i])
    return m @ w_out


def token_mixing(h_l, h_c, w_in, hy_conv_w, hy_conv_b, hy_ffn_w1, hy_ffn_b1, hy_ffn_w2, hy_ffn_b2,
                 hy_ffn_w3, hy_sin_freq, hy_bias, na_rpb, ret_log_decay, ret_gn_w,
                 gqa_q_norm_w, gqa_k_norm_w, w_branch, w_out, with_ctx_out):
    (hy_l, naq_l, nak_l, nav_l, rq_l, rk_l, rv_l, rg_l, gq_l, gk_l, gv_l, gate_l) = split_proj(h_l @ w_in)
    (hy_c, naq_c, nak_c, nav_c, rq_c, rk_c, rv_c, rg_c, gq_c, gk_c, gv_c, gate_c) = split_proj(h_c @ w_in)
    hy_params = (hy_conv_w, hy_conv_b, hy_ffn_w1, hy_ffn_b1, hy_ffn_w2, hy_ffn_b2, hy_ffn_w3, hy_sin_freq, hy_bias)
    y_hy_l = hyena_mixer(hy_l, *hy_params)
    y_na_l, y_na_c = na_mixer(naq_l, nak_l, nav_l, naq_c, nak_c, nav_c, na_rpb, with_ctx_out)
    y_rt_l, y_rt_c = retention_mixer(rq_l, rk_l, rv_l, rg_l, rq_c, rk_c, rv_c, rg_c,
                                     ret_log_decay, ret_gn_w, with_ctx_out)
    y_gq_l, y_gq_c = gqa_mixer(gq_l, gk_l, gv_l, gq_c, gk_c, gv_c, gqa_q_norm_w, gqa_k_norm_w, with_ctx_out)
    out_l = merge_branches((y_hy_l, y_na_l, y_rt_l, y_gq_l), gate_l, w_branch, w_out)
    if not with_ctx_out:
        return out_l, None
    y_hy_c = hyena_mixer(hy_c, *hy_params)
    out_c = merge_branches((y_hy_c, y_na_c, y_rt_c, y_gq_c), gate_c, w_branch, w_out)
    return out_l, out_c


def swiglu(h, w13, w2):
    a, b = jnp.split(h @ w13, 2, axis=-1)
    return (jax.nn.silu(a) * b) @ w2


def setup_inputs(seed: int = 0) -> dict:
    key = jax.random.key(seed)
    ks = jax.random.split(key, 32)
    nrm = jax.random.normal
    D = D_MODEL

    def w(k, shape, fan_in, gain=1.0):
        return nrm(k, shape, jnp.float32) * (gain * fan_in ** -0.5)

    ret_init = jnp.log(1.0 - 2.0 ** (-5.0 - jnp.arange(RET_HEADS, dtype=jnp.float32)))
    return {
        'x': nrm(ks[0], (BATCH, SEQ, D), jnp.float32),
        'c': nrm(ks[1], (BATCH, D), jnp.float32),
        'ctx': nrm(ks[2], (BATCH, CTX_LEN, D), jnp.float32),
        'c_ctx': nrm(ks[3], (D,), jnp.float32),
        'ada_w': w(ks[4], (DEPTH, D, 6 * D), D, 0.5),
        'ada_b': 0.01 * nrm(ks[5], (DEPTH, 6 * D), jnp.float32),
        'norm1_w': 1.0 + 0.02 * nrm(ks[6], (DEPTH, D), jnp.float32),
        'norm2_w': 1.0 + 0.02 * nrm(ks[7], (DEPTH, D), jnp.float32),
        'w_in': w(ks[8], (DEPTH, D, IN_COLS), D),
        'hy_conv_w': w(ks[9], (DEPTH, HY_CONV, 3 * HY_W), HY_CONV),
        'hy_conv_b': 0.01 * nrm(ks[10], (DEPTH, 3 * HY_W), jnp.float32),
        'hy_ffn_w1': w(ks[11], (DEPTH, HY_EMB_DIM, HY_FFN), HY_EMB_DIM),
        'hy_ffn_b1': 0.1 * nrm(ks[12], (DEPTH, HY_FFN), jnp.float32),
        'hy_ffn_w2': w(ks[13], (DEPTH, HY_FFN, HY_FFN), HY_FFN),
        'hy_ffn_b2': 0.1 * nrm(ks[14], (DEPTH, HY_FFN), jnp.float32),
        'hy_ffn_w3': w(ks[15], (DEPTH, HY_FFN, HY_ORDER * 2 * HY_W), HY_FFN),
        'hy_sin_freq': 1.0 + 0.02 * nrm(ks[16], (DEPTH, 2, HY_FFN), jnp.float32),
        'hy_bias': nrm(ks[17], (DEPTH, HY_ORDER, HY_W), jnp.float32),
        'na_rpb': 0.02 * nrm(ks[18], (DEPTH, NA_HEADS, 2 * NA_WIN_R - 1, 2 * NA_WIN_C - 1), jnp.float32),
        'ret_log_decay': ret_init * (1.0 + 0.02 * nrm(ks[19], (DEPTH, 2, RET_HEADS), jnp.float32)),
        'ret_gn_w': 1.0 + 0.02 * nrm(ks[20], (DEPTH, RET_HEADS * RET_DV), jnp.float32),
        'gqa_q_norm_w': 1.0 + 0.02 * nrm(ks[21], (DEPTH, HEAD_DIM), jnp.float32),
        'gqa_k_norm_w': 1.0 + 0.02 * nrm(ks[22], (DEPTH, HEAD_DIM), jnp.float32),
        'w_branch': w(ks[23], (DEPTH, N_BRANCH, MIX_W, D), MIX_W),
        'w_out': w(ks[24], (DEPTH, D, D), D),
        'ffn_w13': w(ks[25], (DEPTH, D, 2 * FFN_HIDDEN), D),
        'ffn_w2': w(ks[26], (DEPTH, FFN_HIDDEN, D), FFN_HIDDEN),
        'final_norm_w': 1.0 + 0.02 * nrm(ks[27], (D,), jnp.float32),
    }


def reference(x, c, ctx, c_ctx, ada_w, ada_b, norm1_w, norm2_w, w_in, hy_conv_w, hy_conv_b,
              hy_ffn_w1, hy_ffn_b1, hy_ffn_w2, hy_ffn_b2, hy_ffn_w3, hy_sin_freq, hy_bias, na_rpb,
              ret_log_decay, ret_gn_w, gqa_q_norm_w, gqa_k_norm_w, w_branch, w_out, ffn_w13, ffn_w2,
              final_norm_w):
    x_l, x_c = x, ctx
    for layer in range(DEPTH):
        last = layer == DEPTH - 1
        mod_l = (jax.nn.silu(c) @ ada_w[layer] + ada_b[layer])[:, None, :]
        mod_c = jax.nn.silu(c_ctx) @ ada_w[layer] + ada_b[layer]
        sh1_l, sc1_l, g1_l, sh2_l, sc2_l, g2_l = jnp.split(mod_l, 6, axis=-1)
        sh1_c, sc1_c, g1_c, sh2_c, sc2_c, g2_c = jnp.split(mod_c, 6, axis=-1)
        h_l = modulate(rms_norm(x_l, norm1_w[layer]), sh1_l, sc1_l)
        h_c = modulate(rms_norm(x_c, norm1_w[layer]), sh1_c, sc1_c)
        m_l, m_c = token_mixing(h_l, h_c, w_in[layer], hy_conv_w[layer], hy_conv_b[layer],
                                hy_ffn_w1[layer], hy_ffn_b1[layer], hy_ffn_w2[layer], hy_ffn_b2[layer],
                                hy_ffn_w3[layer], hy_sin_freq[layer], hy_bias[layer], na_rpb[layer],
                                ret_log_decay[layer], ret_gn_w[layer], gqa_q_norm_w[layer],
                                gqa_k_norm_w[layer], w_branch[layer], w_out[layer], not last)
        x_l = x_l + g1_l * m_l
        h_l = modulate(rms_norm(x_l, norm2_w[layer]), sh2_l, sc2_l)
        x_l = x_l + g2_l * swiglu(h_l, ffn_w13[layer], ffn_w2[layer])
        if not last:
            x_c = x_c + g1_c * m_c
            h_c = modulate(rms_norm(x_c, norm2_w[layer]), sh2_c, sc2_c)
            x_c = x_c + g2_c * swiglu(h_c, ffn_w13[layer], ffn_w2[layer])
    return rms_norm(x_l, final_norm_w)
```

```python
import functools
import math

import numpy as np
import jax
import jax.numpy as jnp
from jax import lax
from jax.experimental import pallas as pl
from jax.experimental.pallas import tpu as pltpu

F32 = jnp.float32
BF16 = jnp.bfloat16

EPS = 1e-6
GRID_W = 64
HEAD_DIM = 128
N_BRANCH = 4
ROPE_BASE = 10000.0
HY_BANDS = 16
HY_DECAY_TARGET = 1e-2
HY_FAST_PCT = 0.3
HY_SLOW_PCT = 1.5
NA_WIN_R = 8
NA_WIN_C = 16
RET_DK = 64
RET_DV = 128
RET_CHUNK = 128
RET_ROPE_BASE = 10000.0
NEG_BIG = -1e30

LANES = 128
FFT_N2 = 128
VMEM_LIMIT = 56 * 1024 * 1024

NT_DIMS = (((1,), (1,)), ((), ()))
TN_DIMS = (((0,), (0,)), ((), ()))


def _cparams(*sem):
    return pltpu.CompilerParams(dimension_semantics=sem, vmem_limit_bytes=VMEM_LIMIT)


def _pick_tile(n, cap, mult):
    best = None
    for t in range(mult, min(n, cap) + 1, mult):
        if n % t == 0:
            best = t
    assert best is not None, (n, cap, mult)
    return best


def _swap32(x):
    n = x.shape[-1]
    lane = lax.broadcasted_iota(jnp.int32, x.shape, x.ndim - 1)
    up = pltpu.roll(x, n - 32, x.ndim - 1)
    down = pltpu.roll(x, 32, x.ndim - 1)
    return jnp.where((lane % 64) < 32, up, down)


def _silu(x):
    return x * jax.nn.sigmoid(x)


def _ada_kernel(c_ref, w_ref, b_ref, o_ref):
    a = _silu(c_ref[...]).astype(BF16)
    o_ref[...] = jnp.dot(a, w_ref[...].astype(BF16), preferred_element_type=F32) + b_ref[...]


def ada_mod(cs, ada_w, ada_b):
    depth, d, n = ada_w.shape
    tn = _pick_tile(n, 1536, LANES)
    return pl.pallas_call(
        _ada_kernel,
        grid=(depth, n // tn),
        in_specs=[pl.BlockSpec((8, d), lambda l, j: (0, 0)),
                  pl.BlockSpec((None, d, tn), lambda l, j: (l, 0, j)),
                  pl.BlockSpec((None, 1, tn), lambda l, j: (l, 0, j))],
        out_specs=pl.BlockSpec((None, 8, tn), lambda l, j: (l, 0, j)),
        out_shape=jax.ShapeDtypeStruct((depth, 8, n), F32),
        compiler_params=_cparams("parallel", "parallel"),
        name="ada_mod",
    )(cs, ada_w, ada_b.reshape(depth, 1, n))


def _mod_row(row_base, tiles_per_batch, axis):
    if tiles_per_batch is None:
        return row_base
    return row_base + pl.program_id(axis) // tiles_per_batch


def _norm_mod_kernel(x_ref, w_ref, sh_ref, sc_ref, o_ref, *, row_base, tiles_per_batch):
    x = x_ref[...]
    y = x * lax.rsqrt(jnp.mean(x * x, axis=-1, keepdims=True) + EPS) * w_ref[...]
    row = _mod_row(row_base, tiles_per_batch, 0)
    sh = sh_ref[pl.ds(row, 1), :]
    sc = sc_ref[pl.ds(row, 1), :]
    o_ref[...] = (y * (1.0 + sc) + sh).astype(o_ref.dtype)


def _norm_kernel(x_ref, w_ref, o_ref):
    x = x_ref[...]
    y = x * lax.rsqrt(jnp.mean(x * x, axis=-1, keepdims=True) + EPS) * w_ref[...]
    o_ref[...] = y.astype(o_ref.dtype)


def norm_mod(x, w, mod, sh_chunk, sc_chunk, rows_per_batch, row_base):
    m, d = x.shape
    tm = _pick_tile(m if rows_per_batch is None else rows_per_batch, 512, 8)
    tpb = None if rows_per_batch is None else rows_per_batch // tm
    kern = functools.partial(_norm_mod_kernel, row_base=row_base, tiles_per_batch=tpb)
    return pl.pallas_call(
        kern, grid=(m // tm,),
        in_specs=[pl.BlockSpec((tm, d), lambda i: (i, 0)),
                  pl.BlockSpec((1, d), lambda i: (0, 0)),
                  pl.BlockSpec((8, d), lambda i: (0, sh_chunk)),
                  pl.BlockSpec((8, d), lambda i: (0, sc_chunk))],
        out_specs=pl.BlockSpec((tm, d), lambda i: (i, 0)),
        out_shape=jax.ShapeDtypeStruct((m, d), BF16),
        compiler_params=_cparams("parallel"),
        name="norm_mod",
    )(x, w.reshape(1, d), mod, mod)


def final_norm(x, w):
    m, d = x.shape
    tm = _pick_tile(m, 512, 8)
    return pl.pallas_call(
        _norm_kernel, grid=(m // tm,),
        in_specs=[pl.BlockSpec((tm, d), lambda i: (i, 0)), pl.BlockSpec((1, d), lambda i: (0, 0))],
        out_specs=pl.BlockSpec((tm, d), lambda i: (i, 0)),
        out_shape=jax.ShapeDtypeStruct((m, d), F32),
        compiler_params=_cparams("parallel"),
        name="final_norm",
    )(x, w.reshape(1, d))


def _mm_kernel(a_ref, w_ref, o_ref):
    o_ref[...] = jnp.dot(a_ref[...], w_ref[...], preferred_element_type=F32).astype(o_ref.dtype)


def _mm_heads_kernel(a_ref, w_ref, o_ref):
    acc = jnp.dot(a_ref[...], w_ref[...], preferred_element_type=F32)
    for h in range(o_ref.shape[0]):
        o_ref[h] = acc[:, h * HEAD_DIM:(h + 1) * HEAD_DIM].astype(o_ref.dtype)


def _mm_swiglu_kernel(a_ref, w1_ref, w3_ref, o_ref):
    a = a_ref[...]
    u = jnp.dot(a, w1_ref[...], preferred_element_type=F32)
    g = jnp.dot(a, w3_ref[...], preferred_element_type=F32)
    o_ref[...] = (_silu(u) * g).astype(o_ref.dtype)


def _mm_residual_kernel(a_ref, w_ref, x_ref, g_ref, o_ref, *, row_base, tiles_per_batch):
    acc = jnp.dot(a_ref[...], w_ref[...], preferred_element_type=F32)
    row = _mod_row(row_base, tiles_per_batch, 1)
    o_ref[...] = x_ref[...] + g_ref[pl.ds(row, 1), :] * acc


def matmul(a, w, out_dtype=BF16):
    m, k = a.shape
    n = w.shape[1]
    tm = _pick_tile(m, 512, 16)
    tn = _pick_tile(n, 2048, LANES)
    return pl.pallas_call(
        _mm_kernel, grid=(n // tn, m // tm),
        in_specs=[pl.BlockSpec((tm, k), lambda j, i: (i, 0)), pl.BlockSpec((k, tn), lambda j, i: (0, j))],
        out_specs=pl.BlockSpec((tm, tn), lambda j, i: (i, j)),
        out_shape=jax.ShapeDtypeStruct((m, n), out_dtype),
        compiler_params=_cparams("parallel", "parallel"),
        name="matmul",
    )(a, w)


def matmul_heads(a, w, batch):
    m, k = a.shape
    n = w.shape[1]
    lb = m // batch
    tm = _pick_tile(lb, 512, 16)
    tn = _pick_tile(n, 1536, LANES)
    tpb = lb // tm
    nh = tn // HEAD_DIM
    return pl.pallas_call(
        _mm_heads_kernel, grid=(n // tn, m // tm),
        in_specs=[pl.BlockSpec((tm, k), lambda j, i: (i, 0)), pl.BlockSpec((k, tn), lambda j, i: (0, j))],
        out_specs=pl.BlockSpec((None, nh, tm, HEAD_DIM), lambda j, i: (i // tpb, j, i % tpb, 0)),
        out_shape=jax.ShapeDtypeStruct((batch, n // HEAD_DIM, lb, HEAD_DIM), BF16),
        compiler_params=_cparams("parallel", "parallel"),
        name="matmul_heads",
    )(a, w)


def matmul_swiglu(a, w1, w3):
    m, k = a.shape
    n = w1.shape[1]
    tm = _pick_tile(m, 512, 16)
    tn = _pick_tile(n, 1408, LANES)
    return pl.pallas_call(
        _mm_swiglu_kernel, grid=(n // tn, m // tm),
        in_specs=[pl.BlockSpec((tm, k), lambda j, i: (i, 0)),
                  pl.BlockSpec((k, tn), lambda j, i: (0, j)),
                  pl.BlockSpec((k, tn), lambda j, i: (0, j))],
        out_specs=pl.BlockSpec((tm, tn), lambda j, i: (i, j)),
        out_shape=jax.ShapeDtypeStruct((m, n), BF16),
        compiler_params=_cparams("parallel", "parallel"),
        name="matmul_swiglu",
    )(a, w1, w3)


def matmul_residual(a, w, x, mod, gate_chunk, rows_per_batch, row_base):
    m, k = a.shape
    n = w.shape[1]
    tm = _pick_tile(m if rows_per_batch is None else rows_per_batch, 512, 16)
    tn = _pick_tile(n, 1024 if k <= 2048 else 512, LANES)
    tpb = None if rows_per_batch is None else rows_per_batch // tm
    kern = functools.partial(_mm_residual_kernel, row_base=row_base, tiles_per_batch=tpb)
    gblk = gate_chunk * (n // tn)
    return pl.pallas_call(
        kern, grid=(n // tn, m // tm),
        in_specs=[pl.BlockSpec((tm, k), lambda j, i: (i, 0)),
                  pl.BlockSpec((k, tn), lambda j, i: (0, j)),
                  pl.BlockSpec((tm, tn), lambda j, i: (i, j)),
                  pl.BlockSpec((8, tn), lambda j, i: (0, gblk + j))],
        out_specs=pl.BlockSpec((tm, tn), lambda j, i: (i, j)),
        out_shape=jax.ShapeDtypeStruct((m, n), F32),
        compiler_params=_cparams("parallel", "parallel"),
        name="matmul_residual",
    )(a, w, x, mod)


def _merge_kernel(y0, y1, y2, y3, g0, g1, g2, g3, wb_ref, o_ref):
    acc = None
    for i, (y, g) in enumerate(((y0, g0), (y1, g1), (y2, g2), (y3, g3))):
        t = jax.nn.sigmoid(g[...].astype(F32)) * jnp.dot(y[...], wb_ref[i], preferred_element_type=F32)
        acc = t if acc is None else acc + t
    o_ref[...] = acc.astype(o_ref.dtype)


def merge_branches(ys, gate_pre, wb):
    m, w = ys[0].shape
    d = wb.shape[2]
    tm = _pick_tile(m, 512, 16)
    tn = _pick_tile(d, 1024, LANES)
    nb = d // tn
    y_spec = pl.BlockSpec((tm, w), lambda j, i: (i, 0))
    g_specs = [pl.BlockSpec((tm, tn), functools.partial(lambda j, i, b: (i, b * nb + j), b=b))
               for b in range(N_BRANCH)]
    return pl.pallas_call(
        _merge_kernel, grid=(nb, m // tm),
        in_specs=[y_spec] * N_BRANCH + g_specs + [pl.BlockSpec((N_BRANCH, w, tn), lambda j, i: (0, 0, j))],
        out_specs=pl.BlockSpec((tm, tn), lambda j, i: (i, j)),
        out_shape=jax.ShapeDtypeStruct((m, d), BF16),
        compiler_params=_cparams("parallel", "parallel"),
        name="merge_branches",
    )(*ys, gate_pre, gate_pre, gate_pre, gate_pre, wb)


def _short_conv_kernel(x_ref, prev_ref, next_ref, w_ref, b_ref, o_ref, *, n_tiles):
    i = pl.program_id(1)
    x = x_ref[...].astype(F32)
    t = x.shape[0]
    row = lax.broadcasted_iota(jnp.int32, x.shape, 0)
    halo = prev_ref.shape[0]
    before = jnp.where(i > 0, prev_ref[...].astype(F32)[halo - 1:halo, :], 0.0)
    after = jnp.where(i < n_tiles - 1, next_ref[...].astype(F32)[0:1, :], 0.0)
    xm1 = jnp.where(row == 0, before, pltpu.roll(x, 1, 0))
    xp1 = jnp.where(row == t - 1, after, pltpu.roll(x, t - 1, 0))
    w = w_ref[...]
    y = w[0:1, :] * xm1 + w[1:2, :] * x + w[2:3, :] * xp1 + b_ref[...]
    o_ref[...] = y.astype(o_ref.dtype)


def short_conv(p, conv_w, conv_b):
    b, l, c = p.shape
    halo = 16
    t = _pick_tile(l, 512, halo)
    n_tiles = l // t
    hb = t // halo
    kern = functools.partial(_short_conv_kernel, n_tiles=n_tiles)
    return pl.pallas_call(
        kern, grid=(b, n_tiles),
        in_specs=[pl.BlockSpec((None, t, c), lambda bi, i: (bi, i, 0)),
                  pl.BlockSpec((None, halo, c), lambda bi, i: (bi, jnp.maximum(i * hb - 1, 0), 0)),
                  pl.BlockSpec((None, halo, c), lambda bi, i: (bi, jnp.minimum((i + 1) * hb, l // halo - 1), 0)),
                  pl.BlockSpec((8, c), lambda bi, i: (0, 0)),
                  pl.BlockSpec((1, c), lambda bi, i: (0, 0))],
        out_specs=pl.BlockSpec((None, t, c), lambda bi, i: (bi, i, 0)),
        out_shape=jax.ShapeDtypeStruct((b, l, c), BF16),
        compiler_params=_cparams("parallel", "parallel"),
        name="short_conv",
    )(p, p, p, jnp.pad(conv_w, ((0, 8 - conv_w.shape[0]), (0, 0))), conv_b.reshape(1, c))


def _hy_filter_kernel(fv_ref, w1_ref, b1_ref, sf1_ref, w2_ref, b2_ref, sf2_ref, w3f_ref, w3b_ref, dl_ref,
                      f_ref, ss_ref, *, seq_len):
    i = pl.program_id(0)
    tl = f_ref.shape[1]
    hi = lax.Precision.HIGHEST
    j = (i * tl + lax.broadcasted_iota(jnp.int32, (tl, 1), 0)).astype(F32)
    lane = lax.broadcasted_iota(jnp.int32, (1, LANES), 1)

    def mlp(t):
        t_norm = t / float(max(seq_len - 1, 1))
        ang = (2.0 * math.pi * t / float(seq_len)) * fv_ref[...]
        feat = jnp.where(lane == 0, t_norm,
                         jnp.where(lane <= HY_BANDS, jnp.cos(ang),
                                   jnp.where(lane <= 2 * HY_BANDS, -jnp.sin(ang), 0.0)))
        z = jnp.sin(sf1_ref[...] * (jnp.dot(feat, w1_ref[...], precision=hi, preferred_element_type=F32)
                                    + b1_ref[...]))
        z = jnp.sin(sf2_ref[...] * (jnp.dot(z, w2_ref[...], precision=hi, preferred_element_type=F32)
                                    + b2_ref[...]))
        return z, t_norm

    zf, tnf = mlp(j)
    hf = jnp.dot(zf, w3f_ref[...], precision=hi, preferred_element_type=F32) * jnp.exp(-tnf * dl_ref[...])
    zb, tnb = mlp(float(seq_len) - j)
    hb = jnp.dot(zb, w3b_ref[...], precision=hi, preferred_element_type=F32) * jnp.exp(-tnb * dl_ref[...])
    hb = jnp.where(j > 0.0, hb, 0.0)
    f_ref[0] = hf
    f_ref[1] = hb

    @pl.when(i == 0)
    def _():
        ss_ref[...] = jnp.zeros(ss_ref.shape, F32)

    ss_ref[...] += (jnp.sum(hf * hf, axis=0, keepdims=True) + jnp.sum(hb * hb, axis=0, keepdims=True))


def hy_filters(seq_len, w1, b1, w2, b2, w3, sin_freq, hy_w):
    emb, ffn = w1.shape
    cw = 2 * hy_w
    f = np.linspace(1e-4, HY_BANDS - 1, HY_BANDS)
    fv = np.zeros((1, LANES), np.float32)
    fv[0, 1:1 + HY_BANDS] = f
    fv[0, 1 + HY_BANDS:1 + 2 * HY_BANDS] = f
    w1p = jnp.pad(w1, ((0, LANES - emb), (0, 0)))
    w3r = w3.reshape(ffn, 2, 2, hy_w)
    w3f = w3r[:, :, 0, :].reshape(ffn, cw)
    w3b = w3r[:, :, 1, :].reshape(ffn, cw)
    deltas = np.abs(np.linspace(math.log(HY_DECAY_TARGET) / HY_SLOW_PCT,
                                math.log(HY_DECAY_TARGET) / HY_FAST_PCT, hy_w)).astype(np.float32)
    dl = jnp.asarray(np.tile(deltas, 2).reshape(1, cw))
    tl = _pick_tile(seq_len, 512, 8)
    const = lambda i: (0, 0)
    kern = functools.partial(_hy_filter_kernel, seq_len=seq_len)
    return pl.pallas_call(
        kern, grid=(seq_len // tl,),
        in_specs=[pl.BlockSpec((1, LANES), const), pl.BlockSpec((LANES, ffn), const),
                  pl.BlockSpec((1, ffn), const), pl.BlockSpec((1, ffn), const),
                  pl.BlockSpec((ffn, ffn), const), pl.BlockSpec((1, ffn), const), pl.BlockSpec((1, ffn), const),
                  pl.BlockSpec((ffn, cw), const), pl.BlockSpec((ffn, cw), const), pl.BlockSpec((1, cw), const)],
        out_specs=[pl.BlockSpec((2, tl, cw), lambda i: (0, i, 0)), pl.BlockSpec((1, cw), const)],
        out_shape=[jax.ShapeDtypeStruct((2, seq_len, cw), F32), jax.ShapeDtypeStruct((1, cw), F32)],
        compiler_params=_cparams("arbitrary"),
        name="hy_filters",
    )(jnp.asarray(fv), w1p, b1.reshape(1, ffn), sin_freq[0].reshape(1, ffn), w2, b2.reshape(1, ffn),
      sin_freq[1].reshape(1, ffn), w3f, w3b, dl)


def _phase_mats(phase_num, denom, conj):
    ang = np.pi * (phase_num % (2 * denom)).astype(np.float64) / denom
    cr, ci = np.cos(ang), (np.sin(ang) if conj else -np.sin(ang))
    return cr, ci


def _stage1_mats(n1):
    k = np.arange(n1)[:, None]
    n = np.arange(n1)[None, :]
    w1d, w1f, wfin = [], [], []
    for v in (0, 1):
        cr, ci = _phase_mats(2 * k * n + v * n, n1, conj=False)
        w1d.append(np.block([[cr, -ci], [ci, cr]]))
        sgn = 1.0 if v == 0 else -1.0
        w1f.append(np.block([[cr, sgn * cr], [ci, sgn * ci]]))
        cri, cii = _phase_mats(2 * n.T * k.T + v * n.T, n1, conj=True)
        wfin.append(np.block([[cri, -cii], [cii, cri]]))
    return (np.concatenate(w1d, 0).astype(np.float32), np.concatenate(w1f, 0).astype(np.float32),
            np.concatenate(wfin, 1).astype(np.float32))


def _stage2_mats(n1, n2):
    l = n1 * n2
    k2 = np.arange(n2)[:, None]
    nn = np.arange(n2)[None, :]
    base = (2 * n1 * k2 * nn) % (2 * l)
    br, bi = np.cos(np.pi * base / l), -np.sin(np.pi * base / l)
    k1 = np.arange(n1)[None, :, None]
    v = np.arange(2)[:, None, None]
    tw = (2 * np.arange(n2)[None, None, :] * k1 + v * np.arange(n2)[None, None, :]) % (2 * l)
    tr, ti = np.cos(np.pi * tw / l), -np.sin(np.pi * tw / l)
    br, bi, tr, ti = (jnp.asarray(a, F32) for a in (br, bi, tr, ti))
    cr = br[None, None] * tr[:, :, None, :] - bi[None, None] * ti[:, :, None, :]
    ci = br[None, None] * ti[:, :, None, :] + bi[None, None] * tr[:, :, None, :]
    fwd = jnp.concatenate([jnp.concatenate([cr, -ci], -1), jnp.concatenate([ci, cr], -1)], -2)
    crt, cit = jnp.swapaxes(cr, -1, -2), -jnp.swapaxes(ci, -1, -2)
    inv = jnp.concatenate([jnp.concatenate([crt, -cit], -1), jnp.concatenate([cit, crt], -1)], -2)
    return fwd.astype(BF16), inv.astype(BF16)


def _fft_stage1_kernel(z_ref, w_ref, o_ref):
    n1 = z_ref.shape[1]
    x = z_ref[...].reshape(2 * n1, z_ref.shape[2]).astype(BF16)
    y = jnp.dot(w_ref[...], x, preferred_element_type=F32)
    o_ref[...] = y.reshape(o_ref.shape).astype(o_ref.dtype)


def fft_stage1(z, w1, n1, n2, col_blocks, col_off, n_chan_blocks):
    cw = 512
    zv = z.reshape(2, n1, n2 * col_blocks * cw)
    out = pl.pallas_call(
        _fft_stage1_kernel, grid=(n2, n_chan_blocks),
        in_specs=[pl.BlockSpec((2, n1, cw), lambda j, c: (0, 0, j * col_blocks + col_off + c)),
                  pl.BlockSpec((4 * n1, 2 * n1), lambda j, c: (0, 0))],
        out_specs=pl.BlockSpec((2, 2, n1, cw), lambda j, c: (0, 0, 0, j * n_chan_blocks + c)),
        out_shape=jax.ShapeDtypeStruct((2, 2, n1, n2 * n_chan_blocks * cw), BF16),
        compiler_params=_cparams("parallel", "parallel"),
        name="fft_stage1",
    )(zv, w1)
    return out.reshape(2, 2, n1, n2, n_chan_blocks * cw)


def _fft_spec_kernel(a_ref, wf_ref, ss_ref, o_ref, *, scale):
    n2 = a_ref.shape[1]
    x = a_ref[...].reshape(2 * n2, a_ref.shape[2])
    t = jnp.dot(wf_ref[...], x, preferred_element_type=F32)
    nrm = lax.rsqrt(ss_ref[...] + EPS) * scale
    o_ref[...] = (t * nrm).reshape(o_ref.shape)


def fft_spectrum(a, wf, ss, n1, n2):
    cw = 512
    nc = a.shape[-1] // cw
    kern = functools.partial(_fft_spec_kernel, scale=1.0 / (2.0 * n1 * n2))
    return pl.pallas_call(
        kern, grid=(2, n1, nc),
        in_specs=[pl.BlockSpec((None, 2, None, n2, cw), lambda v, k, c: (v, 0, k, 0, c)),
                  pl.BlockSpec((None, None, 2 * n2, 2 * n2), lambda v, k, c: (v, k, 0, 0)),
                  pl.BlockSpec((1, cw), lambda v, k, c: (0, c))],
        out_specs=pl.BlockSpec((None, 2, None, n2, cw), lambda v, k, c: (v, 0, k, 0, c)),
        out_shape=jax.ShapeDtypeStruct(a.shape, F32),
        compiler_params=_cparams("parallel", "parallel", "parallel"),
        name="fft_spectrum",
    )(a, wf, ss)


def _fft_mid_kernel(a_ref, wf_ref, wi_ref, g_ref, o_ref):
    n2 = a_ref.shape[1]
    x = a_ref[...].reshape(2 * n2, a_ref.shape[2])
    t = jnp.dot(wf_ref[...], x, preferred_element_type=F32)
    tr, ti = t[:n2], t[n2:]
    gr, gi = g_ref[0], g_ref[1]
    p = jnp.concatenate([tr * gr - ti * gi, tr * gi + ti * gr], axis=0).astype(BF16)
    u = jnp.dot(wi_ref[...], p, preferred_element_type=F32)
    o_ref[...] = u.reshape(o_ref.shape).astype(o_ref.dtype)


def fft_mid(a, wf, wi, g, order, n1, n2):
    cw = 512
    return pl.pallas_call(
        _fft_mid_kernel, grid=(2, n1),
        in_specs=[pl.BlockSpec((None, 2, None, n2, cw), lambda v, k: (v, 0, k, 0, 0)),
                  pl.BlockSpec((None, None, 2 * n2, 2 * n2), lambda v, k: (v, k, 0, 0)),
                  pl.BlockSpec((None, None, 2 * n2, 2 * n2), lambda v, k: (v, k, 0, 0)),
                  pl.BlockSpec((None, 2, None, n2, cw), lambda v, k: (v, 0, k, 0, order))],
        out_specs=pl.BlockSpec((None, 2, None, n2, cw), lambda v, k: (v, 0, k, 0, 0)),
        out_shape=jax.ShapeDtypeStruct(a.shape, BF16),
        compiler_params=_cparams("parallel", "parallel"),
        name="fft_mid",
    )(a, wf, wi, g)


def _fft_final_kernel(u_ref, w_ref, gate_ref, z_ref, bias_ref, o_ref):
    n1 = gate_ref.shape[1]
    cw = gate_ref.shape[2]
    u = u_ref[...].reshape(4 * n1, cw)
    y = jnp.dot(w_ref[...], u, preferred_element_type=F32).reshape(2, n1, cw)
    z = z_ref[...].astype(F32)
    o_ref[...] = (gate_ref[...].astype(F32) * (y + bias_ref[...] * z)).astype(o_ref.dtype)


def fft_final(u, wfin, gate_arr, gate_blocks, gate_off, z_arr, z_blocks, z_off, bias, n1, n2):
    cw = 512
    gv = gate_arr.reshape(2, n1, n2 * gate_blocks * cw)
    zv = z_arr.reshape(2, n1, n2 * z_blocks * cw)
    out = pl.pallas_call(
        _fft_final_kernel, grid=(n2,),
        in_specs=[pl.BlockSpec((2, 2, n1, cw), lambda j: (0, 0, 0, j)),
                  pl.BlockSpec((2 * n1, 4 * n1), lambda j: (0, 0)),
                  pl.BlockSpec((2, n1, cw), lambda j: (0, 0, j * gate_blocks + gate_off)),
                  pl.BlockSpec((2, n1, cw), lambda j: (0, 0, j * z_blocks + z_off)),
                  pl.BlockSpec((1, cw), lambda j: (0, 0))],
        out_specs=pl.BlockSpec((2, n1, cw), lambda j: (0, 0, j)),
        out_shape=jax.ShapeDtypeStruct((2, n1, n2 * cw), BF16),
        compiler_params=_cparams("parallel"),
        name="fft_final",
    )(u.reshape(2, 2, n1, n2 * cw), wfin, gv, zv, bias.reshape(1, cw))
    return out.reshape(2, n1 * n2, cw)


def _hy_dense_kernel(z_ref, gate_ref, f_ref, ss_ref, bias_ref, wf_ref, wi_ref, o_ref):
    lc = z_ref.shape[1]
    cw = z_ref.shape[2]
    nrm = lax.rsqrt(ss_ref[...] + EPS) * (1.0 / (2.0 * lc))
    f1, f2 = f_ref[0], f_ref[1]
    filt = (((f1 + f2) * nrm).astype(BF16), ((f1 - f2) * nrm).astype(BF16))
    x = z_ref[...].reshape(2 * lc, cw)
    acc = None
    for v in (0, 1):
        wf = wf_ref[v]
        t = jnp.dot(wf, x, preferred_element_type=F32)
        g = jnp.dot(wf[:, :lc], filt[v], preferred_element_type=F32)
        tr, ti, gr, gi = t[:lc], t[lc:], g[:lc], g[lc:]
        p = jnp.concatenate([tr * gr - ti * gi, tr * gi + ti * gr], axis=0).astype(BF16)
        u = jnp.dot(wi_ref[v], p, preferred_element_type=F32)
        acc = u if acc is None else acc + u
    y = acc.reshape(2, lc, cw)
    o_ref[...] = (gate_ref[...].astype(F32) * (y + bias_ref[...] * z_ref[...].astype(F32))).astype(o_ref.dtype)


def hy_dense_conv(z_arr, z_off, gate_arr, gate_off, f, ss, order, bias, wf, wi):
    cw = 512
    lc = z_arr.shape[1]
    return pl.pallas_call(
        _hy_dense_kernel, grid=(1,),
        in_specs=[pl.BlockSpec((2, lc, cw), lambda i: (0, 0, z_off)),
                  pl.BlockSpec((2, lc, cw), lambda i: (0, 0, gate_off)),
                  pl.BlockSpec((2, lc, cw), lambda i: (0, 0, order)),
                  pl.BlockSpec((1, cw), lambda i: (0, order)),
                  pl.BlockSpec((1, cw), lambda i: (0, 0)),
                  pl.BlockSpec((2, None, 2 * lc, 2 * lc), lambda i: (0, 0, 0, 0)),
                  pl.BlockSpec((2, None, 2 * lc, 2 * lc), lambda i: (0, 0, 0, 0))],
        out_specs=pl.BlockSpec((2, lc, cw), lambda i: (0, 0, 0)),
        out_shape=jax.ShapeDtypeStruct((2, lc, cw), BF16),
        compiler_params=_cparams("arbitrary"),
        name="hy_dense_conv",
    )(z_arr, gate_arr, f, ss, bias.reshape(1, cw), wf, wi)


def hyena_mixer(p, conv_w, conv_b, w1, b1, w2, b2, w3, sin_freq, bias, mats):
    b, l, c3 = p.shape
    assert b == 2, "batch elements are packed as the re/im parts of one complex signal"
    hy_w = c3 // 3
    assert hy_w == 512
    u = short_conv(p, conv_w, conv_b)
    f, ss = hy_filters(l, w1, b1, w2, b2, w3, sin_freq, hy_w)
    if l <= 2 * FFT_N2:
        wf, wi = mats["dense"]
        z2 = hy_dense_conv(u, 0, u, 1, f, ss, 0, bias[0], wf, wi)
        return hy_dense_conv(z2, 0, u, 2, f, ss, 1, bias[1], wf, wi)
    n2 = FFT_N2
    n1 = l // n2
    w1d, w1f, wfin, wf, wi = mats["fft"]
    g = fft_spectrum(fft_stage1(f, w1f, n1, n2, 2, 0, 2), wf, ss, n1, n2)
    a = fft_stage1(u, w1d, n1, n2, 3, 0, 1)
    z2 = fft_final(fft_mid(a, wf, wi, g, 0, n1, n2), wfin, u, 3, 1, u, 3, 0, bias[0], n1, n2)
    a = fft_stage1(z2, w1d, n1, n2, 1, 0, 1)
    return fft_final(fft_mid(a, wf, wi, g, 1, n1, n2), wfin, u, 3, 2, z2, 1, 0, bias[1], n1, n2)


def hyena_mats(l_lat, l_ctx):
    n1 = l_lat // FFT_N2
    w1d, w1f, wfin = _stage1_mats(n1)
    wf, wi = _stage2_mats(n1, FFT_N2)
    mats = {"fft": (jnp.asarray(w1d, BF16), jnp.asarray(w1f, BF16), jnp.asarray(wfin, BF16), wf, wi)}
    mats["dense"] = _stage2_mats(1, l_ctx)
    return mats


def _ctx_attn_kernel(q_ref, k_ref, v_ref, o_ref):
    s = lax.dot_general(q_ref[...], k_ref[...], NT_DIMS, preferred_element_type=F32)
    m = jnp.max(s, axis=-1, keepdims=True)
    p = jnp.exp(s - m)
    l = jnp.sum(p, axis=-1, keepdims=True)
    o = jnp.dot(p.astype(BF16), v_ref[...], preferred_element_type=F32) / l
    o_ref[...] = o.astype(o_ref.dtype)


def ctx_attention(qa, q_off, ka, k_off, va, v_off, n_q_heads, group):
    b, _, lc, d = qa.shape
    return pl.pallas_call(
        _ctx_attn_kernel, grid=(b, n_q_heads),
        in_specs=[pl.BlockSpec((None, None, lc, d), lambda bi, h: (bi, q_off + h, 0, 0)),
                  pl.BlockSpec((None, None, lc, d), lambda bi, h: (bi, k_off + h // group, 0, 0)),
                  pl.BlockSpec((None, None, lc, d), lambda bi, h: (bi, v_off + h // group, 0, 0))],
        out_specs=pl.BlockSpec((None, lc, d), lambda bi, h: (bi, 0, h)),
        out_shape=jax.ShapeDtypeStruct((b, lc, n_q_heads * d), BF16),
        compiler_params=_cparams("parallel", "parallel"),
        name="ctx_attention",
    )(qa, ka, va)


NA_TILE_ROWS = 4
NA_KEY_ROWS = 12


def _na_geometry(n_rows):
    wr = min(NA_WIN_R, n_rows)
    nt = n_rows // NA_TILE_ROWS
    sigs = []
    for t in range(nt):
        rt = t * NA_TILE_ROWS
        w0 = int(np.clip(rt - wr // 2, 0, n_rows - NA_KEY_ROWS))
        r = rt + np.arange(NA_TILE_ROWS)
        r0 = np.clip(r - wr // 2, 0, n_rows - wr)
        sigs.append((w0 - rt, tuple((r0 - w0).tolist())))
    classes = (sigs[0], sigs[1], sigs[-1])
    for t, s in enumerate(sigs):
        assert s == classes[0 if t == 0 else (2 if t == nt - 1 else 1)], "tile does not match its bias class"
    return wr, nt, classes


def _na_bias_tables(rpb, n_rows):
    wr, _, classes = _na_geometry(n_rows)
    cols = np.arange(GRID_W)
    c0 = np.clip(cols - NA_WIN_C // 2, 0, GRID_W - NA_WIN_C)
    col_ok = (cols[None, :] >= c0[:, None]) & (cols[None, :] < c0[:, None] + NA_WIN_C)
    col_idx = np.clip(cols[None, :] - cols[:, None] + NA_WIN_C - 1, 0, 2 * NA_WIN_C - 2)
    tables = []
    for off, rel_r0 in classes:
        qi = np.arange(NA_TILE_ROWS)[:, None]
        kw = np.arange(NA_KEY_ROWS)[None, :]
        r0 = np.asarray(rel_r0)[:, None]
        row_ok = (kw >= r0) & (kw < r0 + wr)
        row_idx = np.clip(kw + off - qi + NA_WIN_R - 1, 0, 2 * NA_WIN_R - 2)
        vals = rpb[:, row_idx[:, None, :, None], col_idx[None, :, None, :]].astype(F32)
        ok = row_ok[:, None, :, None] & col_ok[None, :, None, :]
        tables.append(jnp.where(jnp.asarray(ok)[None], vals, NEG_BIG))
    tab = jnp.stack(tables, axis=1)
    return tab.reshape(rpb.shape[0], 3, NA_TILE_ROWS * GRID_W, NA_KEY_ROWS * GRID_W)


def _na_kernel(q_ref, k_ref, v_ref, kc_ref, vc_ref, bias_ref, o_ref, *, n_rows, win_half):
    t = pl.program_id(2)
    w0 = jnp.clip(t * NA_TILE_ROWS - win_half, 0, n_rows - NA_KEY_ROWS)
    start = pl.multiple_of(w0 * GRID_W, GRID_W)
    kwin = k_ref[pl.ds(start, NA_KEY_ROWS * GRID_W), :]
    vwin = v_ref[pl.ds(start, NA_KEY_ROWS * GRID_W), :]
    q = q_ref[...]
    s_lat = lax.dot_general(q, kwin, NT_DIMS, preferred_element_type=F32) + bias_ref[...]
    s_ctx = lax.dot_general(q, kc_ref[...], NT_DIMS, preferred_element_type=F32)
    m = jnp.maximum(jnp.max(s_lat, axis=-1, keepdims=True), jnp.max(s_ctx, axis=-1, keepdims=True))
    p_lat = jnp.exp(s_lat - m)
    p_ctx = jnp.exp(s_ctx - m)
    l = jnp.sum(p_lat, axis=-1, keepdims=True) + jnp.sum(p_ctx, axis=-1, keepdims=True)
    o = (jnp.dot(p_lat.astype(BF16), vwin, preferred_element_type=F32)
         + jnp.dot(p_ctx.astype(BF16), vc_ref[...], preferred_element_type=F32)) / l
    o_ref[...] = o.astype(o_ref.dtype)


def na_attention(pl_heads, pc_heads, rpb):
    b, h3, l, d = pl_heads.shape
    lc = pc_heads.shape[2]
    h = h3 // 3
    n_rows = l // GRID_W
    wr, nt, _ = _na_geometry(n_rows)
    bias = _na_bias_tables(rpb, n_rows)
    tq = NA_TILE_ROWS * GRID_W
    kw = NA_KEY_ROWS * GRID_W
    kern = functools.partial(_na_kernel, n_rows=n_rows, win_half=wr // 2)

    def cls(t):
        return jnp.where(t == 0, 0, jnp.where(t == nt - 1, 2, 1))

    return pl.pallas_call(
        kern, grid=(b, h, nt),
        in_specs=[pl.BlockSpec((None, None, tq, d), lambda bi, hi, t: (bi, hi, t, 0)),
                  pl.BlockSpec((None, None, l, d), lambda bi, hi, t: (bi, h + hi, 0, 0)),
                  pl.BlockSpec((None, None, l, d), lambda bi, hi, t: (bi, 2 * h + hi, 0, 0)),
                  pl.BlockSpec((None, None, lc, d), lambda bi, hi, t: (bi, h + hi, 0, 0)),
                  pl.BlockSpec((None, None, lc, d), lambda bi, hi, t: (bi, 2 * h + hi, 0, 0)),
                  pl.BlockSpec((None, None, tq, kw), lambda bi, hi, t: (hi, cls(t), 0, 0))],
        out_specs=pl.BlockSpec((None, tq, d), lambda bi, hi, t: (bi, t, hi)),
        out_shape=jax.ShapeDtypeStruct((b, l, h * d), BF16),
        compiler_params=_cparams("parallel", "parallel", "parallel"),
        name="na_attention",
    )(pl_heads, pl_heads, pl_heads, pc_heads, pc_heads, bias)


def _ret_kernel(*refs, reverse, rope, finalize, n_heads):
    it = iter(refs)
    lg_ref, p_ref = next(it), next(it)
    cos_ref, sin_ref = (next(it), next(it)) if rope else (None, None)
    s0_ref = next(it)
    oprev_ref, gnw_ref = (next(it), next(it)) if finalize else (None, None)
    o_ref, sfin_ref, state_ref = next(it), next(it), next(it)

    i = pl.program_id(1)
    n_steps = pl.num_programs(1)

    @pl.when(i == 0)
    def _():
        state_ref[...] = s0_ref[...]

    c = RET_CHUNK
    ts = p_ref.shape[0]
    nc = ts // c
    dq = n_heads * RET_DK
    dv = n_heads * RET_DV
    x = p_ref[...]
    q = x[:, :dq].astype(F32)
    k = x[:, dq:2 * dq].astype(F32)
    v = x[:, 2 * dq:2 * dq + dv]
    if rope:
        q = q * cos_ref[...] + _swap32(q) * sin_ref[...]
        k = k * cos_ref[...] + _swap32(k) * sin_ref[...]
    jr = lax.broadcasted_iota(jnp.int32, (c, c), 0)
    jc = lax.broadcasted_iota(jnp.int32, (c, c), 1)
    rel = ((jc - jr) if reverse else (jr - jc)).astype(F32)
    jcol = lax.broadcasted_iota(jnp.int32, (c, 1), 0).astype(F32)
    order = range(nc - 1, -1, -1) if reverse else range(nc)
    blocks = [[None] * n_heads for _ in range(nc)]
    for h in range(n_heads):
        g = lg_ref[h]
        intra = jnp.where(rel >= 0.0, jnp.exp(g * jnp.maximum(rel, 0.0)), 0.0)
        if reverse:
            cross_f = jnp.exp(g * (float(c) - jcol))
            k_dec = jnp.exp(g * jcol)
        else:
            cross_f = jnp.exp(g * (jcol + 1.0))
            k_dec = jnp.exp(g * (float(c - 1) - jcol))
        chunk_decay = jnp.exp(g * float(c))
        state = state_ref[h]
        for ci in order:
            rows = slice(ci * c, (ci + 1) * c)
            qh = q[rows, h * RET_DK:(h + 1) * RET_DK].astype(BF16)
            kf = k[rows, h * RET_DK:(h + 1) * RET_DK]
            vh = v[rows, h * RET_DV:(h + 1) * RET_DV]
            scores = lax.dot_general(qh, kf.astype(BF16), NT_DIMS, preferred_element_type=F32) * intra
            inner = jnp.dot(scores.astype(BF16), vh, preferred_element_type=F32)
            cross = jnp.dot(qh, state.astype(BF16), preferred_element_type=F32) * cross_f
            blocks[ci][h] = inner + cross
            kv = lax.dot_general((kf * k_dec).astype(BF16), vh, TN_DIMS, preferred_element_type=F32)
            state = chunk_decay * state + kv
        state_ref[h] = state
    o = jnp.concatenate([jnp.concatenate(blocks[ci], axis=1) for ci in range(nc)], axis=0)
    if finalize:
        o = o + oprev_ref[...]
        gate = x[:, 2 * dq + dv:2 * dq + 2 * dv].astype(F32)
        normed = []
        for h in range(n_heads):
            oh = o[:, h * RET_DV:(h + 1) * RET_DV]
            normed.append(oh * lax.rsqrt(jnp.mean(oh * oh, axis=-1, keepdims=True) + EPS))
        o = jnp.concatenate(normed, axis=1) * gnw_ref[...] * _silu(gate)
    o_ref[...] = o.astype(o_ref.dtype)

    @pl.when(i == n_steps - 1)
    def _():
        sfin_ref[...] = state_ref[...]


def retention_pass(p, log_g, s0, reverse, rope_tables=None, o_prev=None, gn_w=None):
    b, l, w = p.shape
    n_heads = w // (2 * RET_DK + 2 * RET_DV)
    dq, dv = n_heads * RET_DK, n_heads * RET_DV
    ts = _pick_tile(l, 512, RET_CHUNK)
    n_steps = l // ts
    finalize = o_prev is not None
    rope = rope_tables is not None

    def tile(i):
        return (n_steps - 1 - i) if reverse else i

    in_specs = [pl.BlockSpec(memory_space=pltpu.SMEM),
                pl.BlockSpec((None, ts, w), lambda bi, i: (bi, tile(i), 0))]
    args = [log_g, p]
    if rope:
        in_specs += [pl.BlockSpec((ts, dq), lambda bi, i: (tile(i), 0))] * 2
        args += list(rope_tables)
    in_specs.append(pl.BlockSpec((None, n_heads, RET_DK, RET_DV), lambda bi, i: (bi, 0, 0, 0)))
    args.append(s0)
    if finalize:
        in_specs += [pl.BlockSpec((None, ts, dv), lambda bi, i: (bi, tile(i), 0)),
                     pl.BlockSpec((1, dv), lambda bi, i: (0, 0))]
        args += [o_prev, gn_w.reshape(1, dv)]
    kern = functools.partial(_ret_kernel, reverse=reverse, rope=rope, finalize=finalize, n_heads=n_heads)
    return pl.pallas_call(
        kern, grid=(b, n_steps),
        in_specs=in_specs,
        out_specs=[pl.BlockSpec((None, ts, dv), lambda bi, i: (bi, tile(i), 0)),
                   pl.BlockSpec((None, n_heads, RET_DK, RET_DV), lambda bi, i: (bi, 0, 0, 0))],
        out_shape=[jax.ShapeDtypeStruct((b, l, dv), BF16 if finalize else F32),
                   jax.ShapeDtypeStruct((b, n_heads, RET_DK, RET_DV), F32)],
        scratch_shapes=[pltpu.VMEM((n_heads, RET_DK, RET_DV), F32)],
        compiler_params=_cparams("parallel", "arbitrary"),
        name="retention_pass",
    )(*args)


def _ret_rope_tables(l, n_heads):
    half = RET_DK // 2
    inv = RET_ROPE_BASE ** (-jnp.linspace(0.0, 1.0, half, dtype=F32))
    ang = jnp.arange(l, dtype=F32)[:, None] * inv
    cos, sin = jnp.cos(ang), jnp.sin(ang)
    return (jnp.tile(jnp.concatenate([cos, cos], -1), (1, n_heads)),
            jnp.tile(jnp.concatenate([-sin, sin], -1), (1, n_heads)))


def retention_mixer(p_l, p_c, log_decay, gn_w, tables, with_ctx_out):
    b = p_l.shape[0]
    n_heads = p_l.shape[2] // (2 * RET_DK + 2 * RET_DV)
    log_g = -jnp.abs(log_decay.astype(F32))
    s0 = jnp.zeros((b, n_heads, RET_DK, RET_DV), F32)
    o_cf, s_fwd = retention_pass(p_c, log_g[0], s0, False)
    y_c, s_bwd = retention_pass(p_c, log_g[1], s0, True, o_prev=o_cf, gn_w=gn_w)
    o_lf, _ = retention_pass(p_l, log_g[0], s_fwd, False, rope_tables=tables)
    y_l, _ = retention_pass(p_l, log_g[1], s_bwd, True, rope_tables=tables, o_prev=o_lf, gn_w=gn_w)
    return y_l, (y_c if with_ctx_out else None)


def _gqa_prep_kernel(*refs, rope, n_q, n_kv, q_scale):
    if rope:
        p_ref, cos_ref, sin_ref, qw_ref, kw_ref, q_out, k_out, v_out = refs
    else:
        p_ref, qw_ref, kw_ref, q_out, k_out, v_out = refs
    x = p_ref[...]
    d = HEAD_DIM

    def norm_rope(xh, w):
        xh = xh.astype(F32)
        y = xh * lax.rsqrt(jnp.mean(xh * xh, axis=-1, keepdims=True) + EPS) * w
        if rope:
            y = y * cos_ref[...] + _swap32(y) * sin_ref[...]
        return y

    for h in range(n_q):
        q_out[h] = (norm_rope(x[:, h * d:(h + 1) * d], qw_ref[...]) * q_scale).astype(q_out.dtype)
    for h in range(n_kv):
        k_out[h] = norm_rope(x[:, (n_q + h) * d:(n_q + h + 1) * d], kw_ref[...]).astype(k_out.dtype)
        v_out[h] = x[:, (n_q + n_kv + h) * d:(n_q + n_kv + h + 1) * d]


def gqa_prep(p, qn_w, kn_w, n_q, n_kv, rope_tables=None):
    b, l, w = p.shape
    d = HEAD_DIM
    t = _pick_tile(l, 512, 16)
    rope = rope_tables is not None
    in_specs = [pl.BlockSpec((None, t, w), lambda bi, i: (bi, i, 0))]
    args = [p]
    if rope:
        in_specs += [pl.BlockSpec((t, d), lambda bi, i: (i, 0))] * 2
        args += list(rope_tables)
    in_specs += [pl.BlockSpec((1, d), lambda bi, i: (0, 0))] * 2
    args += [qn_w.reshape(1, d), kn_w.reshape(1, d)]
    kern = functools.partial(_gqa_prep_kernel, rope=rope, n_q=n_q, n_kv=n_kv, q_scale=d ** -0.5)
    return pl.pallas_call(
        kern, grid=(b, l // t),
        in_specs=in_specs,
        out_specs=[pl.BlockSpec((None, n_q, t, d), lambda bi, i: (bi, 0, i, 0)),
                   pl.BlockSpec((None, n_kv, t, d), lambda bi, i: (bi, 0, i, 0)),
                   pl.BlockSpec((None, n_kv, t, d), lambda bi, i: (bi, 0, i, 0))],
        out_shape=[jax.ShapeDtypeStruct((b, n_q, l, d), BF16),
                   jax.ShapeDtypeStruct((b, n_kv, l, d), BF16),
                   jax.ShapeDtypeStruct((b, n_kv, l, d), BF16)],
        compiler_params=_cparams("parallel", "parallel"),
        name="gqa_prep",
    )(*args)


def _axial_rope_tables(l):
    nf = HEAD_DIM // 4
    t = jnp.arange(l)
    inv = ROPE_BASE ** (-jnp.arange(nf, dtype=F32) / nf)
    ang_r = (t // GRID_W).astype(F32)[:, None] * inv
    ang_c = (t % GRID_W).astype(F32)[:, None] * inv
    cr, sr, cc, sc = jnp.cos(ang_r), jnp.sin(ang_r), jnp.cos(ang_c), jnp.sin(ang_c)
    return (jnp.concatenate([cr, cr, cc, cc], -1), jnp.concatenate([-sr, sr, -sc, sc], -1))


def _flash_kernel(q_ref, k_ref, v_ref, o_ref, m_ref, l_ref, acc_ref):
    j = pl.program_id(3)
    g, tq, d = q_ref.shape

    @pl.when(j == 0)
    def _():
        m_ref[...] = jnp.full(m_ref.shape, NEG_BIG, F32)
        l_ref[...] = jnp.zeros(l_ref.shape, F32)
        acc_ref[...] = jnp.zeros(acc_ref.shape, F32)

    q = q_ref[...].reshape(g * tq, d)
    s = lax.dot_general(q, k_ref[...], NT_DIMS, preferred_element_type=F32)
    m_prev = m_ref[...]
    m_new = jnp.maximum(m_prev, jnp.max(s, axis=-1, keepdims=True))
    alpha = jnp.exp(m_prev - m_new)
    p = jnp.exp(s - m_new)
    l_ref[...] = alpha * l_ref[...] + jnp.sum(p, axis=-1, keepdims=True)
    acc_ref[...] = alpha * acc_ref[...] + jnp.dot(p.astype(BF16), v_ref[...], preferred_element_type=F32)
    m_ref[...] = m_new

    @pl.when(j == pl.num_programs(3) - 1)
    def _():
        o = acc_ref[...] / l_ref[...]
        o_ref[...] = jnp.concatenate([o[h * tq:(h + 1) * tq] for h in range(g)], axis=1).astype(o_ref.dtype)


def flash_gqa(q, k, v):
    b, hq, l, d = q.shape
    hkv, lk = k.shape[1], k.shape[2]
    g = hq // hkv
    tq = _pick_tile(l, 512, 16)
    tk = _pick_tile(lk, 1664, LANES)
    return pl.pallas_call(
        _flash_kernel, grid=(b, hkv, l // tq, lk // tk),
        in_specs=[pl.BlockSpec((None, g, tq, d), lambda bi, h, i, j: (bi, h, i, 0)),
                  pl.BlockSpec((None, None, tk, d), lambda bi, h, i, j: (bi, h, j, 0)),
                  pl.BlockSpec((None, None, tk, d), lambda bi, h, i, j: (bi, h, j, 0))],
        out_specs=pl.BlockSpec((None, tq, g * d), lambda bi, h, i, j: (bi, i, h)),
        out_shape=jax.ShapeDtypeStruct((b, l, hq * d), BF16),
        scratch_shapes=[pltpu.VMEM((g * tq, 1), F32), pltpu.VMEM((g * tq, 1), F32),
                        pltpu.VMEM((g * tq, d), F32)],
        compiler_params=_cparams("parallel", "parallel", "parallel", "arbitrary"),
        name="flash_gqa",
    )(q, k, v)


def gqa_mixer(p_l, p_c, qn_w, kn_w, tables, with_ctx_out):
    n_q = p_l.shape[2] // (2 * HEAD_DIM)
    n_kv = n_q // 2
    q_l, k_l, v_l = gqa_prep(p_l, qn_w, kn_w, n_q, n_kv, rope_tables=tables)
    q_c, k_c, v_c = gqa_prep(p_c, qn_w, kn_w, n_q, n_kv)
    y_l = flash_gqa(q_l, jnp.concatenate([k_l, k_c], axis=2), jnp.concatenate([v_l, v_c], axis=2))
    y_c = ctx_attention(q_c, 0, k_c, 0, v_c, 0, n_q, n_q // n_kv) if with_ctx_out else None
    return y_l, y_c


def _split_w_in(w_in, d_model):
    mix_w = d_model // N_BRANCH
    h = mix_w // HEAD_DIM
    sizes = (3 * mix_w, 3 * h * HEAD_DIM, 2 * h * RET_DK + 2 * h * RET_DV,
             (h + 2 * max(h // 2, 1)) * HEAD_DIM, N_BRANCH * d_model)
    offs = np.cumsum((0,) + sizes)
    w_hy, w_na, w_rt, w_gq, w_gate = (w_in[:, offs[i]:offs[i + 1]] for i in range(5))
    na_scale = jnp.concatenate([jnp.full((h * HEAD_DIM,), HEAD_DIM ** -0.5, F32),
                                jnp.ones((2 * h * HEAD_DIM,), F32)])
    rt_scale = jnp.concatenate([jnp.ones((h * RET_DK,), F32), jnp.full((h * RET_DK,), RET_DK ** -0.5, F32),
                                jnp.ones((2 * h * RET_DV,), F32)])
    return ((w_hy).astype(BF16), (w_na * na_scale).astype(BF16), (w_rt * rt_scale).astype(BF16),
            w_gq.astype(BF16), w_gate.astype(BF16))


def kernel(x, c, ctx, c_ctx, ada_w, ada_b, norm1_w, norm2_w, w_in, hy_conv_w, hy_conv_b, hy_ffn_w1, hy_ffn_b1,
           hy_ffn_w2, hy_ffn_b2, hy_ffn_w3, hy_sin_freq, hy_bias, na_rpb, ret_log_decay, ret_gn_w, gqa_q_norm_w,
           gqa_k_norm_w, w_branch, w_out, ffn_w13, ffn_w2, final_norm_w):
    b, l, d = x.shape
    lc = ctx.shape[1]
    depth = ada_w.shape[0]
    ffn_hidden = ffn_w2.shape[1]
    n_ret_heads = (d // N_BRANCH) // RET_DV

    cs = jnp.zeros((8, d), F32).at[:b].set(c).at[b].set(c_ctx)
    mods = ada_mod(cs, ada_w, ada_b)
    ctx_row = b
    mats = hyena_mats(l, lc)
    ret_tables = _ret_rope_tables(l, n_ret_heads)
    gqa_tables = _axial_rope_tables(l)

    x_l = x.reshape(b * l, d)
    x_c = ctx.reshape(b * lc, d)
    for layer in range(depth):
        last = layer == depth - 1
        mod = mods[layer]
        w_hy, w_na, w_rt, w_gq, w_gate = _split_w_in(w_in[layer], d)
        hy_params = (hy_conv_w[layer], hy_conv_b[layer], hy_ffn_w1[layer], hy_ffn_b1[layer], hy_ffn_w2[layer],
                     hy_ffn_b2[layer], hy_ffn_w3[layer], hy_sin_freq[layer], hy_bias[layer])
        wb = w_branch[layer].astype(BF16)
        wo = w_out[layer].astype(BF16)
        w1 = ffn_w13[layer][:, :ffn_hidden].astype(BF16)
        w3 = ffn_w13[layer][:, ffn_hidden:].astype(BF16)
        w2 = ffn_w2[layer].astype(BF16)

        h_l = norm_mod(x_l, norm1_w[layer], mod, 0, 1, l, 0)
        h_c = norm_mod(x_c, norm1_w[layer], mod, 0, 1, None, ctx_row)

        na_l = matmul_heads(h_l, w_na, b)
        na_c = matmul_heads(h_c, w_na, b)
        rt_l = matmul(h_l, w_rt).reshape(b, l, -1)
        rt_c = matmul(h_c, w_rt).reshape(b, lc, -1)
        gq_l = matmul(h_l, w_gq).reshape(b, l, -1)
        gq_c = matmul(h_c, w_gq).reshape(b, lc, -1)
        hy_l = matmul(h_l, w_hy).reshape(b, l, -1)
        gate_l = matmul(h_l, w_gate)

        y_hy_l = hyena_mixer(hy_l, *hy_params, mats)
        y_na_l = na_attention(na_l, na_c, na_rpb[layer])
        y_rt_l, y_rt_c = retention_mixer(rt_l, rt_c, ret_log_decay[layer], ret_gn_w[layer], ret_tables, not last)
        y_gq_l, y_gq_c = gqa_mixer(gq_l, gq_c, gqa_q_norm_w[layer], gqa_k_norm_w[layer], gqa_tables, not last)

        m_l = merge_branches([y.reshape(b * l, -1) for y in (y_hy_l, y_na_l, y_rt_l, y_gq_l)], gate_l, wb)
        x_l = matmul_residual(m_l, wo, x_l, mod, 2, l, 0)
        h2 = norm_mod(x_l, norm2_w[layer], mod, 3, 4, l, 0)
        x_l = matmul_residual(matmul_swiglu(h2, w1, w3), w2, x_l, mod, 5, l, 0)

        if not last:
            n_na = na_c.shape[1] // 3
            hy_c = matmul(h_c, w_hy).reshape(b, lc, -1)
            gate_c = matmul(h_c, w_gate)
            y_hy_c = hyena_mixer(hy_c, *hy_params, mats)
            y_na_c = ctx_attention(na_c, 0, na_c, n_na, na_c, 2 * n_na, n_na, 1)
            m_c = merge_branches([y.reshape(b * lc, -1) for y in (y_hy_c, y_na_c, y_rt_c, y_gq_c)], gate_c, wb)
            x_c = matmul_residual(m_c, wo, x_c, mod, 2, None, ctx_row)
            h2c = norm_mod(x_c, norm2_w[layer], mod, 3, 4, None, ctx_row)
            x_c = matmul_residual(matmul_swiglu(h2c, w1, w3), w2, x_c, mod, 5, None, ctx_row)

    return final_norm(x_l, final_norm_w).reshape(b, l, d)
```

```python
import functools
import math

import numpy as np
import jax
import jax.numpy as jnp
from jax import lax
from jax.experimental import pallas as pl
from jax.experimental.pallas import tpu as pltpu

F32 = jnp.float32
BF16 = jnp.bfloat16

EPS = 1e-6
GRID_W = 64
HEAD_DIM = 128
N_BRANCH = 4
ROPE_BASE = 10000.0
HY_BANDS = 16
HY_DECAY_TARGET = 1e-2
HY_FAST_PCT = 0.3
HY_SLOW_PCT = 1.5
NA_WIN_R = 8
NA_WIN_C = 16
RET_DK = 64
RET_DV = 128
RET_CHUNK = 128
RET_ROPE_BASE = 10000.0
NEG_BIG = -1e30

LANES = 128
FFT_N2 = 128
VMEM_LIMIT = 56 * 1024 * 1024

NT_DIMS = (((1,), (1,)), ((), ()))
TN_DIMS = (((0,), (0,)), ((), ()))


def _cparams(*sem):
    return pltpu.CompilerParams(dimension_semantics=sem, vmem_limit_bytes=VMEM_LIMIT)


def _pick_tile(n, cap, mult):
    best = None
    for t in range(mult, min(n, cap) + 1, mult):
        if n % t == 0:
            best = t
    assert best is not None, (n, cap, mult)
    return best


def _swap32(x):
    n = x.shape[-1]
    lane = lax.broadcasted_iota(jnp.int32, x.shape, x.ndim - 1)
    up = pltpu.roll(x, n - 32, x.ndim - 1)
    down = pltpu.roll(x, 32, x.ndim - 1)
    return jnp.where((lane % 64) < 32, up, down)


def _silu(x):
    return x * jax.nn.sigmoid(x)


def _ada_kernel(c_ref, w_ref, b_ref, o_ref):
    a = _silu(c_ref[...]).astype(BF16)
    o_ref[...] = jnp.dot(a, w_ref[...].astype(BF16), preferred_element_type=F32) + b_ref[...]


def ada_mod(cs, ada_w, ada_b):
    depth, d, n = ada_w.shape
    tn = _pick_tile(n, 1536, LANES)
    return pl.pallas_call(
        _ada_kernel,
        grid=(depth, n // tn),
        in_specs=[pl.BlockSpec((8, d), lambda l, j: (0, 0)),
                  pl.BlockSpec((None, d, tn), lambda l, j: (l, 0, j)),
                  pl.BlockSpec((None, 1, tn), lambda l, j: (l, 0, j))],
        out_specs=pl.BlockSpec((None, 8, tn), lambda l, j: (l, 0, j)),
        out_shape=jax.ShapeDtypeStruct((depth, 8, n), F32),
        compiler_params=_cparams("parallel", "parallel"),
        name="ada_mod",
    )(cs, ada_w, ada_b.reshape(depth, 1, n))


def _mod_row(row_base, tiles_per_batch, axis):
    if tiles_per_batch is None:
        return row_base
    return row_base + pl.program_id(axis) // tiles_per_batch


def _norm_mod_kernel(x_ref, w_ref, sh_ref, sc_ref, o_ref, *, row_base, tiles_per_batch):
    x = x_ref[...]
    y = x * lax.rsqrt(jnp.mean(x * x, axis=-1, keepdims=True) + EPS) * w_ref[...]
    row = _mod_row(row_base, tiles_per_batch, 0)
    sh = sh_ref[pl.ds(row, 1), :]
    sc = sc_ref[pl.ds(row, 1), :]
    o_ref[...] = (y * (1.0 + sc) + sh).astype(o_ref.dtype)


def _norm_kernel(x_ref, w_ref, o_ref):
    x = x_ref[...]
    y = x * lax.rsqrt(jnp.mean(x * x, axis=-1, keepdims=True) + EPS) * w_ref[...]
    o_ref[...] = y.astype(o_ref.dtype)


def norm_mod(x, w, mod, sh_chunk, sc_chunk, rows_per_batch, row_base):
    m, d = x.shape
    tm = _pick_tile(m if rows_per_batch is None else rows_per_batch, 512, 8)
    tpb = None if rows_per_batch is None else rows_per_batch // tm
    kern = functools.partial(_norm_mod_kernel, row_base=row_base, tiles_per_batch=tpb)
    return pl.pallas_call(
        kern, grid=(m // tm,),
        in_specs=[pl.BlockSpec((tm, d), lambda i: (i, 0)),
                  pl.BlockSpec((1, d), lambda i: (0, 0)),
                  pl.BlockSpec((8, d), lambda i: (0, sh_chunk)),
                  pl.BlockSpec((8, d), lambda i: (0, sc_chunk))],
        out_specs=pl.BlockSpec((tm, d), lambda i: (i, 0)),
        out_shape=jax.ShapeDtypeStruct((m, d), BF16),
        compiler_params=_cparams("parallel"),
        name="norm_mod",
    )(x, w.reshape(1, d), mod, mod)


def final_norm(x, w):
    m, d = x.shape
    tm = _pick_tile(m, 512, 8)
    return pl.pallas_call(
        _norm_kernel, grid=(m // tm,),
        in_specs=[pl.BlockSpec((tm, d), lambda i: (i, 0)), pl.BlockSpec((1, d), lambda i: (0, 0))],
        out_specs=pl.BlockSpec((tm, d), lambda i: (i, 0)),
        out_shape=jax.ShapeDtypeStruct((m, d), F32),
        compiler_params=_cparams("parallel"),
        name="final_norm",
    )(x, w.reshape(1, d))


def _mm_kernel(a_ref, w_ref, o_ref):
    o_ref[...] = jnp.dot(a_ref[...], w_ref[...], preferred_element_type=F32).astype(o_ref.dtype)


def _mm_heads_kernel(a_ref, w_ref, o_ref):
    acc = jnp.dot(a_ref[...], w_ref[...], preferred_element_type=F32)
    for h in range(o_ref.shape[0]):
        o_ref[h] = acc[:, h * HEAD_DIM:(h + 1) * HEAD_DIM].astype(o_ref.dtype)


def _mm_swiglu_kernel(a_ref, w1_ref, w3_ref, o_ref):
    a = a_ref[...]
    u = jnp.dot(a, w1_ref[...], preferred_element_type=F32)
    g = jnp.dot(a, w3_ref[...], preferred_element_type=F32)
    o_ref[...] = (_silu(u) * g).astype(o_ref.dtype)


def _mm_residual_kernel(a_ref, w_ref, x_ref, g_ref, o_ref, *, row_base, tiles_per_batch):
    acc = jnp.dot(a_ref[...], w_ref[...], preferred_element_type=F32)
    row = _mod_row(row_base, tiles_per_batch, 1)
    o_ref[...] = x_ref[...] + g_ref[pl.ds(row, 1), :] * acc


def matmul(a, w, out_dtype=BF16):
    m, k = a.shape
    n = w.shape[1]
    tm = _pick_tile(m, 512, 16)
    tn = _pick_tile(n, 2048, LANES)
    return pl.pallas_call(
        _mm_kernel, grid=(n // tn, m // tm),
        in_specs=[pl.BlockSpec((tm, k), lambda j, i: (i, 0)), pl.BlockSpec((k, tn), lambda j, i: (0, j))],
        out_specs=pl.BlockSpec((tm, tn), lambda j, i: (i, j)),
        out_shape=jax.ShapeDtypeStruct((m, n), out_dtype),
        compiler_params=_cparams("parallel", "parallel"),
        name="matmul",
    )(a, w)


def matmul_heads(a, w, batch):
    m, k = a.shape
    n = w.shape[1]
    lb = m // batch
    tm = _pick_tile(lb, 512, 16)
    tn = _pick_tile(n, 1536, LANES)
    tpb = lb // tm
    nh = tn // HEAD_DIM
    return pl.pallas_call(
        _mm_heads_kernel, grid=(n // tn, m // tm),
        in_specs=[pl.BlockSpec((tm, k), lambda j, i: (i, 0)), pl.BlockSpec((k, tn), lambda j, i: (0, j))],
        out_specs=pl.BlockSpec((None, nh, tm, HEAD_DIM), lambda j, i: (i // tpb, j, i % tpb, 0)),
        out_shape=jax.ShapeDtypeStruct((batch, n // HEAD_DIM, lb, HEAD_DIM), BF16),
        compiler_params=_cparams("parallel", "parallel"),
        name="matmul_heads",
    )(a, w)


def matmul_swiglu(a, w1, w3):
    m, k = a.shape
    n = w1.shape[1]
    tm = _pick_tile(m, 512, 16)
    tn = _pick_tile(n, 1408, LANES)
    return pl.pallas_call(
        _mm_swiglu_kernel, grid=(n // tn, m // tm),
        in_specs=[pl.BlockSpec((tm, k), lambda j, i: (i, 0)),
                  pl.BlockSpec((k, tn), lambda j, i: (0, j)),
                  pl.BlockSpec((k, tn), lambda j, i: (0, j))],
        out_specs=pl.BlockSpec((tm, tn), lambda j, i: (i, j)),
        out_shape=jax.ShapeDtypeStruct((m, n), BF16),
        compiler_params=_cparams("parallel", "parallel"),
        name="matmul_swiglu",
    )(a, w1, w3)


def matmul_residual(a, w, x, mod, gate_chunk, rows_per_batch, row_base):
    m, k = a.shape
    n = w.shape[1]
    tm = _pick_tile(m if rows_per_batch is None else rows_per_batch, 512, 16)
    tn = _pick_tile(n, 1024 if k <= 2048 else 512, LANES)
    tpb = None if rows_per_batch is None else rows_per_batch // tm
    kern = functools.partial(_mm_residual_kernel, row_base=row_base, tiles_per_batch=tpb)
    gblk = gate_chunk * (n // tn)
    return pl.pallas_call(
        kern, grid=(n // tn, m // tm),
        in_specs=[pl.BlockSpec((tm, k), lambda j, i: (i, 0)),
                  pl.BlockSpec((k, tn), lambda j, i: (0, j)),
                  pl.BlockSpec((tm, tn), lambda j, i: (i, j)),
                  pl.BlockSpec((8, tn), lambda j, i: (0, gblk + j))],
        out_specs=pl.BlockSpec((tm, tn), lambda j, i: (i, j)),
        out_shape=jax.ShapeDtypeStruct((m, n), F32),
        compiler_params=_cparams("parallel", "parallel"),
        name="matmul_residual",
    )(a, w, x, mod)


def _merge_kernel(y0, y1, y2, y3, g0, g1, g2, g3, wb_ref, o_ref):
    acc = None
    for i, (y, g) in enumerate(((y0, g0), (y1, g1), (y2, g2), (y3, g3))):
        t = jax.nn.sigmoid(g[...].astype(F32)) * jnp.dot(y[...], wb_ref[i], preferred_element_type=F32)
        acc = t if acc is None else acc + t
    o_ref[...] = acc.astype(o_ref.dtype)


def merge_branches(ys, gate_pre, wb):
    m, w = ys[0].shape
    d = wb.shape[2]
    tm = _pick_tile(m, 512, 16)
    tn = _pick_tile(d, 1024, LANES)
    nb = d // tn
    y_spec = pl.BlockSpec((tm, w), lambda j, i: (i, 0))
    g_specs = [pl.BlockSpec((tm, tn), functools.partial(lambda j, i, b: (i, b * nb + j), b=b))
               for b in range(N_BRANCH)]
    return pl.pallas_call(
        _merge_kernel, grid=(nb, m // tm),
        in_specs=[y_spec] * N_BRANCH + g_specs + [pl.BlockSpec((N_BRANCH, w, tn), lambda j, i: (0, 0, j))],
        out_specs=pl.BlockSpec((tm, tn), lambda j, i: (i, j)),
        out_shape=jax.ShapeDtypeStruct((m, d), BF16),
        compiler_params=_cparams("parallel", "parallel"),
        name="merge_branches",
    )(*ys, gate_pre, gate_pre, gate_pre, gate_pre, wb)


def _short_conv_kernel(x_ref, prev_ref, next_ref, w_ref, b_ref, *o_refs, n_tiles):
    i = pl.program_id(1)
    x = x_ref[...].astype(F32)
    t = x.shape[0]
    row = lax.broadcasted_iota(jnp.int32, x.shape, 0)
    halo = prev_ref.shape[0]
    before = jnp.where(i > 0, prev_ref[...].astype(F32)[halo - 1:halo, :], 0.0)
    after = jnp.where(i < n_tiles - 1, next_ref[...].astype(F32)[0:1, :], 0.0)
    xm1 = jnp.where(row == 0, before, pltpu.roll(x, 1, 0))
    xp1 = jnp.where(row == t - 1, after, pltpu.roll(x, t - 1, 0))
    w = w_ref[...]
    y = w[0:1, :] * xm1 + w[1:2, :] * x + w[2:3, :] * xp1 + b_ref[...]
    cw = o_refs[0].shape[-1]
    for n, o_ref in enumerate(o_refs):
        o_ref[...] = y[:, n * cw:(n + 1) * cw].astype(o_ref.dtype)


def short_conv(p, conv_w, conv_b, n_out):
    b, l, c = p.shape
    cw = c // n_out
    halo = 16
    t = _pick_tile(l, 512, halo)
    n_tiles = l // t
    hb = t // halo
    kern = functools.partial(_short_conv_kernel, n_tiles=n_tiles)
    return pl.pallas_call(
        kern, grid=(b, n_tiles),
        in_specs=[pl.BlockSpec((None, t, c), lambda bi, i: (bi, i, 0)),
                  pl.BlockSpec((None, halo, c), lambda bi, i: (bi, jnp.maximum(i * hb - 1, 0), 0)),
                  pl.BlockSpec((None, halo, c), lambda bi, i: (bi, jnp.minimum((i + 1) * hb, l // halo - 1), 0)),
                  pl.BlockSpec((8, c), lambda bi, i: (0, 0)),
                  pl.BlockSpec((1, c), lambda bi, i: (0, 0))],
        out_specs=[pl.BlockSpec((None, t, cw), lambda bi, i: (bi, i, 0))] * n_out,
        out_shape=[jax.ShapeDtypeStruct((b, l, cw), BF16)] * n_out,
        compiler_params=_cparams("parallel", "parallel"),
        name="short_conv",
    )(p, p, p, jnp.pad(conv_w, ((0, 8 - conv_w.shape[0]), (0, 0))), conv_b.reshape(1, c))


def _hy_filter_kernel(fv_ref, c2_ref, s2_ref, w1_ref, b1_ref, sf1_ref, w2_ref, b2_ref, sf2_ref, w3f_ref, w3b_ref,
                      dl_ref, f_ref, ss_ref, cb_ref, sb_ref, *, seq_len):
    i = pl.program_id(0)
    tl = f_ref.shape[1]
    hi = lax.Precision.HIGHEST
    step = 2.0 * math.pi / float(seq_len)
    r = lax.broadcasted_iota(jnp.int32, (tl, 1), 0)
    lane = lax.broadcasted_iota(jnp.int32, (1, LANES), 1)

    @pl.when(i == 0)
    def _():
        ang_r = (step * r.astype(F32)) * fv_ref[...]
        cb_ref[...] = jnp.cos(ang_r)
        sb_ref[...] = jnp.sin(ang_r)
        ss_ref[...] = jnp.zeros(ss_ref.shape, F32)

    base = (step * (i * tl).astype(F32)) * fv_ref[...]
    ca, sa = jnp.cos(base), jnp.sin(base)
    cb, sb = cb_ref[...], sb_ref[...]
    cos_j, sin_j = ca * cb - sa * sb, sa * cb + ca * sb
    cos_m, sin_m = c2_ref[...] * cos_j + s2_ref[...] * sin_j, s2_ref[...] * cos_j - c2_ref[...] * sin_j
    j = (i * tl + r).astype(F32)

    def mlp(t, cos_t, sin_t):
        t_norm = t / float(max(seq_len - 1, 1))
        feat = jnp.where(lane == 0, t_norm,
                         jnp.where(lane <= HY_BANDS, cos_t, jnp.where(lane <= 2 * HY_BANDS, -sin_t, 0.0)))
        z = jnp.sin(sf1_ref[...] * (jnp.dot(feat, w1_ref[...], precision=hi, preferred_element_type=F32)
                                    + b1_ref[...]))
        z = jnp.sin(sf2_ref[...] * (jnp.dot(z, w2_ref[...], precision=hi, preferred_element_type=F32)
                                    + b2_ref[...]))
        return z.astype(BF16), t_norm

    zf, tnf = mlp(j, cos_j, sin_j)
    hf = jnp.dot(zf, w3f_ref[...], preferred_element_type=F32) * jnp.exp(-tnf * dl_ref[...])
    zb, tnb = mlp(float(seq_len) - j, cos_m, sin_m)
    hb = jnp.dot(zb, w3b_ref[...], preferred_element_type=F32) * jnp.exp(-tnb * dl_ref[...])
    hb = jnp.where(j > 0.0, hb, 0.0)
    f_ref[0] = hf
    f_ref[1] = hb
    ss_ref[...] += (jnp.sum(hf * hf, axis=0, keepdims=True) + jnp.sum(hb * hb, axis=0, keepdims=True))


def hy_filters(seq_len, w1, b1, w2, b2, w3, sin_freq, hy_w):
    emb, ffn = w1.shape
    cw = 2 * hy_w
    f = np.linspace(1e-4, HY_BANDS - 1, HY_BANDS)
    fv = np.zeros((1, LANES), np.float32)
    fv[0, 1:1 + HY_BANDS] = f
    fv[0, 1 + HY_BANDS:1 + 2 * HY_BANDS] = f
    c2 = np.cos(2.0 * np.pi * fv.astype(np.float64)).astype(np.float32)
    s2 = np.sin(2.0 * np.pi * fv.astype(np.float64)).astype(np.float32)
    w1p = jnp.pad(w1, ((0, LANES - emb), (0, 0)))
    w3r = w3.reshape(ffn, 2, 2, hy_w)
    w3f = w3r[:, :, 0, :].reshape(ffn, cw).astype(BF16)
    w3b = w3r[:, :, 1, :].reshape(ffn, cw).astype(BF16)
    deltas = np.abs(np.linspace(math.log(HY_DECAY_TARGET) / HY_SLOW_PCT,
                                math.log(HY_DECAY_TARGET) / HY_FAST_PCT, hy_w)).astype(np.float32)
    dl = jnp.asarray(np.tile(deltas, 2).reshape(1, cw))
    tl = _pick_tile(seq_len, 512, 8)
    const = lambda i: (0, 0)
    kern = functools.partial(_hy_filter_kernel, seq_len=seq_len)
    return pl.pallas_call(
        kern, grid=(seq_len // tl,),
        in_specs=[pl.BlockSpec((1, LANES), const), pl.BlockSpec((1, LANES), const), pl.BlockSpec((1, LANES), const),
                  pl.BlockSpec((LANES, ffn), const),
                  pl.BlockSpec((1, ffn), const), pl.BlockSpec((1, ffn), const),
                  pl.BlockSpec((ffn, ffn), const), pl.BlockSpec((1, ffn), const), pl.BlockSpec((1, ffn), const),
                  pl.BlockSpec((ffn, cw), const), pl.BlockSpec((ffn, cw), const), pl.BlockSpec((1, cw), const)],
        out_specs=[pl.BlockSpec((2, tl, cw), lambda i: (0, i, 0)), pl.BlockSpec((1, cw), const)],
        out_shape=[jax.ShapeDtypeStruct((2, seq_len, cw), F32), jax.ShapeDtypeStruct((1, cw), F32)],
        scratch_shapes=[pltpu.VMEM((tl, LANES), F32), pltpu.VMEM((tl, LANES), F32)],
        compiler_params=_cparams("arbitrary"),
        name="hy_filters",
    )(jnp.asarray(fv), jnp.asarray(c2), jnp.asarray(s2), w1p, b1.reshape(1, ffn), sin_freq[0].reshape(1, ffn), w2,
      b2.reshape(1, ffn), sin_freq[1].reshape(1, ffn), w3f, w3b, dl)


def _phase_mats(phase_num, denom, conj):
    ang = np.pi * (phase_num % (2 * denom)).astype(np.float64) / denom
    cr, ci = np.cos(ang), (np.sin(ang) if conj else -np.sin(ang))
    return cr, ci


def _stage1_mats(n1):
    k = np.arange(n1)[:, None]
    n = np.arange(n1)[None, :]
    w1d, w1f, wfin = [], [], []
    for v in (0, 1):
        cr, ci = _phase_mats(2 * k * n + v * n, n1, conj=False)
        w1d.append(np.block([[cr, -ci], [ci, cr]]))
        sgn = 1.0 if v == 0 else -1.0
        w1f.append(np.block([[cr, sgn * cr], [ci, sgn * ci]]))
        cri, cii = _phase_mats(2 * n.T * k.T + v * n.T, n1, conj=True)
        wfin.append(np.block([[cri, -cii], [cii, cri]]))
    return (np.concatenate(w1d, 0).astype(np.float32), np.concatenate(w1f, 0).astype(np.float32),
            np.concatenate(wfin, 1).astype(np.float32))


def _stage2_mats(n1, n2):
    l = n1 * n2
    k2 = np.arange(n2)[:, None]
    nn = np.arange(n2)[None, :]
    base = (2 * n1 * k2 * nn) % (2 * l)
    br, bi = np.cos(np.pi * base / l), -np.sin(np.pi * base / l)
    k1 = np.arange(n1)[None, :, None]
    v = np.arange(2)[:, None, None]
    tw = (2 * np.arange(n2)[None, None, :] * k1 + v * np.arange(n2)[None, None, :]) % (2 * l)
    tr, ti = np.cos(np.pi * tw / l), -np.sin(np.pi * tw / l)
    br, bi, tr, ti = (jnp.asarray(a, F32) for a in (br, bi, tr, ti))
    cr = br[None, None] * tr[:, :, None, :] - bi[None, None] * ti[:, :, None, :]
    ci = br[None, None] * ti[:, :, None, :] + bi[None, None] * tr[:, :, None, :]
    fwd = jnp.concatenate([jnp.concatenate([cr, -ci], -1), jnp.concatenate([ci, cr], -1)], -2)
    crt, cit = jnp.swapaxes(cr, -1, -2), -jnp.swapaxes(ci, -1, -2)
    inv = jnp.concatenate([jnp.concatenate([crt, -cit], -1), jnp.concatenate([cit, crt], -1)], -2)
    return fwd.astype(BF16), inv.astype(BF16)


def _fft_stage1_kernel(z_ref, w_ref, o_ref):
    n1 = z_ref.shape[1]
    x = z_ref[...].reshape(2 * n1, z_ref.shape[2]).astype(BF16)
    y = jnp.dot(w_ref[...], x, preferred_element_type=F32)
    o_ref[...] = y.reshape(o_ref.shape).astype(o_ref.dtype)


FFT_COLS = 2048


def fft_stage1(z, w1, n1, n2):
    w = z.shape[2]
    cols = n2 * w
    tc = _pick_tile(cols, FFT_COLS, LANES)
    out = pl.pallas_call(
        _fft_stage1_kernel, grid=(cols // tc,),
        in_specs=[pl.BlockSpec((2, n1, tc), lambda j: (0, 0, j)),
                  pl.BlockSpec((4 * n1, 2 * n1), lambda j: (0, 0))],
        out_specs=pl.BlockSpec((2, 2, n1, tc), lambda j: (0, 0, 0, j)),
        out_shape=jax.ShapeDtypeStruct((2, 2, n1, cols), BF16),
        compiler_params=_cparams("parallel"),
        name="fft_stage1",
    )(z.reshape(2, n1, cols), w1)
    return out.reshape(2, 2, n1, n2, w)


def _fft_mid_kernel(a_ref, f_ref, ss_ref, wf_ref, wi_ref, o_ref, *, scale):
    n2 = a_ref.shape[1]
    cw = a_ref.shape[2]
    wf = wf_ref[...]
    t = jnp.dot(wf, a_ref[...].reshape(2 * n2, cw), preferred_element_type=F32)
    g = jnp.dot(wf, f_ref[...].reshape(2 * n2, cw), preferred_element_type=F32)
    g = g * (lax.rsqrt(ss_ref[...] + EPS) * scale)
    tr, ti, gr, gi = t[:n2], t[n2:], g[:n2], g[n2:]
    p = jnp.concatenate([tr * gr - ti * gi, tr * gi + ti * gr], axis=0).astype(BF16)
    u = jnp.dot(wi_ref[...], p, preferred_element_type=F32)
    o_ref[...] = u.reshape(o_ref.shape).astype(o_ref.dtype)


def fft_mid(a, af, ss, wf, wi, order, n1, n2):
    cw = a.shape[-1]
    kern = functools.partial(_fft_mid_kernel, scale=1.0 / (2.0 * n1 * n2))
    mat_spec = pl.BlockSpec((None, None, 2 * n2, 2 * n2), lambda v, k: (v, k, 0, 0))
    return pl.pallas_call(
        kern, grid=(2, n1),
        in_specs=[pl.BlockSpec((None, 2, None, n2, cw), lambda v, k: (v, 0, k, 0, 0)),
                  pl.BlockSpec((None, 2, None, n2, cw), lambda v, k: (v, 0, k, 0, order)),
                  pl.BlockSpec((1, cw), lambda v, k: (0, order)),
                  mat_spec, mat_spec],
        out_specs=pl.BlockSpec((None, 2, None, n2, cw), lambda v, k: (v, 0, k, 0, 0)),
        out_shape=jax.ShapeDtypeStruct(a.shape, BF16),
        compiler_params=_cparams("parallel", "parallel"),
        name="fft_mid",
    )(a, af, ss, wf, wi)


def _fft_final_kernel(u_ref, w_ref, gate_ref, z_ref, bias_ref, o_ref):
    n1 = gate_ref.shape[1]
    cw = gate_ref.shape[2]
    u = u_ref[...].reshape(4 * n1, cw)
    y = jnp.dot(w_ref[...], u, preferred_element_type=F32).reshape(2, n1, cw)
    z = z_ref[...].astype(F32)
    o_ref[...] = (gate_ref[...].astype(F32) * (y + bias_ref[...] * z)).astype(o_ref.dtype)


def fft_final(u, wfin, gate, z, bias, n1, n2):
    cw = z.shape[2]
    cols = n2 * cw
    tc = _pick_tile(cols, FFT_COLS, cw)
    out = pl.pallas_call(
        _fft_final_kernel, grid=(cols // tc,),
        in_specs=[pl.BlockSpec((2, 2, n1, tc), lambda j: (0, 0, 0, j)),
                  pl.BlockSpec((2 * n1, 4 * n1), lambda j: (0, 0)),
                  pl.BlockSpec((2, n1, tc), lambda j: (0, 0, j)),
                  pl.BlockSpec((2, n1, tc), lambda j: (0, 0, j)),
                  pl.BlockSpec((1, tc), lambda j: (0, 0))],
        out_specs=pl.BlockSpec((2, n1, tc), lambda j: (0, 0, j)),
        out_shape=jax.ShapeDtypeStruct((2, n1, cols), BF16),
        compiler_params=_cparams("parallel"),
        name="fft_final",
    )(u.reshape(2, 2, n1, cols), wfin, gate.reshape(2, n1, cols), z.reshape(2, n1, cols),
      jnp.tile(bias.reshape(1, cw), (1, tc // cw)))
    return out.reshape(2, n1 * n2, cw)


def _hy_dense_kernel(z_ref, gate_ref, f_ref, ss_ref, bias_ref, wf_ref, wi_ref, o_ref):
    lc = z_ref.shape[1]
    cw = z_ref.shape[2]
    nrm = lax.rsqrt(ss_ref[...] + EPS) * (1.0 / (2.0 * lc))
    f1, f2 = f_ref[0], f_ref[1]
    filt = (((f1 + f2) * nrm).astype(BF16), ((f1 - f2) * nrm).astype(BF16))
    x = z_ref[...].reshape(2 * lc, cw)
    acc = None
    for v in (0, 1):
        wf = wf_ref[v]
        t = jnp.dot(wf, x, preferred_element_type=F32)
        g = jnp.dot(wf[:, :lc], filt[v], preferred_element_type=F32)
        tr, ti, gr, gi = t[:lc], t[lc:], g[:lc], g[lc:]
        p = jnp.concatenate([tr * gr - ti * gi, tr * gi + ti * gr], axis=0).astype(BF16)
        u = jnp.dot(wi_ref[v], p, preferred_element_type=F32)
        acc = u if acc is None else acc + u
    y = acc.reshape(2, lc, cw)
    o_ref[...] = (gate_ref[...].astype(F32) * (y + bias_ref[...] * z_ref[...].astype(F32))).astype(o_ref.dtype)


def hy_dense_conv(z_arr, gate_arr, f, ss, order, bias, wf, wi):
    lc, cw = z_arr.shape[1], z_arr.shape[2]
    return pl.pallas_call(
        _hy_dense_kernel, grid=(1,),
        in_specs=[pl.BlockSpec((2, lc, cw), lambda i: (0, 0, 0)),
                  pl.BlockSpec((2, lc, cw), lambda i: (0, 0, 0)),
                  pl.BlockSpec((2, lc, cw), lambda i: (0, 0, order)),
                  pl.BlockSpec((1, cw), lambda i: (0, order)),
                  pl.BlockSpec((1, cw), lambda i: (0, 0)),
                  pl.BlockSpec((2, None, 2 * lc, 2 * lc), lambda i: (0, 0, 0, 0)),
                  pl.BlockSpec((2, None, 2 * lc, 2 * lc), lambda i: (0, 0, 0, 0))],
        out_specs=pl.BlockSpec((2, lc, cw), lambda i: (0, 0, 0)),
        out_shape=jax.ShapeDtypeStruct((2, lc, cw), BF16),
        compiler_params=_cparams("arbitrary"),
        name="hy_dense_conv",
    )(z_arr, gate_arr, f, ss, bias.reshape(1, cw), wf, wi)


def hyena_mixer(p, conv_w, conv_b, w1, b1, w2, b2, w3, sin_freq, bias, mats):
    b, l, c3 = p.shape
    assert b == 2, "batch elements are packed as the re/im parts of one complex signal"
    hy_w = c3 // 3
    v, x1, x2 = short_conv(p, conv_w, conv_b, 3)
    f, ss = hy_filters(l, w1, b1, w2, b2, w3, sin_freq, hy_w)
    if l <= 2 * FFT_N2:
        wf, wi = mats["dense"]
        z2 = hy_dense_conv(v, x1, f, ss, 0, bias[0], wf, wi)
        return hy_dense_conv(z2, x2, f, ss, 1, bias[1], wf, wi)
    n2 = FFT_N2
    n1 = l // n2
    w1d, w1f, wfin, wf, wi = mats["fft"]
    af = fft_stage1(f, w1f, n1, n2)
    z2 = fft_final(fft_mid(fft_stage1(v, w1d, n1, n2), af, ss, wf, wi, 0, n1, n2), wfin, x1, v, bias[0], n1, n2)
    return fft_final(fft_mid(fft_stage1(z2, w1d, n1, n2), af, ss, wf, wi, 1, n1, n2), wfin, x2, z2, bias[1], n1, n2)


def hyena_mats(l_lat, l_ctx):
    n1 = l_lat // FFT_N2
    w1d, w1f, wfin = _stage1_mats(n1)
    wf, wi = _stage2_mats(n1, FFT_N2)
    mats = {"fft": (jnp.asarray(w1d, BF16), jnp.asarray(w1f, BF16), jnp.asarray(wfin, BF16), wf, wi)}
    mats["dense"] = _stage2_mats(1, l_ctx)
    return mats


def _ctx_attn_kernel(q_ref, k_ref, v_ref, o_ref):
    s = lax.dot_general(q_ref[...], k_ref[...], NT_DIMS, preferred_element_type=F32)
    m = jnp.max(s, axis=-1, keepdims=True)
    p = jnp.exp(s - m)
    l = jnp.sum(p, axis=-1, keepdims=True)
    o = jnp.dot(p.astype(BF16), v_ref[...], preferred_element_type=F32) / l
    o_ref[...] = o.astype(o_ref.dtype)


def ctx_attention(qa, q_off, ka, k_off, va, v_off, n_q_heads, group):
    b, _, lc, d = qa.shape
    return pl.pallas_call(
        _ctx_attn_kernel, grid=(b, n_q_heads),
        in_specs=[pl.BlockSpec((None, None, lc, d), lambda bi, h: (bi, q_off + h, 0, 0)),
                  pl.BlockSpec((None, None, lc, d), lambda bi, h: (bi, k_off + h // group, 0, 0)),
                  pl.BlockSpec((None, None, lc, d), lambda bi, h: (bi, v_off + h // group, 0, 0))],
        out_specs=pl.BlockSpec((None, lc, d), lambda bi, h: (bi, 0, h)),
        out_shape=jax.ShapeDtypeStruct((b, lc, n_q_heads * d), BF16),
        compiler_params=_cparams("parallel", "parallel"),
        name="ctx_attention",
    )(qa, ka, va)


NA_TILE_ROWS = 4
NA_KEY_ROWS = 12


def _na_geometry(n_rows):
    wr = min(NA_WIN_R, n_rows)
    nt = n_rows // NA_TILE_ROWS
    sigs = []
    for t in range(nt):
        rt = t * NA_TILE_ROWS
        w0 = int(np.clip(rt - wr // 2, 0, n_rows - NA_KEY_ROWS))
        r = rt + np.arange(NA_TILE_ROWS)
        r0 = np.clip(r - wr // 2, 0, n_rows - wr)
        sigs.append((w0 - rt, tuple((r0 - w0).tolist())))
    classes = (sigs[0], sigs[1], sigs[-1])
    for t, s in enumerate(sigs):
        assert s == classes[0 if t == 0 else (2 if t == nt - 1 else 1)], "tile does not match its bias class"
    return wr, nt, classes


def _na_bias_tables(rpb, n_rows):
    wr, _, classes = _na_geometry(n_rows)
    cols = np.arange(GRID_W)
    c0 = np.clip(cols - NA_WIN_C // 2, 0, GRID_W - NA_WIN_C)
    col_ok = (cols[None, :] >= c0[:, None]) & (cols[None, :] < c0[:, None] + NA_WIN_C)
    col_idx = np.clip(cols[None, :] - cols[:, None] + NA_WIN_C - 1, 0, 2 * NA_WIN_C - 2)
    col_sel = (col_idx[None] == np.arange(2 * NA_WIN_C - 1)[:, None, None]).astype(np.float32)
    row_sel, oks = [], []
    for off, rel_r0 in classes:
        qi = np.arange(NA_TILE_ROWS)[:, None]
        kw = np.arange(NA_KEY_ROWS)[None, :]
        r0 = np.asarray(rel_r0)[:, None]
        row_ok = (kw >= r0) & (kw < r0 + wr)
        row_idx = np.clip(kw + off - qi + NA_WIN_R - 1, 0, 2 * NA_WIN_R - 2)
        row_sel.append((row_idx[None] == np.arange(2 * NA_WIN_R - 1)[:, None, None]).astype(np.float32))
        oks.append(row_ok[:, None, :, None] & col_ok[None, :, None, :])
    vals = jnp.einsum("hab,saqk,bcd->hsqckd", rpb.astype(F32), jnp.asarray(np.stack(row_sel)),
                      jnp.asarray(col_sel), precision=lax.Precision.HIGHEST)
    tab = jnp.where(jnp.asarray(np.stack(oks))[None], vals, NEG_BIG)
    return tab.reshape(rpb.shape[0], 3, NA_TILE_ROWS * GRID_W, NA_KEY_ROWS * GRID_W)


def _na_kernel(q_ref, k_ref, v_ref, kc_ref, vc_ref, bias_ref, o_ref, *, n_rows, win_half):
    t = pl.program_id(2)
    w0 = jnp.clip(t * NA_TILE_ROWS - win_half, 0, n_rows - NA_KEY_ROWS)
    start = pl.multiple_of(w0 * GRID_W, GRID_W)
    kwin = k_ref[pl.ds(start, NA_KEY_ROWS * GRID_W), :]
    vwin = v_ref[pl.ds(start, NA_KEY_ROWS * GRID_W), :]
    q = q_ref[...]
    s_lat = lax.dot_general(q, kwin, NT_DIMS, preferred_element_type=F32) + bias_ref[...]
    s_ctx = lax.dot_general(q, kc_ref[...], NT_DIMS, preferred_element_type=F32)
    m = jnp.maximum(jnp.max(s_lat, axis=-1, keepdims=True), jnp.max(s_ctx, axis=-1, keepdims=True))
    p_lat = jnp.exp(s_lat - m)
    p_ctx = jnp.exp(s_ctx - m)
    l = jnp.sum(p_lat, axis=-1, keepdims=True) + jnp.sum(p_ctx, axis=-1, keepdims=True)
    o = (jnp.dot(p_lat.astype(BF16), vwin, preferred_element_type=F32)
         + jnp.dot(p_ctx.astype(BF16), vc_ref[...], preferred_element_type=F32)) / l
    o_ref[...] = o.astype(o_ref.dtype)


def na_attention(pl_heads, pc_heads, rpb):
    b, h3, l, d = pl_heads.shape
    lc = pc_heads.shape[2]
    h = h3 // 3
    n_rows = l // GRID_W
    wr, nt, _ = _na_geometry(n_rows)
    bias = _na_bias_tables(rpb, n_rows)
    tq = NA_TILE_ROWS * GRID_W
    kw = NA_KEY_ROWS * GRID_W
    kern = functools.partial(_na_kernel, n_rows=n_rows, win_half=wr // 2)

    def cls(t):
        return jnp.where(t == 0, 0, jnp.where(t == nt - 1, 2, 1))

    return pl.pallas_call(
        kern, grid=(b, h, nt),
        in_specs=[pl.BlockSpec((None, None, tq, d), lambda bi, hi, t: (bi, hi, t, 0)),
                  pl.BlockSpec((None, None, l, d), lambda bi, hi, t: (bi, h + hi, 0, 0)),
                  pl.BlockSpec((None, None, l, d), lambda bi, hi, t: (bi, 2 * h + hi, 0, 0)),
                  pl.BlockSpec((None, None, lc, d), lambda bi, hi, t: (bi, h + hi, 0, 0)),
                  pl.BlockSpec((None, None, lc, d), lambda bi, hi, t: (bi, 2 * h + hi, 0, 0)),
                  pl.BlockSpec((None, None, tq, kw), lambda bi, hi, t: (hi, cls(t), 0, 0))],
        out_specs=pl.BlockSpec((None, tq, d), lambda bi, hi, t: (bi, t, hi)),
        out_shape=jax.ShapeDtypeStruct((b, l, h * d), BF16),
        compiler_params=_cparams("parallel", "parallel", "parallel"),
        name="na_attention",
    )(pl_heads, pl_heads, pl_heads, pc_heads, pc_heads, bias)


def _ret_kernel(*refs, reverse, rope, finalize, n_heads):
    it = iter(refs)
    lg_ref, p_ref = next(it), next(it)
    cos_ref, sin_ref = (next(it), next(it)) if rope else (None, None)
    s0_ref = next(it)
    oprev_ref, gnw_ref = (next(it), next(it)) if finalize else (None, None)
    o_ref, sfin_ref, state_ref = next(it), next(it), next(it)

    i = pl.program_id(1)
    n_steps = pl.num_programs(1)

    @pl.when(i == 0)
    def _():
        state_ref[...] = s0_ref[...]

    c = RET_CHUNK
    ts = p_ref.shape[0]
    nc = ts // c
    dq = n_heads * RET_DK
    dv = n_heads * RET_DV
    x = p_ref[...]
    q = x[:, :dq].astype(F32)
    k = x[:, dq:2 * dq].astype(F32)
    v = x[:, 2 * dq:2 * dq + dv]
    if rope:
        q = q * cos_ref[...] + _swap32(q) * sin_ref[...]
        k = k * cos_ref[...] + _swap32(k) * sin_ref[...]
    jr = lax.broadcasted_iota(jnp.int32, (c, c), 0)
    jc = lax.broadcasted_iota(jnp.int32, (c, c), 1)
    rel = ((jc - jr) if reverse else (jr - jc)).astype(F32)
    jcol = lax.broadcasted_iota(jnp.int32, (c, 1), 0).astype(F32)
    order = range(nc - 1, -1, -1) if reverse else range(nc)
    blocks = [[None] * n_heads for _ in range(nc)]
    for h in range(n_heads):
        g = lg_ref[h]
        intra = jnp.where(rel >= 0.0, jnp.exp(g * jnp.maximum(rel, 0.0)), 0.0)
        if reverse:
            cross_f = jnp.exp(g * (float(c) - jcol))
            k_dec = jnp.exp(g * jcol)
        else:
            cross_f = jnp.exp(g * (jcol + 1.0))
            k_dec = jnp.exp(g * (float(c - 1) - jcol))
        chunk_decay = jnp.exp(g * float(c))
        state = state_ref[h]
        for ci in order:
            rows = slice(ci * c, (ci + 1) * c)
            qh = q[rows, h * RET_DK:(h + 1) * RET_DK].astype(BF16)
            kf = k[rows, h * RET_DK:(h + 1) * RET_DK]
            vh = v[rows, h * RET_DV:(h + 1) * RET_DV]
            scores = lax.dot_general(qh, kf.astype(BF16), NT_DIMS, preferred_element_type=F32) * intra
            inner = jnp.dot(scores.astype(BF16), vh, preferred_element_type=F32)
            cross = jnp.dot(qh, state.astype(BF16), preferred_element_type=F32) * cross_f
            blocks[ci][h] = inner + cross
            kv = lax.dot_general((kf * k_dec).astype(BF16), vh, TN_DIMS, preferred_element_type=F32)
            state = chunk_decay * state + kv
        state_ref[h] = state
    o = jnp.concatenate([jnp.concatenate(blocks[ci], axis=1) for ci in range(nc)], axis=0)
    if finalize:
        o = o + oprev_ref[...]
        gate = x[:, 2 * dq + dv:2 * dq + 2 * dv].astype(F32)
        normed = []
        for h in range(n_heads):
            oh = o[:, h * RET_DV:(h + 1) * RET_DV]
            normed.append(oh * lax.rsqrt(jnp.mean(oh * oh, axis=-1, keepdims=True) + EPS))
        o = jnp.concatenate(normed, axis=1) * gnw_ref[...] * _silu(gate)
    o_ref[...] = o.astype(o_ref.dtype)

    @pl.when(i == n_steps - 1)
    def _():
        sfin_ref[...] = state_ref[...]


def retention_pass(p, log_g, s0, reverse, rope_tables=None, o_prev=None, gn_w=None):
    b, l, w = p.shape
    n_heads = w // (2 * RET_DK + 2 * RET_DV)
    dq, dv = n_heads * RET_DK, n_heads * RET_DV
    ts = _pick_tile(l, 512, RET_CHUNK)
    n_steps = l // ts
    finalize = o_prev is not None
    rope = rope_tables is not None

    def tile(i):
        return (n_steps - 1 - i) if reverse else i

    in_specs = [pl.BlockSpec(memory_space=pltpu.SMEM),
                pl.BlockSpec((None, ts, w), lambda bi, i: (bi, tile(i), 0))]
    args = [log_g, p]
    if rope:
        in_specs += [pl.BlockSpec((ts, dq), lambda bi, i: (tile(i), 0))] * 2
        args += list(rope_tables)
    in_specs.append(pl.BlockSpec((None, n_heads, RET_DK, RET_DV), lambda bi, i: (bi, 0, 0, 0)))
    args.append(s0)
    if finalize:
        in_specs += [pl.BlockSpec((None, ts, dv), lambda bi, i: (bi, tile(i), 0)),
                     pl.BlockSpec((1, dv), lambda bi, i: (0, 0))]
        args += [o_prev, gn_w.reshape(1, dv)]
    kern = functools.partial(_ret_kernel, reverse=reverse, rope=rope, finalize=finalize, n_heads=n_heads)
    return pl.pallas_call(
        kern, grid=(b, n_steps),
        in_specs=in_specs,
        out_specs=[pl.BlockSpec((None, ts, dv), lambda bi, i: (bi, tile(i), 0)),
                   pl.BlockSpec((None, n_heads, RET_DK, RET_DV), lambda bi, i: (bi, 0, 0, 0))],
        out_shape=[jax.ShapeDtypeStruct((b, l, dv), BF16 if finalize else F32),
                   jax.ShapeDtypeStruct((b, n_heads, RET_DK, RET_DV), F32)],
        scratch_shapes=[pltpu.VMEM((n_heads, RET_DK, RET_DV), F32)],
        compiler_params=_cparams("parallel", "arbitrary"),
        name="retention_pass",
    )(*args)


def _ret_rope_tables(l, n_heads):
    half = RET_DK // 2
    inv = RET_ROPE_BASE ** (-jnp.linspace(0.0, 1.0, half, dtype=F32))
    ang = jnp.arange(l, dtype=F32)[:, None] * inv
    cos, sin = jnp.cos(ang), jnp.sin(ang)
    return (jnp.tile(jnp.concatenate([cos, cos], -1), (1, n_heads)),
            jnp.tile(jnp.concatenate([-sin, sin], -1), (1, n_heads)))


def retention_mixer(p_l, p_c, log_decay, gn_w, tables, with_ctx_out):
    b = p_l.shape[0]
    n_heads = p_l.shape[2] // (2 * RET_DK + 2 * RET_DV)
    log_g = -jnp.abs(log_decay.astype(F32))
    s0 = jnp.zeros((b, n_heads, RET_DK, RET_DV), F32)
    o_cf, s_fwd = retention_pass(p_c, log_g[0], s0, False)
    y_c, s_bwd = retention_pass(p_c, log_g[1], s0, True, o_prev=o_cf, gn_w=gn_w)
    o_lf, _ = retention_pass(p_l, log_g[0], s_fwd, False, rope_tables=tables)
    y_l, _ = retention_pass(p_l, log_g[1], s_bwd, True, rope_tables=tables, o_prev=o_lf, gn_w=gn_w)
    return y_l, (y_c if with_ctx_out else None)


def _gqa_prep_kernel(*refs, rope, n_q, n_kv, q_scale):
    if rope:
        p_ref, cos_ref, sin_ref, qw_ref, kw_ref, q_out, k_out, v_out = refs
    else:
        p_ref, qw_ref, kw_ref, q_out, k_out, v_out = refs
    x = p_ref[...]
    d = HEAD_DIM

    def norm_rope(xh, w):
        xh = xh.astype(F32)
        y = xh * lax.rsqrt(jnp.mean(xh * xh, axis=-1, keepdims=True) + EPS) * w
        if rope:
            y = y * cos_ref[...] + _swap32(y) * sin_ref[...]
        return y

    for h in range(n_q):
        q_out[h] = (norm_rope(x[:, h * d:(h + 1) * d], qw_ref[...]) * q_scale).astype(q_out.dtype)
    for h in range(n_kv):
        k_out[h] = norm_rope(x[:, (n_q + h) * d:(n_q + h + 1) * d], kw_ref[...]).astype(k_out.dtype)
        v_out[h] = x[:, (n_q + n_kv + h) * d:(n_q + n_kv + h + 1) * d]


def gqa_prep(p, qn_w, kn_w, n_q, n_kv, q_scale, rope_tables=None):
    b, l, w = p.shape
    d = HEAD_DIM
    t = _pick_tile(l, 512, 16)
    rope = rope_tables is not None
    in_specs = [pl.BlockSpec((None, t, w), lambda bi, i: (bi, i, 0))]
    args = [p]
    if rope:
        in_specs += [pl.BlockSpec((t, d), lambda bi, i: (i, 0))] * 2
        args += list(rope_tables)
    in_specs += [pl.BlockSpec((1, d), lambda bi, i: (0, 0))] * 2
    args += [qn_w.reshape(1, d), kn_w.reshape(1, d)]
    kern = functools.partial(_gqa_prep_kernel, rope=rope, n_q=n_q, n_kv=n_kv, q_scale=q_scale)
    return pl.pallas_call(
        kern, grid=(b, l // t),
        in_specs=in_specs,
        out_specs=[pl.BlockSpec((None, n_q, t, d), lambda bi, i: (bi, 0, i, 0)),
                   pl.BlockSpec((None, n_kv, t, d), lambda bi, i: (bi, 0, i, 0)),
                   pl.BlockSpec((None, n_kv, t, d), lambda bi, i: (bi, 0, i, 0))],
        out_shape=[jax.ShapeDtypeStruct((b, n_q, l, d), BF16),
                   jax.ShapeDtypeStruct((b, n_kv, l, d), BF16),
                   jax.ShapeDtypeStruct((b, n_kv, l, d), BF16)],
        compiler_params=_cparams("parallel", "parallel"),
        name="gqa_prep",
    )(*args)


def _axial_rope_tables(l):
    nf = HEAD_DIM // 4
    t = jnp.arange(l)
    inv = ROPE_BASE ** (-jnp.arange(nf, dtype=F32) / nf)
    ang_r = (t // GRID_W).astype(F32)[:, None] * inv
    ang_c = (t % GRID_W).astype(F32)[:, None] * inv
    cr, sr, cc, sc = jnp.cos(ang_r), jnp.sin(ang_r), jnp.cos(ang_c), jnp.sin(ang_c)
    return (jnp.concatenate([cr, cr, cc, cc], -1), jnp.concatenate([-sr, sr, -sc, sc], -1))


FLASH_SUB_ROWS = 256


def _flash_kernel(q_ref, k_ref, v_ref, o_ref, m_ref, l_ref, acc_ref):
    j = pl.program_id(3)
    g, tq, d = q_ref.shape

    @pl.when(j == 0)
    def _():
        m_ref[...] = jnp.full(m_ref.shape, NEG_BIG, F32)
        l_ref[...] = jnp.zeros(l_ref.shape, F32)
        acc_ref[...] = jnp.zeros(acc_ref.shape, F32)

    q = q_ref[...].reshape(g * tq, d)
    k = k_ref[...]
    v = v_ref[...]
    n_sub = (g * tq) // FLASH_SUB_ROWS
    s, p, alpha = [None] * n_sub, [None] * n_sub, [None] * n_sub
    for t in range(n_sub + 2):
        if t < n_sub:
            s[t] = lax.dot_general(q[t * FLASH_SUB_ROWS:(t + 1) * FLASH_SUB_ROWS], k, NT_DIMS,
                                   preferred_element_type=F32)
        u = t - 1
        if 0 <= u < n_sub:
            rows = slice(u * FLASH_SUB_ROWS, (u + 1) * FLASH_SUB_ROWS)
            m_prev = m_ref[rows]
            m_new = jnp.maximum(m_prev, jnp.max(s[u], axis=-1, keepdims=True))
            alpha[u] = jnp.exp2(m_prev - m_new)
            pu = jnp.exp2(s[u] - m_new)
            l_ref[rows] = alpha[u] * l_ref[rows] + jnp.sum(pu, axis=-1, keepdims=True)
            m_ref[rows] = m_new
            p[u] = pu.astype(BF16)
            s[u] = None
        w = t - 2
        if 0 <= w < n_sub:
            rows = slice(w * FLASH_SUB_ROWS, (w + 1) * FLASH_SUB_ROWS)
            acc_ref[rows] = alpha[w] * acc_ref[rows] + jnp.dot(p[w], v, preferred_element_type=F32)
            p[w] = None

    @pl.when(j == pl.num_programs(3) - 1)
    def _():
        o = acc_ref[...] / l_ref[...]
        o_ref[...] = jnp.concatenate([o[h * tq:(h + 1) * tq] for h in range(g)], axis=1).astype(o_ref.dtype)


def flash_gqa(q, k, v):
    b, hq, l, d = q.shape
    hkv, lk = k.shape[1], k.shape[2]
    g = hq // hkv
    tq = _pick_tile(l, 512, 16)
    tk = _pick_tile(lk, 1280, LANES)
    assert (g * tq) % FLASH_SUB_ROWS == 0
    return pl.pallas_call(
        _flash_kernel, grid=(b, hkv, l // tq, lk // tk),
        in_specs=[pl.BlockSpec((None, g, tq, d), lambda bi, h, i, j: (bi, h, i, 0)),
                  pl.BlockSpec((None, None, tk, d), lambda bi, h, i, j: (bi, h, j, 0)),
                  pl.BlockSpec((None, None, tk, d), lambda bi, h, i, j: (bi, h, j, 0))],
        out_specs=pl.BlockSpec((None, tq, g * d), lambda bi, h, i, j: (bi, i, h)),
        out_shape=jax.ShapeDtypeStruct((b, l, hq * d), BF16),
        scratch_shapes=[pltpu.VMEM((g * tq, 1), F32), pltpu.VMEM((g * tq, 1), F32),
                        pltpu.VMEM((g * tq, d), F32)],
        compiler_params=_cparams("parallel", "parallel", "parallel", "arbitrary"),
        name="flash_gqa",
    )(q, k, v)


def gqa_mixer(p_l, p_c, qn_w, kn_w, tables, with_ctx_out):
    n_q = p_l.shape[2] // (2 * HEAD_DIM)
    n_kv = n_q // 2
    scale = HEAD_DIM ** -0.5
    q_l, k_l, v_l = gqa_prep(p_l, qn_w, kn_w, n_q, n_kv, scale * math.log2(math.e), rope_tables=tables)
    q_c, k_c, v_c = gqa_prep(p_c, qn_w, kn_w, n_q, n_kv, scale)
    y_l = flash_gqa(q_l, jnp.concatenate([k_l, k_c], axis=2), jnp.concatenate([v_l, v_c], axis=2))
    y_c = ctx_attention(q_c, 0, k_c, 0, v_c, 0, n_q, n_q // n_kv) if with_ctx_out else None
    return y_l, y_c


def _split_w_in(w_in, d_model):
    mix_w = d_model // N_BRANCH
    h = mix_w // HEAD_DIM
    sizes = (3 * mix_w, 3 * h * HEAD_DIM, 2 * h * RET_DK + 2 * h * RET_DV,
             (h + 2 * max(h // 2, 1)) * HEAD_DIM, N_BRANCH * d_model)
    offs = np.cumsum((0,) + sizes)
    w_hy, w_na, w_rt, w_gq, w_gate = (w_in[:, offs[i]:offs[i + 1]] for i in range(5))
    na_scale = jnp.concatenate([jnp.full((h * HEAD_DIM,), HEAD_DIM ** -0.5, F32),
                                jnp.ones((2 * h * HEAD_DIM,), F32)])
    rt_scale = jnp.concatenate([jnp.ones((h * RET_DK,), F32), jnp.full((h * RET_DK,), RET_DK ** -0.5, F32),
                                jnp.ones((2 * h * RET_DV,), F32)])
    return ((w_hy).astype(BF16), (w_na * na_scale).astype(BF16), (w_rt * rt_scale).astype(BF16),
            w_gq.astype(BF16), w_gate.astype(BF16))


def kernel(x, c, ctx, c_ctx, ada_w, ada_b, norm1_w, norm2_w, w_in, hy_conv_w, hy_conv_b, hy_ffn_w1, hy_ffn_b1,
           hy_ffn_w2, hy_ffn_b2, hy_ffn_w3, hy_sin_freq, hy_bias, na_rpb, ret_log_decay, ret_gn_w, gqa_q_norm_w,
           gqa_k_norm_w, w_branch, w_out, ffn_w13, ffn_w2, final_norm_w):
    b, l, d = x.shape
    lc = ctx.shape[1]
    depth = ada_w.shape[0]
    ffn_hidden = ffn_w2.shape[1]
    n_ret_heads = (d // N_BRANCH) // RET_DV

    cs = jnp.zeros((8, d), F32).at[:b].set(c).at[b].set(c_ctx)
    mods = ada_mod(cs, ada_w, ada_b)
    ctx_row = b
    mats = hyena_mats(l, lc)
    ret_tables = _ret_rope_tables(l, n_ret_heads)
    gqa_tables = _axial_rope_tables(l)

    x_l = x.reshape(b * l, d)
    x_c = ctx.reshape(b * lc, d)
    for layer in range(depth):
        last = layer == depth - 1
        mod = mods[layer]
        w_hy, w_na, w_rt, w_gq, w_gate = _split_w_in(w_in[layer], d)
        hy_params = (hy_conv_w[layer], hy_conv_b[layer], hy_ffn_w1[layer], hy_ffn_b1[layer], hy_ffn_w2[layer],
                     hy_ffn_b2[layer], hy_ffn_w3[layer], hy_sin_freq[layer], hy_bias[layer])
        wb = w_branch[layer].astype(BF16)
        wo = w_out[layer].astype(BF16)
        w1 = ffn_w13[layer][:, :ffn_hidden].astype(BF16)
        w3 = ffn_w13[layer][:, ffn_hidden:].astype(BF16)
        w2 = ffn_w2[layer].astype(BF16)

        h_l = norm_mod(x_l, norm1_w[layer], mod, 0, 1, l, 0)
        h_c = norm_mod(x_c, norm1_w[layer], mod, 0, 1, None, ctx_row)

        na_l = matmul_heads(h_l, w_na, b)
        na_c = matmul_heads(h_c, w_na, b)
        rt_l = matmul(h_l, w_rt).reshape(b, l, -1)
        rt_c = matmul(h_c, w_rt).reshape(b, lc, -1)
        gq_l = matmul(h_l, w_gq).reshape(b, l, -1)
        gq_c = matmul(h_c, w_gq).reshape(b, lc, -1)
        hy_l = matmul(h_l, w_hy).reshape(b, l, -1)
        gate_l = matmul(h_l, w_gate)

        y_hy_l = hyena_mixer(hy_l, *hy_params, mats)
        y_na_l = na_attention(na_l, na_c, na_rpb[layer])
        y_rt_l, y_rt_c = retention_mixer(rt_l, rt_c, ret_log_decay[layer], ret_gn_w[layer], ret_tables, not last)
        y_gq_l, y_gq_c = gqa_mixer(gq_l, gq_c, gqa_q_norm_w[layer], gqa_k_norm_w[layer], gqa_tables, not last)

        m_l = merge_branches([y.reshape(b * l, -1) for y in (y_hy_l, y_na_l, y_rt_l, y_gq_l)], gate_l, wb)
        x_l = matmul_residual(m_l, wo, x_l, mod, 2, l, 0)
        h2 = norm_mod(x_l, norm2_w[layer], mod, 3, 4, l, 0)
        x_l = matmul_residual(matmul_swiglu(h2, w1, w3), w2, x_l, mod, 5, l, 0)

        if not last:
            n_na = na_c.shape[1] // 3
            hy_c = matmul(h_c, w_hy).reshape(b, lc, -1)
            gate_c = matmul(h_c, w_gate)
            y_hy_c = hyena_mixer(hy_c, *hy_params, mats)
            y_na_c = ctx_attention(na_c, 0, na_c, n_na, na_c, 2 * n_na, n_na, 1)
            m_c = merge_branches([y.reshape(b * lc, -1) for y in (y_hy_c, y_na_c, y_rt_c, y_gq_c)], gate_c, wb)
            x_c = matmul_residual(m_c, wo, x_c, mod, 2, None, ctx_row)
            h2c = norm_mod(x_c, norm2_w[layer], mod, 3, 4, None, ctx_row)
            x_c = matmul_residual(matmul_swiglu(h2c, w1, w3), w2, x_c, mod, 5, None, ctx_row)

    return final_norm(x_l, final_norm_w).reshape(b, l, d)
```

```python
import functools
import math

import numpy as np
import jax
import jax.numpy as jnp
from jax import lax
from jax.experimental import pallas as pl
from jax.experimental.pallas import tpu as pltpu

F32 = jnp.float32
BF16 = jnp.bfloat16

EPS = 1e-6
GRID_W = 64
HEAD_DIM = 128
N_BRANCH = 4
ROPE_BASE = 10000.0
HY_BANDS = 16
HY_DECAY_TARGET = 1e-2
HY_FAST_PCT = 0.3
HY_SLOW_PCT = 1.5
NA_WIN_R = 8
NA_WIN_C = 16
RET_DK = 64
RET_DV = 128
RET_CHUNK = 128
RET_ROPE_BASE = 10000.0
NEG_BIG = -1e30

LANES = 128
FFT_N2 = 128
VMEM_LIMIT = 56 * 1024 * 1024

NT_DIMS = (((1,), (1,)), ((), ()))
TN_DIMS = (((0,), (0,)), ((), ()))


def _cparams(*sem):
    return pltpu.CompilerParams(dimension_semantics=sem, vmem_limit_bytes=VMEM_LIMIT)


def _pick_tile(n, cap, mult):
    best = None
    for t in range(mult, min(n, cap) + 1, mult):
        if n % t == 0:
            best = t
    assert best is not None, (n, cap, mult)
    return best


def _swap32(x):
    n = x.shape[-1]
    lane = lax.broadcasted_iota(jnp.int32, x.shape, x.ndim - 1)
    up = pltpu.roll(x, n - 32, x.ndim - 1)
    down = pltpu.roll(x, 32, x.ndim - 1)
    return jnp.where((lane % 64) < 32, up, down)


def _silu(x):
    return x * jax.nn.sigmoid(x)


def _ada_kernel(c_ref, w_ref, b_ref, o_ref):
    a = _silu(c_ref[...]).astype(BF16)
    o_ref[...] = jnp.dot(a, w_ref[...].astype(BF16), preferred_element_type=F32) + b_ref[...]


def ada_mod(cs, ada_w, ada_b):
    depth, d, n = ada_w.shape
    tn = _pick_tile(n, 1536, LANES)
    return pl.pallas_call(
        _ada_kernel,
        grid=(depth, n // tn),
        in_specs=[pl.BlockSpec((8, d), lambda l, j: (0, 0)),
                  pl.BlockSpec((None, d, tn), lambda l, j: (l, 0, j)),
                  pl.BlockSpec((None, 1, tn), lambda l, j: (l, 0, j))],
        out_specs=pl.BlockSpec((None, 8, tn), lambda l, j: (l, 0, j)),
        out_shape=jax.ShapeDtypeStruct((depth, 8, n), F32),
        compiler_params=_cparams("parallel", "parallel"),
        name="ada_mod",
    )(cs, ada_w, ada_b.reshape(depth, 1, n))


def _mod_row(row_base, tiles_per_batch, axis):
    if tiles_per_batch is None:
        return row_base
    return row_base + pl.program_id(axis) // tiles_per_batch


def _norm_mod_kernel(x_ref, w_ref, sh_ref, sc_ref, o_ref, *, row_base, tiles_per_batch):
    x = x_ref[...]
    y = x * lax.rsqrt(jnp.mean(x * x, axis=-1, keepdims=True) + EPS) * w_ref[...]
    row = _mod_row(row_base, tiles_per_batch, 0)
    sh = sh_ref[pl.ds(row, 1), :]
    sc = sc_ref[pl.ds(row, 1), :]
    o_ref[...] = (y * (1.0 + sc) + sh).astype(o_ref.dtype)


def _norm_kernel(x_ref, w_ref, o_ref):
    x = x_ref[...]
    y = x * lax.rsqrt(jnp.mean(x * x, axis=-1, keepdims=True) + EPS) * w_ref[...]
    o_ref[...] = y.astype(o_ref.dtype)


def norm_mod(x, w, mod, sh_chunk, sc_chunk, rows_per_batch, row_base):
    m, d = x.shape
    tm = _pick_tile(m if rows_per_batch is None else rows_per_batch, 512, 8)
    tpb = None if rows_per_batch is None else rows_per_batch // tm
    kern = functools.partial(_norm_mod_kernel, row_base=row_base, tiles_per_batch=tpb)
    return pl.pallas_call(
        kern, grid=(m // tm,),
        in_specs=[pl.BlockSpec((tm, d), lambda i: (i, 0)),
                  pl.BlockSpec((1, d), lambda i: (0, 0)),
                  pl.BlockSpec((8, d), lambda i: (0, sh_chunk)),
                  pl.BlockSpec((8, d), lambda i: (0, sc_chunk))],
        out_specs=pl.BlockSpec((tm, d), lambda i: (i, 0)),
        out_shape=jax.ShapeDtypeStruct((m, d), BF16),
        compiler_params=_cparams("parallel"),
        name="norm_mod",
    )(x, w.reshape(1, d), mod, mod)


def final_norm(x, w):
    m, d = x.shape
    tm = _pick_tile(m, 512, 8)
    return pl.pallas_call(
        _norm_kernel, grid=(m // tm,),
        in_specs=[pl.BlockSpec((tm, d), lambda i: (i, 0)), pl.BlockSpec((1, d), lambda i: (0, 0))],
        out_specs=pl.BlockSpec((tm, d), lambda i: (i, 0)),
        out_shape=jax.ShapeDtypeStruct((m, d), F32),
        compiler_params=_cparams("parallel"),
        name="final_norm",
    )(x, w.reshape(1, d))


def _mm_kernel(a_ref, w_ref, o_ref):
    o_ref[...] = jnp.dot(a_ref[...], w_ref[...], preferred_element_type=F32).astype(o_ref.dtype)


def _mm_heads_kernel(a_ref, w_ref, o_ref):
    acc = jnp.dot(a_ref[...], w_ref[...], preferred_element_type=F32)
    for h in range(o_ref.shape[0]):
        o_ref[h] = acc[:, h * HEAD_DIM:(h + 1) * HEAD_DIM].astype(o_ref.dtype)


def _mm_swiglu_kernel(a_ref, w1_ref, w3_ref, o_ref):
    a = a_ref[...]
    u = jnp.dot(a, w1_ref[...], preferred_element_type=F32)
    g = jnp.dot(a, w3_ref[...], preferred_element_type=F32)
    o_ref[...] = (_silu(u) * g).astype(o_ref.dtype)


def _mm_residual_kernel(a_ref, w_ref, x_ref, g_ref, o_ref, *, row_base, tiles_per_batch):
    acc = jnp.dot(a_ref[...], w_ref[...], preferred_element_type=F32)
    row = _mod_row(row_base, tiles_per_batch, 1)
    o_ref[...] = x_ref[...] + g_ref[pl.ds(row, 1), :] * acc


def matmul(a, w, out_dtype=BF16):
    m, k = a.shape
    n = w.shape[1]
    tm = _pick_tile(m, 512, 16)
    tn = _pick_tile(n, 2048, LANES)
    return pl.pallas_call(
        _mm_kernel, grid=(n // tn, m // tm),
        in_specs=[pl.BlockSpec((tm, k), lambda j, i: (i, 0)), pl.BlockSpec((k, tn), lambda j, i: (0, j))],
        out_specs=pl.BlockSpec((tm, tn), lambda j, i: (i, j)),
        out_shape=jax.ShapeDtypeStruct((m, n), out_dtype),
        compiler_params=_cparams("parallel", "parallel"),
        name="matmul",
    )(a, w)


def matmul_heads(a, w, batch):
    m, k = a.shape
    n = w.shape[1]
    lb = m // batch
    tm = _pick_tile(lb, 512, 16)
    tn = _pick_tile(n, 1536, LANES)
    tpb = lb // tm
    nh = tn // HEAD_DIM
    return pl.pallas_call(
        _mm_heads_kernel, grid=(n // tn, m // tm),
        in_specs=[pl.BlockSpec((tm, k), lambda j, i: (i, 0)), pl.BlockSpec((k, tn), lambda j, i: (0, j))],
        out_specs=pl.BlockSpec((None, nh, tm, HEAD_DIM), lambda j, i: (i // tpb, j, i % tpb, 0)),
        out_shape=jax.ShapeDtypeStruct((batch, n // HEAD_DIM, lb, HEAD_DIM), BF16),
        compiler_params=_cparams("parallel", "parallel"),
        name="matmul_heads",
    )(a, w)


def matmul_swiglu(a, w1, w3):
    m, k = a.shape
    n = w1.shape[1]
    tm = _pick_tile(m, 1024, 16)
    tn = _pick_tile(n, 704, LANES)
    return pl.pallas_call(
        _mm_swiglu_kernel, grid=(n // tn, m // tm),
        in_specs=[pl.BlockSpec((tm, k), lambda j, i: (i, 0)),
                  pl.BlockSpec((k, tn), lambda j, i: (0, j)),
                  pl.BlockSpec((k, tn), lambda j, i: (0, j))],
        out_specs=pl.BlockSpec((tm, tn), lambda j, i: (i, j)),
        out_shape=jax.ShapeDtypeStruct((m, n), BF16),
        compiler_params=_cparams("parallel", "parallel"),
        name="matmul_swiglu",
    )(a, w1, w3)


def matmul_residual(a, w, x, mod, gate_chunk, rows_per_batch, row_base):
    m, k = a.shape
    n = w.shape[1]
    small_k = k <= 2048
    tm = _pick_tile(m if rows_per_batch is None else rows_per_batch, 1024 if small_k else 512, 16)
    tn = _pick_tile(n, 1024 if small_k else 512, LANES)
    tpb = None if rows_per_batch is None else rows_per_batch // tm
    kern = functools.partial(_mm_residual_kernel, row_base=row_base, tiles_per_batch=tpb)
    gblk = gate_chunk * (n // tn)
    return pl.pallas_call(
        kern, grid=(n // tn, m // tm),
        in_specs=[pl.BlockSpec((tm, k), lambda j, i: (i, 0)),
                  pl.BlockSpec((k, tn), lambda j, i: (0, j)),
                  pl.BlockSpec((tm, tn), lambda j, i: (i, j)),
                  pl.BlockSpec((8, tn), lambda j, i: (0, gblk + j))],
        out_specs=pl.BlockSpec((tm, tn), lambda j, i: (i, j)),
        out_shape=jax.ShapeDtypeStruct((m, n), F32),
        compiler_params=_cparams("parallel", "parallel"),
        name="matmul_residual",
    )(a, w, x, mod)


def _merge_kernel(y0, y1, y2, y3, g0, g1, g2, g3, wb_ref, o_ref):
    acc = None
    for i, (y, g) in enumerate(((y0, g0), (y1, g1), (y2, g2), (y3, g3))):
        gate = 0.5 * jnp.tanh(0.5 * g[...].astype(F32)) + 0.5
        t = gate * jnp.dot(y[...], wb_ref[i], preferred_element_type=F32)
        acc = t if acc is None else acc + t
    o_ref[...] = acc.astype(o_ref.dtype)


def merge_branches(ys, gate_pre, wb):
    m, w = ys[0].shape
    d = wb.shape[2]
    tm = _pick_tile(m, 512, 16)
    tn = _pick_tile(d, 1024, LANES)
    nb = d // tn
    y_spec = pl.BlockSpec((tm, w), lambda j, i: (i, 0))
    g_specs = [pl.BlockSpec((tm, tn), functools.partial(lambda j, i, b: (i, b * nb + j), b=b))
               for b in range(N_BRANCH)]
    return pl.pallas_call(
        _merge_kernel, grid=(nb, m // tm),
        in_specs=[y_spec] * N_BRANCH + g_specs + [pl.BlockSpec((N_BRANCH, w, tn), lambda j, i: (0, 0, j))],
        out_specs=pl.BlockSpec((tm, tn), lambda j, i: (i, j)),
        out_shape=jax.ShapeDtypeStruct((m, d), BF16),
        compiler_params=_cparams("parallel", "parallel"),
        name="merge_branches",
    )(*ys, gate_pre, gate_pre, gate_pre, gate_pre, wb)


def _short_conv_kernel(x_ref, prev_ref, next_ref, w_ref, b_ref, *o_refs, n_tiles):
    i = pl.program_id(1)
    x = x_ref[...].astype(F32)
    t = x.shape[0]
    row = lax.broadcasted_iota(jnp.int32, x.shape, 0)
    halo = prev_ref.shape[0]
    before = jnp.where(i > 0, prev_ref[...].astype(F32)[halo - 1:halo, :], 0.0)
    after = jnp.where(i < n_tiles - 1, next_ref[...].astype(F32)[0:1, :], 0.0)
    xm1 = jnp.where(row == 0, before, pltpu.roll(x, 1, 0))
    xp1 = jnp.where(row == t - 1, after, pltpu.roll(x, t - 1, 0))
    w = w_ref[...]
    y = w[0:1, :] * xm1 + w[1:2, :] * x + w[2:3, :] * xp1 + b_ref[...]
    cw = o_refs[0].shape[-1]
    for n, o_ref in enumerate(o_refs):
        o_ref[...] = y[:, n * cw:(n + 1) * cw].astype(o_ref.dtype)


def short_conv(p, conv_w, conv_b, n_out):
    b, l, c = p.shape
    cw = c // n_out
    halo = 16
    t = _pick_tile(l, 512, halo)
    n_tiles = l // t
    hb = t // halo
    kern = functools.partial(_short_conv_kernel, n_tiles=n_tiles)
    return pl.pallas_call(
        kern, grid=(b, n_tiles),
        in_specs=[pl.BlockSpec((None, t, c), lambda bi, i: (bi, i, 0)),
                  pl.BlockSpec((None, halo, c), lambda bi, i: (bi, jnp.maximum(i * hb - 1, 0), 0)),
                  pl.BlockSpec((None, halo, c), lambda bi, i: (bi, jnp.minimum((i + 1) * hb, l // halo - 1), 0)),
                  pl.BlockSpec((8, c), lambda bi, i: (0, 0)),
                  pl.BlockSpec((1, c), lambda bi, i: (0, 0))],
        out_specs=[pl.BlockSpec((None, t, cw), lambda bi, i: (bi, i, 0))] * n_out,
        out_shape=[jax.ShapeDtypeStruct((b, l, cw), BF16)] * n_out,
        compiler_params=_cparams("parallel", "parallel"),
        name="short_conv",
    )(p, p, p, jnp.pad(conv_w, ((0, 8 - conv_w.shape[0]), (0, 0))), conv_b.reshape(1, c))


def _hy_filter_kernel(fv_ref, c2_ref, s2_ref, w1_ref, b1_ref, sf1_ref, w2_ref, b2_ref, sf2_ref, w3f_ref, w3b_ref,
                      dl_ref, f_ref, ss_ref, cb_ref, sb_ref, *, seq_len):
    i = pl.program_id(0)
    tl = f_ref.shape[1]
    hi = lax.Precision.HIGHEST
    step = 2.0 * math.pi / float(seq_len)
    r = lax.broadcasted_iota(jnp.int32, (tl, 1), 0)
    lane = lax.broadcasted_iota(jnp.int32, (1, LANES), 1)

    @pl.when(i == 0)
    def _():
        ang_r = (step * r.astype(F32)) * fv_ref[...]
        cb_ref[...] = jnp.cos(ang_r)
        sb_ref[...] = jnp.sin(ang_r)
        ss_ref[...] = jnp.zeros(ss_ref.shape, F32)

    base = (step * (i * tl).astype(F32)) * fv_ref[...]
    ca, sa = jnp.cos(base), jnp.sin(base)
    cb, sb = cb_ref[...], sb_ref[...]
    cos_j, sin_j = ca * cb - sa * sb, sa * cb + ca * sb
    cos_m, sin_m = c2_ref[...] * cos_j + s2_ref[...] * sin_j, s2_ref[...] * cos_j - c2_ref[...] * sin_j
    j = (i * tl + r).astype(F32)

    def mlp(t, cos_t, sin_t):
        t_norm = t / float(max(seq_len - 1, 1))
        feat = jnp.where(lane == 0, t_norm,
                         jnp.where(lane <= HY_BANDS, cos_t, jnp.where(lane <= 2 * HY_BANDS, -sin_t, 0.0)))
        z = jnp.sin(sf1_ref[...] * (jnp.dot(feat, w1_ref[...], precision=hi, preferred_element_type=F32)
                                    + b1_ref[...]))
        z = jnp.sin(sf2_ref[...] * (jnp.dot(z, w2_ref[...], precision=hi, preferred_element_type=F32)
                                    + b2_ref[...]))
        return z.astype(BF16), t_norm

    zf, tnf = mlp(j, cos_j, sin_j)
    hf = jnp.dot(zf, w3f_ref[...], preferred_element_type=F32) * jnp.exp(-tnf * dl_ref[...])
    zb, tnb = mlp(float(seq_len) - j, cos_m, sin_m)
    hb = jnp.dot(zb, w3b_ref[...], preferred_element_type=F32) * jnp.exp(-tnb * dl_ref[...])
    hb = jnp.where(j > 0.0, hb, 0.0)
    f_ref[0] = hf
    f_ref[1] = hb
    ss_ref[...] += (jnp.sum(hf * hf, axis=0, keepdims=True) + jnp.sum(hb * hb, axis=0, keepdims=True))


def hy_filters(seq_len, w1, b1, w2, b2, w3, sin_freq, hy_w):
    emb, ffn = w1.shape
    cw = 2 * hy_w
    f = np.linspace(1e-4, HY_BANDS - 1, HY_BANDS)
    fv = np.zeros((1, LANES), np.float32)
    fv[0, 1:1 + HY_BANDS] = f
    fv[0, 1 + HY_BANDS:1 + 2 * HY_BANDS] = f
    c2 = np.cos(2.0 * np.pi * fv.astype(np.float64)).astype(np.float32)
    s2 = np.sin(2.0 * np.pi * fv.astype(np.float64)).astype(np.float32)
    w1p = jnp.pad(w1, ((0, LANES - emb), (0, 0)))
    w3r = w3.reshape(ffn, 2, 2, hy_w)
    w3f = w3r[:, :, 0, :].reshape(ffn, cw).astype(BF16)
    w3b = w3r[:, :, 1, :].reshape(ffn, cw).astype(BF16)
    deltas = np.abs(np.linspace(math.log(HY_DECAY_TARGET) / HY_SLOW_PCT,
                                math.log(HY_DECAY_TARGET) / HY_FAST_PCT, hy_w)).astype(np.float32)
    dl = jnp.asarray(np.tile(deltas, 2).reshape(1, cw))
    tl = _pick_tile(seq_len, 512, 8)
    const = lambda i: (0, 0)
    kern = functools.partial(_hy_filter_kernel, seq_len=seq_len)
    return pl.pallas_call(
        kern, grid=(seq_len // tl,),
        in_specs=[pl.BlockSpec((1, LANES), const), pl.BlockSpec((1, LANES), const), pl.BlockSpec((1, LANES), const),
                  pl.BlockSpec((LANES, ffn), const),
                  pl.BlockSpec((1, ffn), const), pl.BlockSpec((1, ffn), const),
                  pl.BlockSpec((ffn, ffn), const), pl.BlockSpec((1, ffn), const), pl.BlockSpec((1, ffn), const),
                  pl.BlockSpec((ffn, cw), const), pl.BlockSpec((ffn, cw), const), pl.BlockSpec((1, cw), const)],
        out_specs=[pl.BlockSpec((2, tl, cw), lambda i: (0, i, 0)), pl.BlockSpec((1, cw), const)],
        out_shape=[jax.ShapeDtypeStruct((2, seq_len, cw), F32), jax.ShapeDtypeStruct((1, cw), F32)],
        scratch_shapes=[pltpu.VMEM((tl, LANES), F32), pltpu.VMEM((tl, LANES), F32)],
        compiler_params=_cparams("arbitrary"),
        name="hy_filters",
    )(jnp.asarray(fv), jnp.asarray(c2), jnp.asarray(s2), w1p, b1.reshape(1, ffn), sin_freq[0].reshape(1, ffn), w2,
      b2.reshape(1, ffn), sin_freq[1].reshape(1, ffn), w3f, w3b, dl)


def _phase_mats(phase_num, denom, conj):
    ang = np.pi * (phase_num % (2 * denom)).astype(np.float64) / denom
    cr, ci = np.cos(ang), (np.sin(ang) if conj else -np.sin(ang))
    return cr, ci


def _stage1_mats(n1):
    k = np.arange(n1)[:, None]
    n = np.arange(n1)[None, :]
    w1d, w1f, wfin = [], [], []
    for v in (0, 1):
        cr, ci = _phase_mats(2 * k * n + v * n, n1, conj=False)
        w1d.append(np.block([[cr, -ci], [ci, cr]]))
        sgn = 1.0 if v == 0 else -1.0
        w1f.append(np.block([[cr, sgn * cr], [ci, sgn * ci]]))
        cri, cii = _phase_mats(2 * n.T * k.T + v * n.T, n1, conj=True)
        wfin.append(np.block([[cri, -cii], [cii, cri]]))
    return (np.concatenate(w1d, 0).astype(np.float32), np.concatenate(w1f, 0).astype(np.float32),
            np.concatenate(wfin, 1).astype(np.float32))


def _stage2_mats(n1, n2):
    l = n1 * n2
    k2 = np.arange(n2)[:, None]
    nn = np.arange(n2)[None, :]
    base = (2 * n1 * k2 * nn) % (2 * l)
    br, bi = np.cos(np.pi * base / l), -np.sin(np.pi * base / l)
    k1 = np.arange(n1)[None, :, None]
    v = np.arange(2)[:, None, None]
    tw = (2 * np.arange(n2)[None, None, :] * k1 + v * np.arange(n2)[None, None, :]) % (2 * l)
    tr, ti = np.cos(np.pi * tw / l), -np.sin(np.pi * tw / l)
    br, bi, tr, ti = (jnp.asarray(a, F32) for a in (br, bi, tr, ti))
    cr = br[None, None] * tr[:, :, None, :] - bi[None, None] * ti[:, :, None, :]
    ci = br[None, None] * ti[:, :, None, :] + bi[None, None] * tr[:, :, None, :]
    fwd = jnp.concatenate([jnp.concatenate([cr, -ci], -1), jnp.concatenate([ci, cr], -1)], -2)
    crt, cit = jnp.swapaxes(cr, -1, -2), -jnp.swapaxes(ci, -1, -2)
    inv = jnp.concatenate([jnp.concatenate([crt, -cit], -1), jnp.concatenate([cit, crt], -1)], -2)
    return fwd.astype(BF16), inv.astype(BF16)


def _fft_stage1_kernel(z_ref, w_ref, o_ref):
    n1 = z_ref.shape[1]
    x = z_ref[...].reshape(2 * n1, z_ref.shape[2]).astype(BF16)
    y = jnp.dot(w_ref[...], x, preferred_element_type=F32)
    o_ref[...] = y.reshape(o_ref.shape).astype(o_ref.dtype)


FFT_COLS = 2048


def fft_stage1(z, w1, n1, n2):
    w = z.shape[2]
    cols = n2 * w
    tc = _pick_tile(cols, FFT_COLS, LANES)
    out = pl.pallas_call(
        _fft_stage1_kernel, grid=(cols // tc,),
        in_specs=[pl.BlockSpec((2, n1, tc), lambda j: (0, 0, j)),
                  pl.BlockSpec((4 * n1, 2 * n1), lambda j: (0, 0))],
        out_specs=pl.BlockSpec((2, 2, n1, tc), lambda j: (0, 0, 0, j)),
        out_shape=jax.ShapeDtypeStruct((2, 2, n1, cols), BF16),
        compiler_params=_cparams("parallel"),
        name="fft_stage1",
    )(z.reshape(2, n1, cols), w1)
    return out.reshape(2, 2, n1, n2, w)


def _fft_mid_kernel(a_ref, f_ref, ss_ref, wf_ref, wi_ref, o_ref, *, scale):
    kb, n2, cw = a_ref.shape[1], a_ref.shape[2], a_ref.shape[3]
    nrm = lax.rsqrt(ss_ref[...] + EPS) * scale
    for kk in range(kb):
        wf = wf_ref[kk]
        t = jnp.dot(wf, a_ref[:, kk].reshape(2 * n2, cw), preferred_element_type=F32)
        g = jnp.dot(wf, f_ref[:, kk].reshape(2 * n2, cw), preferred_element_type=F32) * nrm
        tr, ti, gr, gi = t[:n2], t[n2:], g[:n2], g[n2:]
        p = jnp.concatenate([tr * gr - ti * gi, tr * gi + ti * gr], axis=0).astype(BF16)
        u = jnp.dot(wi_ref[kk], p, preferred_element_type=F32)
        o_ref[:, kk] = u.reshape(2, n2, cw).astype(o_ref.dtype)


FFT_MID_K1 = 4


def fft_mid(a, af, ss, wf, wi, order, n1, n2):
    cw = a.shape[-1]
    kb = _pick_tile(n1, FFT_MID_K1, 1)
    kern = functools.partial(_fft_mid_kernel, scale=1.0 / (2.0 * n1 * n2))
    mat_spec = pl.BlockSpec((None, kb, 2 * n2, 2 * n2), lambda v, k: (v, k, 0, 0))
    return pl.pallas_call(
        kern, grid=(2, n1 // kb),
        in_specs=[pl.BlockSpec((None, 2, kb, n2, cw), lambda v, k: (v, 0, k, 0, 0)),
                  pl.BlockSpec((None, 2, kb, n2, cw), lambda v, k: (v, 0, k, 0, order)),
                  pl.BlockSpec((1, cw), lambda v, k: (0, order)),
                  mat_spec, mat_spec],
        out_specs=pl.BlockSpec((None, 2, kb, n2, cw), lambda v, k: (v, 0, k, 0, 0)),
        out_shape=jax.ShapeDtypeStruct(a.shape, BF16),
        compiler_params=_cparams("parallel", "parallel"),
        name="fft_mid",
    )(a, af, ss, wf, wi)


def _fft_final_kernel(u_ref, w_ref, gate_ref, z_ref, bias_ref, o_ref):
    n1 = gate_ref.shape[1]
    cw = gate_ref.shape[2]
    u = u_ref[...].reshape(4 * n1, cw)
    y = jnp.dot(w_ref[...], u, preferred_element_type=F32).reshape(2, n1, cw)
    z = z_ref[...].astype(F32)
    o_ref[...] = (gate_ref[...].astype(F32) * (y + bias_ref[...] * z)).astype(o_ref.dtype)


def fft_final(u, wfin, gate, z, bias, n1, n2):
    cw = z.shape[2]
    cols = n2 * cw
    tc = _pick_tile(cols, FFT_COLS, cw)
    out = pl.pallas_call(
        _fft_final_kernel, grid=(cols // tc,),
        in_specs=[pl.BlockSpec((2, 2, n1, tc), lambda j: (0, 0, 0, j)),
                  pl.BlockSpec((2 * n1, 4 * n1), lambda j: (0, 0)),
                  pl.BlockSpec((2, n1, tc), lambda j: (0, 0, j)),
                  pl.BlockSpec((2, n1, tc), lambda j: (0, 0, j)),
                  pl.BlockSpec((1, tc), lambda j: (0, 0))],
        out_specs=pl.BlockSpec((2, n1, tc), lambda j: (0, 0, j)),
        out_shape=jax.ShapeDtypeStruct((2, n1, cols), BF16),
        compiler_params=_cparams("parallel"),
        name="fft_final",
    )(u.reshape(2, 2, n1, cols), wfin, gate.reshape(2, n1, cols), z.reshape(2, n1, cols),
      jnp.tile(bias.reshape(1, cw), (1, tc // cw)))
    return out.reshape(2, n1 * n2, cw)


def _hy_dense_kernel(z_ref, gate_ref, f_ref, ss_ref, bias_ref, wf_ref, wi_ref, o_ref):
    lc = z_ref.shape[1]
    cw = z_ref.shape[2]
    nrm = lax.rsqrt(ss_ref[...] + EPS) * (1.0 / (2.0 * lc))
    f1, f2 = f_ref[0], f_ref[1]
    filt = (((f1 + f2) * nrm).astype(BF16), ((f1 - f2) * nrm).astype(BF16))
    x = z_ref[...].reshape(2 * lc, cw)
    acc = None
    for v in (0, 1):
        wf = wf_ref[v]
        t = jnp.dot(wf, x, preferred_element_type=F32)
        g = jnp.dot(wf[:, :lc], filt[v], preferred_element_type=F32)
        tr, ti, gr, gi = t[:lc], t[lc:], g[:lc], g[lc:]
        p = jnp.concatenate([tr * gr - ti * gi, tr * gi + ti * gr], axis=0).astype(BF16)
        u = jnp.dot(wi_ref[v], p, preferred_element_type=F32)
        acc = u if acc is None else acc + u
    y = acc.reshape(2, lc, cw)
    o_ref[...] = (gate_ref[...].astype(F32) * (y + bias_ref[...] * z_ref[...].astype(F32))).astype(o_ref.dtype)


def hy_dense_conv(z_arr, gate_arr, f, ss, order, bias, wf, wi):
    lc, cw = z_arr.shape[1], z_arr.shape[2]
    return pl.pallas_call(
        _hy_dense_kernel, grid=(1,),
        in_specs=[pl.BlockSpec((2, lc, cw), lambda i: (0, 0, 0)),
                  pl.BlockSpec((2, lc, cw), lambda i: (0, 0, 0)),
                  pl.BlockSpec((2, lc, cw), lambda i: (0, 0, order)),
                  pl.BlockSpec((1, cw), lambda i: (0, order)),
                  pl.BlockSpec((1, cw), lambda i: (0, 0)),
                  pl.BlockSpec((2, None, 2 * lc, 2 * lc), lambda i: (0, 0, 0, 0)),
                  pl.BlockSpec((2, None, 2 * lc, 2 * lc), lambda i: (0, 0, 0, 0))],
        out_specs=pl.BlockSpec((2, lc, cw), lambda i: (0, 0, 0)),
        out_shape=jax.ShapeDtypeStruct((2, lc, cw), BF16),
        compiler_params=_cparams("arbitrary"),
        name="hy_dense_conv",
    )(z_arr, gate_arr, f, ss, bias.reshape(1, cw), wf, wi)


def hyena_mixer(p, conv_w, conv_b, w1, b1, w2, b2, w3, sin_freq, bias, mats):
    b, l, c3 = p.shape
    assert b == 2, "batch elements are packed as the re/im parts of one complex signal"
    hy_w = c3 // 3
    v, x1, x2 = short_conv(p, conv_w, conv_b, 3)
    f, ss = hy_filters(l, w1, b1, w2, b2, w3, sin_freq, hy_w)
    if l <= 2 * FFT_N2:
        wf, wi = mats["dense"]
        z2 = hy_dense_conv(v, x1, f, ss, 0, bias[0], wf, wi)
        return hy_dense_conv(z2, x2, f, ss, 1, bias[1], wf, wi)
    n2 = FFT_N2
    n1 = l // n2
    w1d, w1f, wfin, wf, wi = mats["fft"]
    af = fft_stage1(f, w1f, n1, n2)
    z2 = fft_final(fft_mid(fft_stage1(v, w1d, n1, n2), af, ss, wf, wi, 0, n1, n2), wfin, x1, v, bias[0], n1, n2)
    return fft_final(fft_mid(fft_stage1(z2, w1d, n1, n2), af, ss, wf, wi, 1, n1, n2), wfin, x2, z2, bias[1], n1, n2)


def hyena_mats(l_lat, l_ctx):
    n1 = l_lat // FFT_N2
    w1d, w1f, wfin = _stage1_mats(n1)
    wf, wi = _stage2_mats(n1, FFT_N2)
    mats = {"fft": (jnp.asarray(w1d, BF16), jnp.asarray(w1f, BF16), jnp.asarray(wfin, BF16), wf, wi)}
    mats["dense"] = _stage2_mats(1, l_ctx)
    return mats


def _ctx_attn_kernel(q_ref, k_ref, v_ref, o_ref):
    s = lax.dot_general(q_ref[...], k_ref[...], NT_DIMS, preferred_element_type=F32)
    m = jnp.max(s, axis=-1, keepdims=True)
    p = jnp.exp(s - m)
    l = jnp.sum(p, axis=-1, keepdims=True)
    o = jnp.dot(p.astype(BF16), v_ref[...], preferred_element_type=F32) / l
    o_ref[...] = o.astype(o_ref.dtype)


def ctx_attention(qa, q_off, ka, k_off, va, v_off, n_q_heads, group):
    b, _, lc, d = qa.shape
    return pl.pallas_call(
        _ctx_attn_kernel, grid=(b, n_q_heads),
        in_specs=[pl.BlockSpec((None, None, lc, d), lambda bi, h: (bi, q_off + h, 0, 0)),
                  pl.BlockSpec((None, None, lc, d), lambda bi, h: (bi, k_off + h // group, 0, 0)),
                  pl.BlockSpec((None, None, lc, d), lambda bi, h: (bi, v_off + h // group, 0, 0))],
        out_specs=pl.BlockSpec((None, lc, d), lambda bi, h: (bi, 0, h)),
        out_shape=jax.ShapeDtypeStruct((b, lc, n_q_heads * d), BF16),
        compiler_params=_cparams("parallel", "parallel"),
        name="ctx_attention",
    )(qa, ka, va)


NA_TILE_ROWS = 4
NA_KEY_ROWS = 12


def _na_geometry(n_rows):
    wr = min(NA_WIN_R, n_rows)
    nt = n_rows // NA_TILE_ROWS
    sigs = []
    for t in range(nt):
        rt = t * NA_TILE_ROWS
        w0 = int(np.clip(rt - wr // 2, 0, n_rows - NA_KEY_ROWS))
        r = rt + np.arange(NA_TILE_ROWS)
        r0 = np.clip(r - wr // 2, 0, n_rows - wr)
        sigs.append((w0 - rt, tuple((r0 - w0).tolist())))
    classes = (sigs[0], sigs[1], sigs[-1])
    for t, s in enumerate(sigs):
        assert s == classes[0 if t == 0 else (2 if t == nt - 1 else 1)], "tile does not match its bias class"
    return wr, nt, classes


def _na_bias_tables(rpb, n_rows):
    wr, _, classes = _na_geometry(n_rows)
    cols = np.arange(GRID_W)
    c0 = np.clip(cols - NA_WIN_C // 2, 0, GRID_W - NA_WIN_C)
    col_ok = (cols[None, :] >= c0[:, None]) & (cols[None, :] < c0[:, None] + NA_WIN_C)
    col_idx = np.clip(cols[None, :] - cols[:, None] + NA_WIN_C - 1, 0, 2 * NA_WIN_C - 2)
    col_sel = (col_idx[None] == np.arange(2 * NA_WIN_C - 1)[:, None, None]).astype(np.float32)
    row_sel, oks = [], []
    for off, rel_r0 in classes:
        qi = np.arange(NA_TILE_ROWS)[:, None]
        kw = np.arange(NA_KEY_ROWS)[None, :]
        r0 = np.asarray(rel_r0)[:, None]
        row_ok = (kw >= r0) & (kw < r0 + wr)
        row_idx = np.clip(kw + off - qi + NA_WIN_R - 1, 0, 2 * NA_WIN_R - 2)
        row_sel.append((row_idx[None] == np.arange(2 * NA_WIN_R - 1)[:, None, None]).astype(np.float32))
        oks.append(row_ok[:, None, :, None] & col_ok[None, :, None, :])
    vals = jnp.einsum("hab,saqk,bcd->hsqckd", rpb.astype(F32), jnp.asarray(np.stack(row_sel)),
                      jnp.asarray(col_sel), precision=lax.Precision.HIGHEST)
    tab = jnp.where(jnp.asarray(np.stack(oks))[None], vals, NEG_BIG)
    return tab.reshape(rpb.shape[0], 3, NA_TILE_ROWS * GRID_W, NA_KEY_ROWS * GRID_W)


def _na_kernel(q_ref, k_ref, v_ref, kc_ref, vc_ref, bias_ref, o_ref, *, n_rows, win_half):
    t = pl.program_id(2)
    w0 = jnp.clip(t * NA_TILE_ROWS - win_half, 0, n_rows - NA_KEY_ROWS)
    start = pl.multiple_of(w0 * GRID_W, GRID_W)
    kwin = k_ref[pl.ds(start, NA_KEY_ROWS * GRID_W), :]
    vwin = v_ref[pl.ds(start, NA_KEY_ROWS * GRID_W), :]
    q = q_ref[...]
    s_lat = lax.dot_general(q, kwin, NT_DIMS, preferred_element_type=F32) + bias_ref[...]
    s_ctx = lax.dot_general(q, kc_ref[...], NT_DIMS, preferred_element_type=F32)
    m = jnp.maximum(jnp.max(s_lat, axis=-1, keepdims=True), jnp.max(s_ctx, axis=-1, keepdims=True))
    p_lat = jnp.exp(s_lat - m)
    p_ctx = jnp.exp(s_ctx - m)
    l = jnp.sum(p_lat, axis=-1, keepdims=True) + jnp.sum(p_ctx, axis=-1, keepdims=True)
    o = (jnp.dot(p_lat.astype(BF16), vwin, preferred_element_type=F32)
         + jnp.dot(p_ctx.astype(BF16), vc_ref[...], preferred_element_type=F32)) / l
    o_ref[...] = o.astype(o_ref.dtype)


def na_attention(pl_heads, pc_heads, rpb):
    b, h3, l, d = pl_heads.shape
    lc = pc_heads.shape[2]
    h = h3 // 3
    n_rows = l // GRID_W
    wr, nt, _ = _na_geometry(n_rows)
    bias = _na_bias_tables(rpb, n_rows)
    tq = NA_TILE_ROWS * GRID_W
    kw = NA_KEY_ROWS * GRID_W
    kern = functools.partial(_na_kernel, n_rows=n_rows, win_half=wr // 2)

    def cls(t):
        return jnp.where(t == 0, 0, jnp.where(t == nt - 1, 2, 1))

    return pl.pallas_call(
        kern, grid=(b, h, nt),
        in_specs=[pl.BlockSpec((None, None, tq, d), lambda bi, hi, t: (bi, hi, t, 0)),
                  pl.BlockSpec((None, None, l, d), lambda bi, hi, t: (bi, h + hi, 0, 0)),
                  pl.BlockSpec((None, None, l, d), lambda bi, hi, t: (bi, 2 * h + hi, 0, 0)),
                  pl.BlockSpec((None, None, lc, d), lambda bi, hi, t: (bi, h + hi, 0, 0)),
                  pl.BlockSpec((None, None, lc, d), lambda bi, hi, t: (bi, 2 * h + hi, 0, 0)),
                  pl.BlockSpec((None, None, tq, kw), lambda bi, hi, t: (hi, cls(t), 0, 0))],
        out_specs=pl.BlockSpec((None, tq, d), lambda bi, hi, t: (bi, t, hi)),
        out_shape=jax.ShapeDtypeStruct((b, l, h * d), BF16),
        compiler_params=_cparams("parallel", "parallel", "parallel"),
        name="na_attention",
    )(pl_heads, pl_heads, pl_heads, pc_heads, pc_heads, bias)


def _ret_kernel(*refs, reverse, rope, finalize, n_heads):
    it = iter(refs)
    lg_ref, p_ref = next(it), next(it)
    cos_ref, sin_ref = (next(it), next(it)) if rope else (None, None)
    s0_ref = next(it)
    oprev_ref, gnw_ref = (next(it), next(it)) if finalize else (None, None)
    o_ref, sfin_ref, state_ref = next(it), next(it), next(it)

    i = pl.program_id(1)
    n_steps = pl.num_programs(1)

    @pl.when(i == 0)
    def _():
        state_ref[...] = s0_ref[...]

    c = RET_CHUNK
    ts = p_ref.shape[0]
    nc = ts // c
    dq = n_heads * RET_DK
    dv = n_heads * RET_DV
    x = p_ref[...]
    q = x[:, :dq].astype(F32)
    k = x[:, dq:2 * dq].astype(F32)
    v = x[:, 2 * dq:2 * dq + dv]
    if rope:
        q = q * cos_ref[...] + _swap32(q) * sin_ref[...]
        k = k * cos_ref[...] + _swap32(k) * sin_ref[...]
    jr = lax.broadcasted_iota(jnp.int32, (c, c), 0)
    jc = lax.broadcasted_iota(jnp.int32, (c, c), 1)
    rel = ((jc - jr) if reverse else (jr - jc)).astype(F32)
    jcol = lax.broadcasted_iota(jnp.int32, (c, 1), 0).astype(F32)
    order = range(nc - 1, -1, -1) if reverse else range(nc)
    blocks = [[None] * n_heads for _ in range(nc)]
    for h in range(n_heads):
        g = lg_ref[h]
        intra = jnp.where(rel >= 0.0, jnp.exp(g * jnp.maximum(rel, 0.0)), 0.0)
        if reverse:
            cross_f = jnp.exp(g * (float(c) - jcol))
            k_dec = jnp.exp(g * jcol)
        else:
            cross_f = jnp.exp(g * (jcol + 1.0))
            k_dec = jnp.exp(g * (float(c - 1) - jcol))
        chunk_decay = jnp.exp(g * float(c))
        state = state_ref[h]
        for ci in order:
            rows = slice(ci * c, (ci + 1) * c)
            qh = q[rows, h * RET_DK:(h + 1) * RET_DK].astype(BF16)
            kf = k[rows, h * RET_DK:(h + 1) * RET_DK]
            vh = v[rows, h * RET_DV:(h + 1) * RET_DV]
            scores = lax.dot_general(qh, kf.astype(BF16), NT_DIMS, preferred_element_type=F32) * intra
            inner = jnp.dot(scores.astype(BF16), vh, preferred_element_type=F32)
            cross = jnp.dot(qh, state.astype(BF16), preferred_element_type=F32) * cross_f
            blocks[ci][h] = inner + cross
            kv = lax.dot_general((kf * k_dec).astype(BF16), vh, TN_DIMS, preferred_element_type=F32)
            state = chunk_decay * state + kv
        state_ref[h] = state
    o = jnp.concatenate([jnp.concatenate(blocks[ci], axis=1) for ci in range(nc)], axis=0)
    if finalize:
        o = o + oprev_ref[...]
        gate = x[:, 2 * dq + dv:2 * dq + 2 * dv].astype(F32)
        normed = []
        for h in range(n_heads):
            oh = o[:, h * RET_DV:(h + 1) * RET_DV]
            normed.append(oh * lax.rsqrt(jnp.mean(oh * oh, axis=-1, keepdims=True) + EPS))
        o = jnp.concatenate(normed, axis=1) * gnw_ref[...] * _silu(gate)
    o_ref[...] = o.astype(o_ref.dtype)

    @pl.when(i == n_steps - 1)
    def _():
        sfin_ref[...] = state_ref[...]


def retention_pass(p, log_g, s0, reverse, rope_tables=None, o_prev=None, gn_w=None):
    b, l, w = p.shape
    n_heads = w // (2 * RET_DK + 2 * RET_DV)
    dq, dv = n_heads * RET_DK, n_heads * RET_DV
    ts = _pick_tile(l, 512, RET_CHUNK)
    n_steps = l // ts
    finalize = o_prev is not None
    rope = rope_tables is not None

    def tile(i):
        return (n_steps - 1 - i) if reverse else i

    in_specs = [pl.BlockSpec(memory_space=pltpu.SMEM),
                pl.BlockSpec((None, ts, w), lambda bi, i: (bi, tile(i), 0))]
    args = [log_g, p]
    if rope:
        in_specs += [pl.BlockSpec((ts, dq), lambda bi, i: (tile(i), 0))] * 2
        args += list(rope_tables)
    in_specs.append(pl.BlockSpec((None, n_heads, RET_DK, RET_DV), lambda bi, i: (bi, 0, 0, 0)))
    args.append(s0)
    if finalize:
        in_specs += [pl.BlockSpec((None, ts, dv), lambda bi, i: (bi, tile(i), 0)),
                     pl.BlockSpec((1, dv), lambda bi, i: (0, 0))]
        args += [o_prev, gn_w.reshape(1, dv)]
    kern = functools.partial(_ret_kernel, reverse=reverse, rope=rope, finalize=finalize, n_heads=n_heads)
    return pl.pallas_call(
        kern, grid=(b, n_steps),
        in_specs=in_specs,
        out_specs=[pl.BlockSpec((None, ts, dv), lambda bi, i: (bi, tile(i), 0)),
                   pl.BlockSpec((None, n_heads, RET_DK, RET_DV), lambda bi, i: (bi, 0, 0, 0))],
        out_shape=[jax.ShapeDtypeStruct((b, l, dv), BF16 if finalize else F32),
                   jax.ShapeDtypeStruct((b, n_heads, RET_DK, RET_DV), F32)],
        scratch_shapes=[pltpu.VMEM((n_heads, RET_DK, RET_DV), F32)],
        compiler_params=_cparams("parallel", "arbitrary"),
        name="retention_pass",
    )(*args)


def _ret_rope_tables(l, n_heads):
    half = RET_DK // 2
    inv = RET_ROPE_BASE ** (-jnp.linspace(0.0, 1.0, half, dtype=F32))
    ang = jnp.arange(l, dtype=F32)[:, None] * inv
    cos, sin = jnp.cos(ang), jnp.sin(ang)
    return (jnp.tile(jnp.concatenate([cos, cos], -1), (1, n_heads)),
            jnp.tile(jnp.concatenate([-sin, sin], -1), (1, n_heads)))


def retention_mixer(p_l, p_c, log_decay, gn_w, tables, with_ctx_out):
    b = p_l.shape[0]
    n_heads = p_l.shape[2] // (2 * RET_DK + 2 * RET_DV)
    log_g = -jnp.abs(log_decay.astype(F32))
    s0 = jnp.zeros((b, n_heads, RET_DK, RET_DV), F32)
    o_cf, s_fwd = retention_pass(p_c, log_g[0], s0, False)
    y_c, s_bwd = retention_pass(p_c, log_g[1], s0, True, o_prev=o_cf, gn_w=gn_w)
    o_lf, _ = retention_pass(p_l, log_g[0], s_fwd, False, rope_tables=tables)
    y_l, _ = retention_pass(p_l, log_g[1], s_bwd, True, rope_tables=tables, o_prev=o_lf, gn_w=gn_w)
    return y_l, (y_c if with_ctx_out else None)


def _gqa_prep_kernel(*refs, rope, n_q, n_kv, q_scale):
    if rope:
        p_ref, cos_ref, sin_ref, qw_ref, kw_ref, q_out, k_out, v_out = refs
    else:
        p_ref, qw_ref, kw_ref, q_out, k_out, v_out = refs
    x = p_ref[...]
    d = HEAD_DIM

    def norm_rope(xh, w):
        xh = xh.astype(F32)
        y = xh * lax.rsqrt(jnp.mean(xh * xh, axis=-1, keepdims=True) + EPS) * w
        if rope:
            y = y * cos_ref[...] + _swap32(y) * sin_ref[...]
        return y

    for h in range(n_q):
        q_out[h] = (norm_rope(x[:, h * d:(h + 1) * d], qw_ref[...]) * q_scale).astype(q_out.dtype)
    for h in range(n_kv):
        k_out[h] = norm_rope(x[:, (n_q + h) * d:(n_q + h + 1) * d], kw_ref[...]).astype(k_out.dtype)
        lane = lax.broadcasted_iota(jnp.int32, (x.shape[0], d), 1)
        ones_col = jnp.where(lane == 0, 1.0, 0.0).astype(v_out.dtype)
        v_out[h] = jnp.concatenate([x[:, (n_q + n_kv + h) * d:(n_q + n_kv + h + 1) * d], ones_col], axis=1)


def gqa_prep(p, qn_w, kn_w, n_q, n_kv, q_scale, rope_tables=None):
    b, l, w = p.shape
    d = HEAD_DIM
    t = _pick_tile(l, 512, 16)
    rope = rope_tables is not None
    in_specs = [pl.BlockSpec((None, t, w), lambda bi, i: (bi, i, 0))]
    args = [p]
    if rope:
        in_specs += [pl.BlockSpec((t, d), lambda bi, i: (i, 0))] * 2
        args += list(rope_tables)
    in_specs += [pl.BlockSpec((1, d), lambda bi, i: (0, 0))] * 2
    args += [qn_w.reshape(1, d), kn_w.reshape(1, d)]
    kern = functools.partial(_gqa_prep_kernel, rope=rope, n_q=n_q, n_kv=n_kv, q_scale=q_scale)
    return pl.pallas_call(
        kern, grid=(b, l // t),
        in_specs=in_specs,
        out_specs=[pl.BlockSpec((None, n_q, t, d), lambda bi, i: (bi, 0, i, 0)),
                   pl.BlockSpec((None, n_kv, t, d), lambda bi, i: (bi, 0, i, 0)),
                   pl.BlockSpec((None, n_kv, t, 2 * d), lambda bi, i: (bi, 0, i, 0))],
        out_shape=[jax.ShapeDtypeStruct((b, n_q, l, d), BF16),
                   jax.ShapeDtypeStruct((b, n_kv, l, d), BF16),
                   jax.ShapeDtypeStruct((b, n_kv, l, 2 * d), BF16)],
        compiler_params=_cparams("parallel", "parallel"),
        name="gqa_prep",
    )(*args)


def _axial_rope_tables(l):
    nf = HEAD_DIM // 4
    t = jnp.arange(l)
    inv = ROPE_BASE ** (-jnp.arange(nf, dtype=F32) / nf)
    ang_r = (t // GRID_W).astype(F32)[:, None] * inv
    ang_c = (t % GRID_W).astype(F32)[:, None] * inv
    cr, sr, cc, sc = jnp.cos(ang_r), jnp.sin(ang_r), jnp.cos(ang_c), jnp.sin(ang_c)
    return (jnp.concatenate([cr, cr, cc, cc], -1), jnp.concatenate([-sr, sr, -sc, sc], -1))


FLASH_SUB_ROWS = 128


def _flash_kernel(q_ref, k_ref, v_ref, o_ref, m_ref, acc_ref):
    j = pl.program_id(3)
    g, tq, d = q_ref.shape

    @pl.when(j == 0)
    def _():
        m_ref[...] = jnp.full(m_ref.shape, NEG_BIG, F32)
        acc_ref[...] = jnp.zeros(acc_ref.shape, F32)

    q = q_ref[...].reshape(g * tq, d)
    k = k_ref[...]
    v = v_ref[...]
    n_sub = (g * tq) // FLASH_SUB_ROWS
    s, p, alpha = [None] * n_sub, [None] * n_sub, [None] * n_sub
    for t in range(n_sub + 2):
        if t < n_sub:
            s[t] = lax.dot_general(q[t * FLASH_SUB_ROWS:(t + 1) * FLASH_SUB_ROWS], k, NT_DIMS,
                                   preferred_element_type=F32)
        u = t - 1
        if 0 <= u < n_sub:
            rows = slice(u * FLASH_SUB_ROWS, (u + 1) * FLASH_SUB_ROWS)
            m_prev = m_ref[rows]
            m_new = jnp.maximum(m_prev, jnp.max(s[u], axis=-1, keepdims=True))
            alpha[u] = jnp.exp2(m_prev - m_new)
            m_ref[rows] = m_new
            p[u] = jnp.exp2((s[u] - m_new).astype(BF16))
            s[u] = None
        w = t - 2
        if 0 <= w < n_sub:
            rows = slice(w * FLASH_SUB_ROWS, (w + 1) * FLASH_SUB_ROWS)
            acc_ref[rows] = alpha[w] * acc_ref[rows] + jnp.dot(p[w], v, preferred_element_type=F32)
            p[w] = None

    @pl.when(j == pl.num_programs(3) - 1)
    def _():
        acc = acc_ref[...]
        o = acc[:, :d] / acc[:, d:d + 1]
        o_ref[...] = jnp.concatenate([o[h * tq:(h + 1) * tq] for h in range(g)], axis=1).astype(o_ref.dtype)


def flash_gqa(q, k, v):
    b, hq, l, d = q.shape
    hkv, lk = k.shape[1], k.shape[2]
    g = hq // hkv
    tq = _pick_tile(l, 1024, 16)
    tk = _pick_tile(lk, 3328, LANES)
    assert (g * tq) % FLASH_SUB_ROWS == 0
    return pl.pallas_call(
        _flash_kernel, grid=(b, hkv, l // tq, lk // tk),
        in_specs=[pl.BlockSpec((None, g, tq, d), lambda bi, h, i, j: (bi, h, i, 0)),
                  pl.BlockSpec((None, None, tk, d), lambda bi, h, i, j: (bi, h, j, 0)),
                  pl.BlockSpec((None, None, tk, 2 * d), lambda bi, h, i, j: (bi, h, j, 0))],
        out_specs=pl.BlockSpec((None, tq, g * d), lambda bi, h, i, j: (bi, i, h)),
        out_shape=jax.ShapeDtypeStruct((b, l, hq * d), BF16),
        scratch_shapes=[pltpu.VMEM((g * tq, 1), F32), pltpu.VMEM((g * tq, 2 * d), F32)],
        compiler_params=_cparams("parallel", "parallel", "parallel", "arbitrary"),
        name="flash_gqa",
    )(q, k, v)


def gqa_mixer(p_l, p_c, qn_w, kn_w, tables, with_ctx_out):
    n_q = p_l.shape[2] // (2 * HEAD_DIM)
    n_kv = n_q // 2
    scale = HEAD_DIM ** -0.5
    q_l, k_l, v_l = gqa_prep(p_l, qn_w, kn_w, n_q, n_kv, scale * math.log2(math.e), rope_tables=tables)
    q_c, k_c, v_c = gqa_prep(p_c, qn_w, kn_w, n_q, n_kv, scale)
    y_l = flash_gqa(q_l, jnp.concatenate([k_l, k_c], axis=2), jnp.concatenate([v_l, v_c], axis=2))
    y_c = ctx_attention(q_c, 0, k_c, 0, v_c, 0, n_q, n_q // n_kv) if with_ctx_out else None
    return y_l, y_c


def _split_w_in(w_in, d_model):
    mix_w = d_model // N_BRANCH
    h = mix_w // HEAD_DIM
    sizes = (3 * mix_w, 3 * h * HEAD_DIM, 2 * h * RET_DK + 2 * h * RET_DV,
             (h + 2 * max(h // 2, 1)) * HEAD_DIM, N_BRANCH * d_model)
    offs = np.cumsum((0,) + sizes)
    w_hy, w_na, w_rt, w_gq, w_gate = (w_in[:, offs[i]:offs[i + 1]] for i in range(5))
    na_scale = jnp.concatenate([jnp.full((h * HEAD_DIM,), HEAD_DIM ** -0.5, F32),
                                jnp.ones((2 * h * HEAD_DIM,), F32)])
    rt_scale = jnp.concatenate([jnp.ones((h * RET_DK,), F32), jnp.full((h * RET_DK,), RET_DK ** -0.5, F32),
                                jnp.ones((2 * h * RET_DV,), F32)])
    return ((w_hy).astype(BF16), (w_na * na_scale).astype(BF16), (w_rt * rt_scale).astype(BF16),
            w_gq.astype(BF16), w_gate.astype(BF16))


def kernel(x, c, ctx, c_ctx, ada_w, ada_b, norm1_w, norm2_w, w_in, hy_conv_w, hy_conv_b, hy_ffn_w1, hy_ffn_b1,
           hy_ffn_w2, hy_ffn_b2, hy_ffn_w3, hy_sin_freq, hy_bias, na_rpb, ret_log_decay, ret_gn_w, gqa_q_norm_w,
           gqa_k_norm_w, w_branch, w_out, ffn_w13, ffn_w2, final_norm_w):
    b, l, d = x.shape
    lc = ctx.shape[1]
    depth = ada_w.shape[0]
    ffn_hidden = ffn_w2.shape[1]
    n_ret_heads = (d // N_BRANCH) // RET_DV

    cs = jnp.zeros((8, d), F32).at[:b].set(c).at[b].set(c_ctx)
    mods = ada_mod(cs, ada_w, ada_b)
    ctx_row = b
    mats = hyena_mats(l, lc)
    ret_tables = _ret_rope_tables(l, n_ret_heads)
    gqa_tables = _axial_rope_tables(l)

    x_l = x.reshape(b * l, d)
    x_c = ctx.reshape(b * lc, d)
    for layer in range(depth):
        last = layer == depth - 1
        mod = mods[layer]
        w_hy, w_na, w_rt, w_gq, w_gate = _split_w_in(w_in[layer], d)
        hy_params = (hy_conv_w[layer], hy_conv_b[layer], hy_ffn_w1[layer], hy_ffn_b1[layer], hy_ffn_w2[layer],
                     hy_ffn_b2[layer], hy_ffn_w3[layer], hy_sin_freq[layer], hy_bias[layer])
        wb = w_branch[layer].astype(BF16)
        wo = w_out[layer].astype(BF16)
        w1 = ffn_w13[layer][:, :ffn_hidden].astype(BF16)
        w3 = ffn_w13[layer][:, ffn_hidden:].astype(BF16)
        w2 = ffn_w2[layer].astype(BF16)

        h_l = norm_mod(x_l, norm1_w[layer], mod, 0, 1, l, 0)
        h_c = norm_mod(x_c, norm1_w[layer], mod, 0, 1, None, ctx_row)

        na_l = matmul_heads(h_l, w_na, b)
        na_c = matmul_heads(h_c, w_na, b)
        rt_l = matmul(h_l, w_rt).reshape(b, l, -1)
        rt_c = matmul(h_c, w_rt).reshape(b, lc, -1)
        gq_l = matmul(h_l, w_gq).reshape(b, l, -1)
        gq_c = matmul(h_c, w_gq).reshape(b, lc, -1)
        hy_l = matmul(h_l, w_hy).reshape(b, l, -1)
        gate_l = matmul(h_l, w_gate)

        y_hy_l = hyena_mixer(hy_l, *hy_params, mats)
        y_na_l = na_attention(na_l, na_c, na_rpb[layer])
        y_rt_l, y_rt_c = retention_mixer(rt_l, rt_c, ret_log_decay[layer], ret_gn_w[layer], ret_tables, not last)
        y_gq_l, y_gq_c = gqa_mixer(gq_l, gq_c, gqa_q_norm_w[layer], gqa_k_norm_w[layer], gqa_tables, not last)

        m_l = merge_branches([y.reshape(b * l, -1) for y in (y_hy_l, y_na_l, y_rt_l, y_gq_l)], gate_l, wb)
        x_l = matmul_residual(m_l, wo, x_l, mod, 2, l, 0)
        h2 = norm_mod(x_l, norm2_w[layer], mod, 3, 4, l, 0)
        x_l = matmul_residual(matmul_swiglu(h2, w1, w3), w2, x_l, mod, 5, l, 0)

        if not last:
            n_na = na_c.shape[1] // 3
            hy_c = matmul(h_c, w_hy).reshape(b, lc, -1)
            gate_c = matmul(h_c, w_gate)
            y_hy_c = hyena_mixer(hy_c, *hy_params, mats)
            y_na_c = ctx_attention(na_c, 0, na_c, n_na, na_c, 2 * n_na, n_na, 1)
            m_c = merge_branches([y.reshape(b * lc, -1) for y in (y_hy_c, y_na_c, y_rt_c, y_gq_c)], gate_c, wb)
            x_c = matmul_residual(m_c, wo, x_c, mod, 2, None, ctx_row)
            h2c = norm_mod(x_c, norm2_w[layer], mod, 3, 4, None, ctx_row)
            x_c = matmul_residual(matmul_swiglu(h2c, w1, w3), w2, x_c, mod, 5, None, ctx_row)

    return final_norm(x_l, final_norm_w).reshape(b, l, d)
```

```python
import functools
import math

import numpy as np
import jax
import jax.numpy as jnp
from jax import lax
from jax.experimental import pallas as pl
from jax.experimental.pallas import tpu as pltpu

F32 = jnp.float32
BF16 = jnp.bfloat16

EPS = 1e-6
GRID_W = 64
HEAD_DIM = 128
N_BRANCH = 4
ROPE_BASE = 10000.0
HY_BANDS = 16
HY_DECAY_TARGET = 1e-2
HY_FAST_PCT = 0.3
HY_SLOW_PCT = 1.5
NA_WIN_R = 8
NA_WIN_C = 16
RET_DK = 64
RET_DV = 128
RET_CHUNK = 128
RET_ROPE_BASE = 10000.0
NEG_BIG = -1e30

LANES = 128
FFT_N2 = 128
VMEM_LIMIT = 56 * 1024 * 1024

NT_DIMS = (((1,), (1,)), ((), ()))
TN_DIMS = (((0,), (0,)), ((), ()))


def _cparams(*sem):
    return pltpu.CompilerParams(dimension_semantics=sem, vmem_limit_bytes=VMEM_LIMIT)


def _pick_tile(n, cap, mult):
    best = None
    for t in range(mult, min(n, cap) + 1, mult):
        if n % t == 0:
            best = t
    assert best is not None, (n, cap, mult)
    return best


def _swap32(x):
    n = x.shape[-1]
    lane = lax.broadcasted_iota(jnp.int32, x.shape, x.ndim - 1)
    up = pltpu.roll(x, n - 32, x.ndim - 1)
    down = pltpu.roll(x, 32, x.ndim - 1)
    return jnp.where((lane % 64) < 32, up, down)


def _silu(x):
    return x * jax.nn.sigmoid(x)


def _ada_kernel(c_ref, w_ref, b_ref, o_ref):
    a = _silu(c_ref[...]).astype(BF16)
    o_ref[...] = jnp.dot(a, w_ref[...].astype(BF16), preferred_element_type=F32) + b_ref[...]


def ada_mod(cs, ada_w, ada_b):
    depth, d, n = ada_w.shape
    tn = _pick_tile(n, 1536, LANES)
    return pl.pallas_call(
        _ada_kernel,
        grid=(depth, n // tn),
        in_specs=[pl.BlockSpec((8, d), lambda l, j: (0, 0)),
                  pl.BlockSpec((None, d, tn), lambda l, j: (l, 0, j)),
                  pl.BlockSpec((None, 1, tn), lambda l, j: (l, 0, j))],
        out_specs=pl.BlockSpec((None, 8, tn), lambda l, j: (l, 0, j)),
        out_shape=jax.ShapeDtypeStruct((depth, 8, n), F32),
        compiler_params=_cparams("parallel", "parallel"),
        name="ada_mod",
    )(cs, ada_w, ada_b.reshape(depth, 1, n))


def _mod_row(row_base, tiles_per_batch, axis):
    if tiles_per_batch is None:
        return row_base
    return row_base + pl.program_id(axis) // tiles_per_batch


def _norm_mod_kernel(x_ref, w_ref, sh_ref, sc_ref, o_ref, *, row_base, tiles_per_batch):
    x = x_ref[...]
    y = x * lax.rsqrt(jnp.mean(x * x, axis=-1, keepdims=True) + EPS) * w_ref[...]
    row = _mod_row(row_base, tiles_per_batch, 0)
    sh = sh_ref[pl.ds(row, 1), :]
    sc = sc_ref[pl.ds(row, 1), :]
    o_ref[...] = (y * (1.0 + sc) + sh).astype(o_ref.dtype)


def _norm_kernel(x_ref, w_ref, o_ref):
    x = x_ref[...]
    y = x * lax.rsqrt(jnp.mean(x * x, axis=-1, keepdims=True) + EPS) * w_ref[...]
    o_ref[...] = y.astype(o_ref.dtype)


def norm_mod(x, w, mod, sh_chunk, sc_chunk, rows_per_batch, row_base):
    m, d = x.shape
    tm = _pick_tile(m if rows_per_batch is None else rows_per_batch, 512, 8)
    tpb = None if rows_per_batch is None else rows_per_batch // tm
    kern = functools.partial(_norm_mod_kernel, row_base=row_base, tiles_per_batch=tpb)
    return pl.pallas_call(
        kern, grid=(m // tm,),
        in_specs=[pl.BlockSpec((tm, d), lambda i: (i, 0)),
                  pl.BlockSpec((1, d), lambda i: (0, 0)),
                  pl.BlockSpec((8, d), lambda i: (0, sh_chunk)),
                  pl.BlockSpec((8, d), lambda i: (0, sc_chunk))],
        out_specs=pl.BlockSpec((tm, d), lambda i: (i, 0)),
        out_shape=jax.ShapeDtypeStruct((m, d), BF16),
        compiler_params=_cparams("parallel"),
        name="norm_mod",
    )(x, w.reshape(1, d), mod, mod)


def final_norm(x, w):
    m, d = x.shape
    tm = _pick_tile(m, 512, 8)
    return pl.pallas_call(
        _norm_kernel, grid=(m // tm,),
        in_specs=[pl.BlockSpec((tm, d), lambda i: (i, 0)), pl.BlockSpec((1, d), lambda i: (0, 0))],
        out_specs=pl.BlockSpec((tm, d), lambda i: (i, 0)),
        out_shape=jax.ShapeDtypeStruct((m, d), F32),
        compiler_params=_cparams("parallel"),
        name="final_norm",
    )(x, w.reshape(1, d))


def _mm_kernel(a_ref, w_ref, o_ref):
    o_ref[...] = jnp.dot(a_ref[...], w_ref[...], preferred_element_type=F32).astype(o_ref.dtype)


def _mm_heads_kernel(a_ref, w_ref, o_ref):
    acc = jnp.dot(a_ref[...], w_ref[...], preferred_element_type=F32)
    for h in range(o_ref.shape[0]):
        o_ref[h] = acc[:, h * HEAD_DIM:(h + 1) * HEAD_DIM].astype(o_ref.dtype)


def _mm_swiglu_kernel(a_ref, w1_ref, w3_ref, o_ref):
    a = a_ref[...]
    u = jnp.dot(a, w1_ref[...], preferred_element_type=F32)
    g = jnp.dot(a, w3_ref[...], preferred_element_type=F32)
    o_ref[...] = (_silu(u) * g).astype(o_ref.dtype)


def _mm_residual_kernel(a_ref, w_ref, x_ref, g_ref, o_ref, *, row_base, tiles_per_batch):
    acc = jnp.dot(a_ref[...], w_ref[...], preferred_element_type=F32)
    row = _mod_row(row_base, tiles_per_batch, 1)
    o_ref[...] = x_ref[...] + g_ref[pl.ds(row, 1), :] * acc


def matmul(a, w, out_dtype=BF16):
    m, k = a.shape
    n = w.shape[1]
    tm = _pick_tile(m, 512, 16)
    tn = _pick_tile(n, 2048, LANES)
    return pl.pallas_call(
        _mm_kernel, grid=(n // tn, m // tm),
        in_specs=[pl.BlockSpec((tm, k), lambda j, i: (i, 0)), pl.BlockSpec((k, tn), lambda j, i: (0, j))],
        out_specs=pl.BlockSpec((tm, tn), lambda j, i: (i, j)),
        out_shape=jax.ShapeDtypeStruct((m, n), out_dtype),
        compiler_params=_cparams("parallel", "parallel"),
        name="matmul",
    )(a, w)


def matmul_heads(a, w, batch):
    m, k = a.shape
    n = w.shape[1]
    lb = m // batch
    tm = _pick_tile(lb, 512, 16)
    tn = _pick_tile(n, 1536, LANES)
    tpb = lb // tm
    nh = tn // HEAD_DIM
    return pl.pallas_call(
        _mm_heads_kernel, grid=(n // tn, m // tm),
        in_specs=[pl.BlockSpec((tm, k), lambda j, i: (i, 0)), pl.BlockSpec((k, tn), lambda j, i: (0, j))],
        out_specs=pl.BlockSpec((None, nh, tm, HEAD_DIM), lambda j, i: (i // tpb, j, i % tpb, 0)),
        out_shape=jax.ShapeDtypeStruct((batch, n // HEAD_DIM, lb, HEAD_DIM), BF16),
        compiler_params=_cparams("parallel", "parallel"),
        name="matmul_heads",
    )(a, w)


def matmul_swiglu(a, w1, w3):
    m, k = a.shape
    n = w1.shape[1]
    tm = _pick_tile(m, 1024, 16)
    tn = _pick_tile(n, 704, LANES)
    return pl.pallas_call(
        _mm_swiglu_kernel, grid=(n // tn, m // tm),
        in_specs=[pl.BlockSpec((tm, k), lambda j, i: (i, 0)),
                  pl.BlockSpec((k, tn), lambda j, i: (0, j)),
                  pl.BlockSpec((k, tn), lambda j, i: (0, j))],
        out_specs=pl.BlockSpec((tm, tn), lambda j, i: (i, j)),
        out_shape=jax.ShapeDtypeStruct((m, n), BF16),
        compiler_params=_cparams("parallel", "parallel"),
        name="matmul_swiglu",
    )(a, w1, w3)


def matmul_residual(a, w, x, mod, gate_chunk, rows_per_batch, row_base):
    m, k = a.shape
    n = w.shape[1]
    small_k = k <= 2048
    tm = _pick_tile(m if rows_per_batch is None else rows_per_batch, 1024 if small_k else 512, 16)
    tn = _pick_tile(n, 1024 if small_k else 512, LANES)
    tpb = None if rows_per_batch is None else rows_per_batch // tm
    kern = functools.partial(_mm_residual_kernel, row_base=row_base, tiles_per_batch=tpb)
    gblk = gate_chunk * (n // tn)
    return pl.pallas_call(
        kern, grid=(n // tn, m // tm),
        in_specs=[pl.BlockSpec((tm, k), lambda j, i: (i, 0)),
                  pl.BlockSpec((k, tn), lambda j, i: (0, j)),
                  pl.BlockSpec((tm, tn), lambda j, i: (i, j)),
                  pl.BlockSpec((8, tn), lambda j, i: (0, gblk + j))],
        out_specs=pl.BlockSpec((tm, tn), lambda j, i: (i, j)),
        out_shape=jax.ShapeDtypeStruct((m, n), F32),
        compiler_params=_cparams("parallel", "parallel"),
        name="matmul_residual",
    )(a, w, x, mod)


def _merge_kernel(y0, y1, y2, y3, g0, g1, g2, g3, wb_ref, o_ref):
    acc = None
    for i, (y, g) in enumerate(((y0, g0), (y1, g1), (y2, g2), (y3, g3))):
        gate = 0.5 * jnp.tanh(0.5 * g[...].astype(F32)) + 0.5
        t = gate * jnp.dot(y[...], wb_ref[i], preferred_element_type=F32)
        acc = t if acc is None else acc + t
    o_ref[...] = acc.astype(o_ref.dtype)


def merge_branches(ys, gate_pre, wb):
    m, w = ys[0].shape
    d = wb.shape[2]
    tm = _pick_tile(m, 512, 16)
    tn = _pick_tile(d, 1024, LANES)
    nb = d // tn
    y_spec = pl.BlockSpec((tm, w), lambda j, i: (i, 0))
    g_specs = [pl.BlockSpec((tm, tn), functools.partial(lambda j, i, b: (i, b * nb + j), b=b))
               for b in range(N_BRANCH)]
    return pl.pallas_call(
        _merge_kernel, grid=(nb, m // tm),
        in_specs=[y_spec] * N_BRANCH + g_specs + [pl.BlockSpec((N_BRANCH, w, tn), lambda j, i: (0, 0, j))],
        out_specs=pl.BlockSpec((tm, tn), lambda j, i: (i, j)),
        out_shape=jax.ShapeDtypeStruct((m, d), BF16),
        compiler_params=_cparams("parallel", "parallel"),
        name="merge_branches",
    )(*ys, gate_pre, gate_pre, gate_pre, gate_pre, wb)


def _short_conv_kernel(x_ref, prev_ref, next_ref, w_ref, b_ref, *o_refs, n_tiles):
    i = pl.program_id(1)
    x = x_ref[...].astype(F32)
    t = x.shape[0]
    row = lax.broadcasted_iota(jnp.int32, x.shape, 0)
    halo = prev_ref.shape[0]
    before = jnp.where(i > 0, prev_ref[...].astype(F32)[halo - 1:halo, :], 0.0)
    after = jnp.where(i < n_tiles - 1, next_ref[...].astype(F32)[0:1, :], 0.0)
    xm1 = jnp.where(row == 0, before, pltpu.roll(x, 1, 0))
    xp1 = jnp.where(row == t - 1, after, pltpu.roll(x, t - 1, 0))
    w = w_ref[...]
    y = w[0:1, :] * xm1 + w[1:2, :] * x + w[2:3, :] * xp1 + b_ref[...]
    cw = o_refs[0].shape[-1]
    for n, o_ref in enumerate(o_refs):
        o_ref[...] = y[:, n * cw:(n + 1) * cw].astype(o_ref.dtype)


def short_conv(p, conv_w, conv_b, n_out):
    b, l, c = p.shape
    cw = c // n_out
    halo = 16
    t = _pick_tile(l, 512, halo)
    n_tiles = l // t
    hb = t // halo
    kern = functools.partial(_short_conv_kernel, n_tiles=n_tiles)
    return pl.pallas_call(
        kern, grid=(b, n_tiles),
        in_specs=[pl.BlockSpec((None, t, c), lambda bi, i: (bi, i, 0)),
                  pl.BlockSpec((None, halo, c), lambda bi, i: (bi, jnp.maximum(i * hb - 1, 0), 0)),
                  pl.BlockSpec((None, halo, c), lambda bi, i: (bi, jnp.minimum((i + 1) * hb, l // halo - 1), 0)),
                  pl.BlockSpec((8, c), lambda bi, i: (0, 0)),
                  pl.BlockSpec((1, c), lambda bi, i: (0, 0))],
        out_specs=[pl.BlockSpec((None, t, cw), lambda bi, i: (bi, i, 0))] * n_out,
        out_shape=[jax.ShapeDtypeStruct((b, l, cw), BF16)] * n_out,
        compiler_params=_cparams("parallel", "parallel"),
        name="short_conv",
    )(p, p, p, jnp.pad(conv_w, ((0, 8 - conv_w.shape[0]), (0, 0))), conv_b.reshape(1, c))


def _hy_filter_kernel(fv_ref, c2_ref, s2_ref, w1_ref, b1_ref, sf1_ref, w2_ref, b2_ref, sf2_ref, w3f_ref, w3b_ref,
                      dl_ref, f_ref, ss_ref, cb_ref, sb_ref, *, seq_len):
    i = pl.program_id(0)
    tl = f_ref.shape[2]
    hi = lax.Precision.HIGHEST
    step = 2.0 * math.pi / float(seq_len)
    r = lax.broadcasted_iota(jnp.int32, (tl, 1), 0)
    lane = lax.broadcasted_iota(jnp.int32, (1, LANES), 1)

    @pl.when(i == 0)
    def _():
        ang_r = (step * r.astype(F32)) * fv_ref[...]
        cb_ref[...] = jnp.cos(ang_r)
        sb_ref[...] = jnp.sin(ang_r)
        ss_ref[...] = jnp.zeros(ss_ref.shape, F32)

    base = (step * (i * tl).astype(F32)) * fv_ref[...]
    ca, sa = jnp.cos(base), jnp.sin(base)
    cb, sb = cb_ref[...], sb_ref[...]
    cos_j, sin_j = ca * cb - sa * sb, sa * cb + ca * sb
    cos_m, sin_m = c2_ref[...] * cos_j + s2_ref[...] * sin_j, s2_ref[...] * cos_j - c2_ref[...] * sin_j
    j = (i * tl + r).astype(F32)

    def mlp(t, cos_t, sin_t):
        t_norm = t / float(max(seq_len - 1, 1))
        feat = jnp.where(lane == 0, t_norm,
                         jnp.where(lane <= HY_BANDS, cos_t, jnp.where(lane <= 2 * HY_BANDS, -sin_t, 0.0)))
        z = jnp.sin(sf1_ref[...] * (jnp.dot(feat, w1_ref[...], precision=hi, preferred_element_type=F32)
                                    + b1_ref[...]))
        z = jnp.sin(sf2_ref[...] * (jnp.dot(z, w2_ref[...], precision=hi, preferred_element_type=F32)
                                    + b2_ref[...]))
        return z.astype(BF16), t_norm

    zf, tnf = mlp(j, cos_j, sin_j)
    hf = jnp.dot(zf, w3f_ref[...], preferred_element_type=F32) * jnp.exp(-tnf * dl_ref[...])
    zb, tnb = mlp(float(seq_len) - j, cos_m, sin_m)
    hb = jnp.dot(zb, w3b_ref[...], preferred_element_type=F32) * jnp.exp(-tnb * dl_ref[...])
    hb = jnp.where(j > 0.0, hb, 0.0)
    cw = f_ref.shape[3]
    for o in range(f_ref.shape[0]):
        f_ref[o, 0] = hf[:, o * cw:(o + 1) * cw].astype(f_ref.dtype)
        f_ref[o, 1] = hb[:, o * cw:(o + 1) * cw].astype(f_ref.dtype)
    ss_ref[...] += (jnp.sum(hf * hf, axis=0, keepdims=True) + jnp.sum(hb * hb, axis=0, keepdims=True))


def hy_filters(seq_len, w1, b1, w2, b2, w3, sin_freq, hy_w):
    emb, ffn = w1.shape
    cw = 2 * hy_w
    f = np.linspace(1e-4, HY_BANDS - 1, HY_BANDS)
    fv = np.zeros((1, LANES), np.float32)
    fv[0, 1:1 + HY_BANDS] = f
    fv[0, 1 + HY_BANDS:1 + 2 * HY_BANDS] = f
    c2 = np.cos(2.0 * np.pi * fv.astype(np.float64)).astype(np.float32)
    s2 = np.sin(2.0 * np.pi * fv.astype(np.float64)).astype(np.float32)
    w1p = jnp.pad(w1, ((0, LANES - emb), (0, 0)))
    w3r = w3.reshape(ffn, 2, 2, hy_w)
    w3f = w3r[:, :, 0, :].reshape(ffn, cw).astype(BF16)
    w3b = w3r[:, :, 1, :].reshape(ffn, cw).astype(BF16)
    deltas = np.abs(np.linspace(math.log(HY_DECAY_TARGET) / HY_SLOW_PCT,
                                math.log(HY_DECAY_TARGET) / HY_FAST_PCT, hy_w)).astype(np.float32)
    dl = jnp.asarray(np.tile(deltas, 2).reshape(1, cw))
    tl = _pick_tile(seq_len, 512, 8)
    const = lambda i: (0, 0)
    kern = functools.partial(_hy_filter_kernel, seq_len=seq_len)
    return pl.pallas_call(
        kern, grid=(seq_len // tl,),
        in_specs=[pl.BlockSpec((1, LANES), const), pl.BlockSpec((1, LANES), const), pl.BlockSpec((1, LANES), const),
                  pl.BlockSpec((LANES, ffn), const),
                  pl.BlockSpec((1, ffn), const), pl.BlockSpec((1, ffn), const),
                  pl.BlockSpec((ffn, ffn), const), pl.BlockSpec((1, ffn), const), pl.BlockSpec((1, ffn), const),
                  pl.BlockSpec((ffn, cw), const), pl.BlockSpec((ffn, cw), const), pl.BlockSpec((1, cw), const)],
        out_specs=[pl.BlockSpec((2, 2, tl, hy_w), lambda i: (0, 0, i, 0)), pl.BlockSpec((1, cw), const)],
        out_shape=[jax.ShapeDtypeStruct((2, 2, seq_len, hy_w), BF16), jax.ShapeDtypeStruct((1, cw), F32)],
        scratch_shapes=[pltpu.VMEM((tl, LANES), F32), pltpu.VMEM((tl, LANES), F32)],
        compiler_params=_cparams("arbitrary"),
        name="hy_filters",
    )(jnp.asarray(fv), jnp.asarray(c2), jnp.asarray(s2), w1p, b1.reshape(1, ffn), sin_freq[0].reshape(1, ffn), w2,
      b2.reshape(1, ffn), sin_freq[1].reshape(1, ffn), w3f, w3b, dl)


def _phase_mats(phase_num, denom, conj):
    ang = np.pi * (phase_num % (2 * denom)).astype(np.float64) / denom
    cr, ci = np.cos(ang), (np.sin(ang) if conj else -np.sin(ang))
    return cr, ci


def _stage1_mats(n1):
    k = np.arange(n1)[:, None]
    n = np.arange(n1)[None, :]
    w1d, w1f, wfin = [], [], []
    for v in (0, 1):
        cr, ci = _phase_mats(2 * k * n + v * n, n1, conj=False)
        w1d.append(np.block([[cr, -ci], [ci, cr]]))
        sgn = 1.0 if v == 0 else -1.0
        w1f.append(np.block([[cr, sgn * cr], [ci, sgn * ci]]))
        cri, cii = _phase_mats(2 * n.T * k.T + v * n.T, n1, conj=True)
        wfin.append(np.block([[cri, -cii], [cii, cri]]))
    return (np.concatenate(w1d, 0).astype(np.float32), np.concatenate(w1f, 0).astype(np.float32),
            np.concatenate(wfin, 1).astype(np.float32))


def _stage2_mats(n1, n2):
    l = n1 * n2
    k2 = np.arange(n2)[:, None]
    nn = np.arange(n2)[None, :]
    base = (2 * n1 * k2 * nn) % (2 * l)
    br, bi = np.cos(np.pi * base / l), -np.sin(np.pi * base / l)
    k1 = np.arange(n1)[None, :, None]
    v = np.arange(2)[:, None, None]
    tw = (2 * np.arange(n2)[None, None, :] * k1 + v * np.arange(n2)[None, None, :]) % (2 * l)
    tr, ti = np.cos(np.pi * tw / l), -np.sin(np.pi * tw / l)
    br, bi, tr, ti = (jnp.asarray(a, F32) for a in (br, bi, tr, ti))
    cr = br[None, None] * tr[:, :, None, :] - bi[None, None] * ti[:, :, None, :]
    ci = br[None, None] * ti[:, :, None, :] + bi[None, None] * tr[:, :, None, :]
    fwd = jnp.concatenate([jnp.concatenate([cr, -ci], -1), jnp.concatenate([ci, cr], -1)], -2)
    crt, cit = jnp.swapaxes(cr, -1, -2), -jnp.swapaxes(ci, -1, -2)
    inv = jnp.concatenate([jnp.concatenate([crt, -cit], -1), jnp.concatenate([cit, crt], -1)], -2)
    return fwd.astype(BF16), inv.astype(BF16)


def _fft_stage1_kernel(z_ref, w_ref, o_ref):
    n1 = z_ref.shape[1]
    nj, cw = o_ref.shape[3], o_ref.shape[4]
    x = z_ref[...].reshape(2 * n1, nj * cw)
    for plane in range(4):
        y = jnp.dot(w_ref[plane * n1:(plane + 1) * n1, :], x, preferred_element_type=F32).astype(o_ref.dtype)
        tiles = jnp.stack([y[:, j * cw:(j + 1) * cw] for j in range(nj)], axis=0)
        o_ref[plane // 2, plane % 2] = jnp.swapaxes(tiles, 0, 1)


FFT_COLS = 2048
FFT_S1_N2 = 16


def fft_stage1(z, w1, n1, n2):
    grp, _, _, w = z.shape
    nj = _pick_tile(n2, FFT_S1_N2, 16)
    return pl.pallas_call(
        _fft_stage1_kernel, grid=(grp, n2 // nj),
        in_specs=[pl.BlockSpec((None, 2, n1, nj * w), lambda gi, j: (gi, 0, 0, j)),
                  pl.BlockSpec((4 * n1, 2 * n1), lambda gi, j: (0, 0))],
        out_specs=pl.BlockSpec((None, 2, 2, n1, nj, w), lambda gi, j: (gi, 0, 0, 0, j, 0)),
        out_shape=jax.ShapeDtypeStruct((grp, 2, 2, n1, n2, w), BF16),
        compiler_params=_cparams("parallel", "parallel"),
        name="fft_stage1",
    )(z.reshape(grp, 2, n1, n2 * w), w1)


def _fft_mid_kernel(a_ref, f_ref, ss_ref, wf_ref, wi_ref, o_ref, *, scale):
    kb, n2, cw = a_ref.shape[1], a_ref.shape[2], a_ref.shape[3]
    nrm = lax.rsqrt(ss_ref[...] + EPS) * scale
    us = []
    for kk in range(kb):
        wf = wf_ref[kk]
        t = jnp.dot(wf, a_ref[:, kk].reshape(2 * n2, cw), preferred_element_type=F32)
        g = jnp.dot(wf, f_ref[:, kk].reshape(2 * n2, cw), preferred_element_type=F32) * nrm
        tr, ti, gr, gi = t[:n2], t[n2:], g[:n2], g[n2:]
        p = jnp.concatenate([tr * gr - ti * gi, tr * gi + ti * gr], axis=0).astype(BF16)
        us.append(jnp.dot(wi_ref[kk], p, preferred_element_type=F32).astype(o_ref.dtype))
    for r in range(2):
        tiles = jnp.swapaxes(jnp.stack([u[r * n2:(r + 1) * n2] for u in us], axis=0), 0, 1)
        for j in range(n2):
            o_ref[r, :, j * cw:(j + 1) * cw] = tiles[j]


FFT_MID_K1 = 16


def fft_mid(a, af, ss, wf, wi, order, n1, n2):
    cw = a.shape[-1]
    kb = _pick_tile(n1, FFT_MID_K1, 16)
    kern = functools.partial(_fft_mid_kernel, scale=1.0 / (2.0 * n1 * n2))
    mat_spec = pl.BlockSpec((None, kb, 2 * n2, 2 * n2), lambda v, k: (v, k, 0, 0))
    return pl.pallas_call(
        kern, grid=(2, n1 // kb),
        in_specs=[pl.BlockSpec((None, 2, kb, n2, cw), lambda v, k: (v, 0, k, 0, 0)),
                  pl.BlockSpec((None, None, 2, kb, n2, cw), lambda v, k: (order, v, 0, k, 0, 0)),
                  pl.BlockSpec((1, cw), lambda v, k: (0, order)),
                  mat_spec, mat_spec],
        out_specs=pl.BlockSpec((None, 2, kb, n2 * cw), lambda v, k: (v, 0, k, 0)),
        out_shape=jax.ShapeDtypeStruct((2, 2, n1, n2 * cw), BF16),
        compiler_params=_cparams("parallel", "parallel"),
        name="fft_mid",
    )(a, af, ss, wf, wi)


def _fft_final_kernel(u_ref, w_ref, gate_ref, z_ref, bias_ref, o_ref):
    n1 = gate_ref.shape[1]
    cw = gate_ref.shape[2]
    u = u_ref[...].reshape(4 * n1, cw)
    y = jnp.dot(w_ref[...], u, preferred_element_type=F32).reshape(2, n1, cw)
    z = z_ref[...].astype(F32)
    o_ref[...] = (gate_ref[...].astype(F32) * (y + bias_ref[...] * z)).astype(o_ref.dtype)


def fft_final(u, wfin, gate, z, bias, n1, n2):
    cw = z.shape[2]
    cols = n2 * cw
    tc = _pick_tile(cols, FFT_COLS, cw)
    out = pl.pallas_call(
        _fft_final_kernel, grid=(cols // tc,),
        in_specs=[pl.BlockSpec((2, 2, n1, tc), lambda j: (0, 0, 0, j)),
                  pl.BlockSpec((2 * n1, 4 * n1), lambda j: (0, 0)),
                  pl.BlockSpec((2, n1, tc), lambda j: (0, 0, j)),
                  pl.BlockSpec((2, n1, tc), lambda j: (0, 0, j)),
                  pl.BlockSpec((1, tc), lambda j: (0, 0))],
        out_specs=pl.BlockSpec((2, n1, tc), lambda j: (0, 0, j)),
        out_shape=jax.ShapeDtypeStruct((2, n1, cols), BF16),
        compiler_params=_cparams("parallel"),
        name="fft_final",
    )(u, wfin, gate.reshape(2, n1, cols), z.reshape(2, n1, cols),
      jnp.tile(bias.reshape(1, cw), (1, tc // cw)))
    return out.reshape(2, n1 * n2, cw)


def _hy_dense_kernel(z_ref, gate_ref, f_ref, ss_ref, bias_ref, wf_ref, wi_ref, o_ref):
    lc = z_ref.shape[1]
    cw = z_ref.shape[2]
    nrm = lax.rsqrt(ss_ref[...] + EPS) * (1.0 / (2.0 * lc))
    f1, f2 = f_ref[0].astype(F32), f_ref[1].astype(F32)
    filt = (((f1 + f2) * nrm).astype(BF16), ((f1 - f2) * nrm).astype(BF16))
    x = z_ref[...].reshape(2 * lc, cw)
    acc = None
    for v in (0, 1):
        wf = wf_ref[v]
        t = jnp.dot(wf, x, preferred_element_type=F32)
        g = jnp.dot(wf[:, :lc], filt[v], preferred_element_type=F32)
        tr, ti, gr, gi = t[:lc], t[lc:], g[:lc], g[lc:]
        p = jnp.concatenate([tr * gr - ti * gi, tr * gi + ti * gr], axis=0).astype(BF16)
        u = jnp.dot(wi_ref[v], p, preferred_element_type=F32)
        acc = u if acc is None else acc + u
    y = acc.reshape(2, lc, cw)
    o_ref[...] = (gate_ref[...].astype(F32) * (y + bias_ref[...] * z_ref[...].astype(F32))).astype(o_ref.dtype)


def hy_dense_conv(z_arr, gate_arr, f, ss, order, bias, wf, wi):
    lc, cw = z_arr.shape[1], z_arr.shape[2]
    return pl.pallas_call(
        _hy_dense_kernel, grid=(1,),
        in_specs=[pl.BlockSpec((2, lc, cw), lambda i: (0, 0, 0)),
                  pl.BlockSpec((2, lc, cw), lambda i: (0, 0, 0)),
                  pl.BlockSpec((None, 2, lc, cw), lambda i: (order, 0, 0, 0)),
                  pl.BlockSpec((1, cw), lambda i: (0, order)),
                  pl.BlockSpec((1, cw), lambda i: (0, 0)),
                  pl.BlockSpec((2, None, 2 * lc, 2 * lc), lambda i: (0, 0, 0, 0)),
                  pl.BlockSpec((2, None, 2 * lc, 2 * lc), lambda i: (0, 0, 0, 0))],
        out_specs=pl.BlockSpec((2, lc, cw), lambda i: (0, 0, 0)),
        out_shape=jax.ShapeDtypeStruct((2, lc, cw), BF16),
        compiler_params=_cparams("arbitrary"),
        name="hy_dense_conv",
    )(z_arr, gate_arr, f, ss, bias.reshape(1, cw), wf, wi)


def hyena_mixer(p, conv_w, conv_b, w1, b1, w2, b2, w3, sin_freq, bias, mats):
    b, l, c3 = p.shape
    assert b == 2, "batch elements are packed as the re/im parts of one complex signal"
    hy_w = c3 // 3
    v, x1, x2 = short_conv(p, conv_w, conv_b, 3)
    f, ss = hy_filters(l, w1, b1, w2, b2, w3, sin_freq, hy_w)
    if l <= 2 * FFT_N2:
        wf, wi = mats["dense"]
        z2 = hy_dense_conv(v, x1, f, ss, 0, bias[0], wf, wi)
        return hy_dense_conv(z2, x2, f, ss, 1, bias[1], wf, wi)
    n2 = FFT_N2
    n1 = l // n2
    w1d, w1f, wfin, wf, wi = mats["fft"]
    af = fft_stage1(f, w1f, n1, n2)
    a = fft_stage1(v[None], w1d, n1, n2)[0]
    z2 = fft_final(fft_mid(a, af, ss, wf, wi, 0, n1, n2), wfin, x1, v, bias[0], n1, n2)
    a = fft_stage1(z2[None], w1d, n1, n2)[0]
    return fft_final(fft_mid(a, af, ss, wf, wi, 1, n1, n2), wfin, x2, z2, bias[1], n1, n2)


def hyena_mats(l_lat, l_ctx):
    n1 = l_lat // FFT_N2
    w1d, w1f, wfin = _stage1_mats(n1)
    wf, wi = _stage2_mats(n1, FFT_N2)
    mats = {"fft": (jnp.asarray(w1d, BF16), jnp.asarray(w1f, BF16), jnp.asarray(wfin, BF16), wf, wi)}
    mats["dense"] = _stage2_mats(1, l_ctx)
    return mats


def _ctx_attn_kernel(q_ref, k_ref, v_ref, o_ref):
    s = lax.dot_general(q_ref[...], k_ref[...], NT_DIMS, preferred_element_type=F32)
    m = jnp.max(s, axis=-1, keepdims=True)
    p = jnp.exp(s - m)
    l = jnp.sum(p, axis=-1, keepdims=True)
    o = jnp.dot(p.astype(BF16), v_ref[...], preferred_element_type=F32) / l
    o_ref[...] = o.astype(o_ref.dtype)


def ctx_attention(qa, q_off, ka, k_off, va, v_off, n_q_heads, group):
    b, _, lc, d = qa.shape
    return pl.pallas_call(
        _ctx_attn_kernel, grid=(b, n_q_heads),
        in_specs=[pl.BlockSpec((None, None, lc, d), lambda bi, h: (bi, q_off + h, 0, 0)),
                  pl.BlockSpec((None, None, lc, d), lambda bi, h: (bi, k_off + h // group, 0, 0)),
                  pl.BlockSpec((None, None, lc, d), lambda bi, h: (bi, v_off + h // group, 0, 0))],
        out_specs=pl.BlockSpec((None, lc, d), lambda bi, h: (bi, 0, h)),
        out_shape=jax.ShapeDtypeStruct((b, lc, n_q_heads * d), BF16),
        compiler_params=_cparams("parallel", "parallel"),
        name="ctx_attention",
    )(qa, ka, va)


NA_TILE_ROWS = 4
NA_KEY_ROWS = 12
NA_HEADS_PER_STEP = 2


def _na_geometry(n_rows):
    wr = min(NA_WIN_R, n_rows)
    nt = n_rows // NA_TILE_ROWS
    sigs = []
    for t in range(nt):
        rt = t * NA_TILE_ROWS
        w0 = int(np.clip(rt - wr // 2, 0, n_rows - NA_KEY_ROWS))
        r = rt + np.arange(NA_TILE_ROWS)
        r0 = np.clip(r - wr // 2, 0, n_rows - wr)
        sigs.append((w0 - rt, tuple((r0 - w0).tolist())))
    classes = (sigs[0], sigs[1], sigs[-1])
    for t, s in enumerate(sigs):
        assert s == classes[0 if t == 0 else (2 if t == nt - 1 else 1)], "tile does not match its bias class"
    return wr, nt, classes


def _na_bias_tables(rpb, n_rows):
    wr, _, classes = _na_geometry(n_rows)
    cols = np.arange(GRID_W)
    c0 = np.clip(cols - NA_WIN_C // 2, 0, GRID_W - NA_WIN_C)
    col_ok = (cols[None, :] >= c0[:, None]) & (cols[None, :] < c0[:, None] + NA_WIN_C)
    col_idx = np.clip(cols[None, :] - cols[:, None] + NA_WIN_C - 1, 0, 2 * NA_WIN_C - 2)
    col_sel = (col_idx[None] == np.arange(2 * NA_WIN_C - 1)[:, None, None]).astype(np.float32)
    row_sel, oks = [], []
    for off, rel_r0 in classes:
        qi = np.arange(NA_TILE_ROWS)[:, None]
        kw = np.arange(NA_KEY_ROWS)[None, :]
        r0 = np.asarray(rel_r0)[:, None]
        row_ok = (kw >= r0) & (kw < r0 + wr)
        row_idx = np.clip(kw + off - qi + NA_WIN_R - 1, 0, 2 * NA_WIN_R - 2)
        row_sel.append((row_idx[None] == np.arange(2 * NA_WIN_R - 1)[:, None, None]).astype(np.float32))
        oks.append(row_ok[:, None, :, None] & col_ok[None, :, None, :])
    vals = jnp.einsum("hab,saqk,bcd->hsqckd", rpb.astype(F32), jnp.asarray(np.stack(row_sel)),
                      jnp.asarray(col_sel), precision=lax.Precision.HIGHEST)
    tab = jnp.where(jnp.asarray(np.stack(oks))[None], vals, NEG_BIG)
    return tab.reshape(rpb.shape[0], 3, NA_TILE_ROWS * GRID_W, NA_KEY_ROWS * GRID_W)


def _na_kernel(q_ref, k_ref, v_ref, kc_ref, vc_ref, bias_ref, o_ref, *, n_rows, win_half):
    t = pl.program_id(2)
    w0 = jnp.clip(t * NA_TILE_ROWS - win_half, 0, n_rows - NA_KEY_ROWS)
    start = pl.multiple_of(w0 * GRID_W, GRID_W)
    outs = []
    for hh in range(q_ref.shape[0]):
        kwin = k_ref[hh, pl.ds(start, NA_KEY_ROWS * GRID_W), :]
        vwin = v_ref[hh, pl.ds(start, NA_KEY_ROWS * GRID_W), :]
        q = q_ref[hh]
        s_lat = lax.dot_general(q, kwin, NT_DIMS, preferred_element_type=F32) + bias_ref[hh]
        s_ctx = lax.dot_general(q, kc_ref[hh], NT_DIMS, preferred_element_type=F32)
        m = jnp.maximum(jnp.max(s_lat, axis=-1, keepdims=True), jnp.max(s_ctx, axis=-1, keepdims=True))
        p_lat = jnp.exp(s_lat - m)
        p_ctx = jnp.exp(s_ctx - m)
        l = jnp.sum(p_lat, axis=-1, keepdims=True) + jnp.sum(p_ctx, axis=-1, keepdims=True)
        o = (jnp.dot(p_lat.astype(BF16), vwin, preferred_element_type=F32)
             + jnp.dot(p_ctx.astype(BF16), vc_ref[hh], preferred_element_type=F32)) / l
        outs.append(o.astype(o_ref.dtype))
    o_ref[...] = jnp.concatenate(outs, axis=1)


def na_attention(pl_heads, pc_heads, rpb):
    b, h3, l, d = pl_heads.shape
    lc = pc_heads.shape[2]
    h = h3 // 3
    n_rows = l // GRID_W
    wr, nt, _ = _na_geometry(n_rows)
    bias = _na_bias_tables(rpb, n_rows)
    tq = NA_TILE_ROWS * GRID_W
    kw = NA_KEY_ROWS * GRID_W
    kern = functools.partial(_na_kernel, n_rows=n_rows, win_half=wr // 2)

    def cls(t):
        return jnp.where(t == 0, 0, jnp.where(t == nt - 1, 2, 1))

    hs = NA_HEADS_PER_STEP
    assert h % hs == 0
    hb = h // hs
    return pl.pallas_call(
        kern, grid=(b, hb, nt),
        in_specs=[pl.BlockSpec((None, hs, tq, d), lambda bi, hi, t: (bi, hi, t, 0)),
                  pl.BlockSpec((None, hs, l, d), lambda bi, hi, t: (bi, hb + hi, 0, 0)),
                  pl.BlockSpec((None, hs, l, d), lambda bi, hi, t: (bi, 2 * hb + hi, 0, 0)),
                  pl.BlockSpec((None, hs, lc, d), lambda bi, hi, t: (bi, hb + hi, 0, 0)),
                  pl.BlockSpec((None, hs, lc, d), lambda bi, hi, t: (bi, 2 * hb + hi, 0, 0)),
                  pl.BlockSpec((hs, None, tq, kw), lambda bi, hi, t: (hi, cls(t), 0, 0))],
        out_specs=pl.BlockSpec((None, tq, hs * d), lambda bi, hi, t: (bi, t, hi)),
        out_shape=jax.ShapeDtypeStruct((b, l, h * d), BF16),
        compiler_params=_cparams("parallel", "parallel", "parallel"),
        name="na_attention",
    )(pl_heads, pl_heads, pl_heads, pc_heads, pc_heads, bias)


def _ret_kernel(*refs, reverse, rope, finalize, n_heads):
    it = iter(refs)
    lg_ref, p_ref = next(it), next(it)
    cos_ref, sin_ref = (next(it), next(it)) if rope else (None, None)
    s0_ref = next(it)
    oprev_ref, gnw_ref = (next(it), next(it)) if finalize else (None, None)
    o_ref, sfin_ref, state_ref = next(it), next(it), next(it)

    i = pl.program_id(1)
    n_steps = pl.num_programs(1)

    @pl.when(i == 0)
    def _():
        state_ref[...] = s0_ref[...]

    c = RET_CHUNK
    ts = p_ref.shape[0]
    nc = ts // c
    dq = n_heads * RET_DK
    dv = n_heads * RET_DV
    x = p_ref[...]
    q = x[:, :dq].astype(F32)
    k = x[:, dq:2 * dq].astype(F32)
    v = x[:, 2 * dq:2 * dq + dv]
    if rope:
        q = q * cos_ref[...] + _swap32(q) * sin_ref[...]
        k = k * cos_ref[...] + _swap32(k) * sin_ref[...]
    jr = lax.broadcasted_iota(jnp.int32, (c, c), 0)
    jc = lax.broadcasted_iota(jnp.int32, (c, c), 1)
    rel = ((jc - jr) if reverse else (jr - jc)).astype(F32)
    jcol = lax.broadcasted_iota(jnp.int32, (c, 1), 0).astype(F32)
    order = range(nc - 1, -1, -1) if reverse else range(nc)
    blocks = [[None] * n_heads for _ in range(nc)]
    for h in range(n_heads):
        g = lg_ref[h]
        intra = jnp.where(rel >= 0.0, jnp.exp(g * jnp.maximum(rel, 0.0)), 0.0)
        if reverse:
            cross_f = jnp.exp(g * (float(c) - jcol))
            k_dec = jnp.exp(g * jcol)
        else:
            cross_f = jnp.exp(g * (jcol + 1.0))
            k_dec = jnp.exp(g * (float(c - 1) - jcol))
        chunk_decay = jnp.exp(g * float(c))
        state = state_ref[h]
        for ci in order:
            rows = slice(ci * c, (ci + 1) * c)
            qh = q[rows, h * RET_DK:(h + 1) * RET_DK].astype(BF16)
            kf = k[rows, h * RET_DK:(h + 1) * RET_DK]
            vh = v[rows, h * RET_DV:(h + 1) * RET_DV]
            scores = lax.dot_general(qh, kf.astype(BF16), NT_DIMS, preferred_element_type=F32) * intra
            inner = jnp.dot(scores.astype(BF16), vh, preferred_element_type=F32)
            cross = jnp.dot(qh, state.astype(BF16), preferred_element_type=F32) * cross_f
            blocks[ci][h] = inner + cross
            kv = lax.dot_general((kf * k_dec).astype(BF16), vh, TN_DIMS, preferred_element_type=F32)
            state = chunk_decay * state + kv
        state_ref[h] = state
    o = jnp.concatenate([jnp.concatenate(blocks[ci], axis=1) for ci in range(nc)], axis=0)
    if finalize:
        o = o + oprev_ref[...]
        gate = x[:, 2 * dq + dv:2 * dq + 2 * dv].astype(F32)
        normed = []
        for h in range(n_heads):
            oh = o[:, h * RET_DV:(h + 1) * RET_DV]
            normed.append(oh * lax.rsqrt(jnp.mean(oh * oh, axis=-1, keepdims=True) + EPS))
        o = jnp.concatenate(normed, axis=1) * gnw_ref[...] * _silu(gate)
    o_ref[...] = o.astype(o_ref.dtype)

    @pl.when(i == n_steps - 1)
    def _():
        sfin_ref[...] = state_ref[...]


def retention_pass(p, log_g, s0, reverse, rope_tables=None, o_prev=None, gn_w=None):
    b, l, w = p.shape
    n_heads = w // (2 * RET_DK + 2 * RET_DV)
    dq, dv = n_heads * RET_DK, n_heads * RET_DV
    ts = _pick_tile(l, 512, RET_CHUNK)
    n_steps = l // ts
    finalize = o_prev is not None
    rope = rope_tables is not None

    def tile(i):
        return (n_steps - 1 - i) if reverse else i

    in_specs = [pl.BlockSpec(memory_space=pltpu.SMEM),
                pl.BlockSpec((None, ts, w), lambda bi, i: (bi, tile(i), 0))]
    args = [log_g, p]
    if rope:
        in_specs += [pl.BlockSpec((ts, dq), lambda bi, i: (tile(i), 0))] * 2
        args += list(rope_tables)
    in_specs.append(pl.BlockSpec((None, n_heads, RET_DK, RET_DV), lambda bi, i: (bi, 0, 0, 0)))
    args.append(s0)
    if finalize:
        in_specs += [pl.BlockSpec((None, ts, dv), lambda bi, i: (bi, tile(i), 0)),
                     pl.BlockSpec((1, dv), lambda bi, i: (0, 0))]
        args += [o_prev, gn_w.reshape(1, dv)]
    kern = functools.partial(_ret_kernel, reverse=reverse, rope=rope, finalize=finalize, n_heads=n_heads)
    return pl.pallas_call(
        kern, grid=(b, n_steps),
        in_specs=in_specs,
        out_specs=[pl.BlockSpec((None, ts, dv), lambda bi, i: (bi, tile(i), 0)),
                   pl.BlockSpec((None, n_heads, RET_DK, RET_DV), lambda bi, i: (bi, 0, 0, 0))],
        out_shape=[jax.ShapeDtypeStruct((b, l, dv), BF16 if finalize else F32),
                   jax.ShapeDtypeStruct((b, n_heads, RET_DK, RET_DV), F32)],
        scratch_shapes=[pltpu.VMEM((n_heads, RET_DK, RET_DV), F32)],
        compiler_params=_cparams("parallel", "arbitrary"),
        name="retention_pass",
    )(*args)


def _ret_rope_tables(l, n_heads):
    half = RET_DK // 2
    inv = RET_ROPE_BASE ** (-jnp.linspace(0.0, 1.0, half, dtype=F32))
    ang = jnp.arange(l, dtype=F32)[:, None] * inv
    cos, sin = jnp.cos(ang), jnp.sin(ang)
    return (jnp.tile(jnp.concatenate([cos, cos], -1), (1, n_heads)),
            jnp.tile(jnp.concatenate([-sin, sin], -1), (1, n_heads)))


def retention_mixer(p_l, p_c, log_decay, gn_w, tables, with_ctx_out):
    b = p_l.shape[0]
    n_heads = p_l.shape[2] // (2 * RET_DK + 2 * RET_DV)
    log_g = -jnp.abs(log_decay.astype(F32))
    s0 = jnp.zeros((b, n_heads, RET_DK, RET_DV), F32)
    o_cf, s_fwd = retention_pass(p_c, log_g[0], s0, False)
    y_c, s_bwd = retention_pass(p_c, log_g[1], s0, True, o_prev=o_cf, gn_w=gn_w)
    o_lf, _ = retention_pass(p_l, log_g[0], s_fwd, False, rope_tables=tables)
    y_l, _ = retention_pass(p_l, log_g[1], s_bwd, True, rope_tables=tables, o_prev=o_lf, gn_w=gn_w)
    return y_l, (y_c if with_ctx_out else None)


def _gqa_prep_kernel(*refs, rope, n_q, n_kv, q_scale):
    if rope:
        p_ref, cos_ref, sin_ref, qw_ref, kw_ref, q_out, k_out, v_out = refs
    else:
        p_ref, qw_ref, kw_ref, q_out, k_out, v_out = refs
    x = p_ref[...]
    d = HEAD_DIM

    def norm_rope(xh, w):
        xh = xh.astype(F32)
        y = xh * lax.rsqrt(jnp.mean(xh * xh, axis=-1, keepdims=True) + EPS) * w
        if rope:
            y = y * cos_ref[...] + _swap32(y) * sin_ref[...]
        return y

    for h in range(n_q):
        q_out[h] = (norm_rope(x[:, h * d:(h + 1) * d], qw_ref[...]) * q_scale).astype(q_out.dtype)
    for h in range(n_kv):
        k_out[h] = norm_rope(x[:, (n_q + h) * d:(n_q + h + 1) * d], kw_ref[...]).astype(k_out.dtype)
        lane = lax.broadcasted_iota(jnp.int32, (x.shape[0], d), 1)
        ones_col = jnp.where(lane == 0, 1.0, 0.0).astype(v_out.dtype)
        v_out[h] = jnp.concatenate([x[:, (n_q + n_kv + h) * d:(n_q + n_kv + h + 1) * d], ones_col], axis=1)


def gqa_prep(p, qn_w, kn_w, n_q, n_kv, q_scale, rope_tables=None):
    b, l, w = p.shape
    d = HEAD_DIM
    t = _pick_tile(l, 512, 16)
    rope = rope_tables is not None
    in_specs = [pl.BlockSpec((None, t, w), lambda bi, i: (bi, i, 0))]
    args = [p]
    if rope:
        in_specs += [pl.BlockSpec((t, d), lambda bi, i: (i, 0))] * 2
        args += list(rope_tables)
    in_specs += [pl.BlockSpec((1, d), lambda bi, i: (0, 0))] * 2
    args += [qn_w.reshape(1, d), kn_w.reshape(1, d)]
    kern = functools.partial(_gqa_prep_kernel, rope=rope, n_q=n_q, n_kv=n_kv, q_scale=q_scale)
    return pl.pallas_call(
        kern, grid=(b, l // t),
        in_specs=in_specs,
        out_specs=[pl.BlockSpec((None, n_q, t, d), lambda bi, i: (bi, 0, i, 0)),
                   pl.BlockSpec((None, n_kv, t, d), lambda bi, i: (bi, 0, i, 0)),
                   pl.BlockSpec((None, n_kv, t, 2 * d), lambda bi, i: (bi, 0, i, 0))],
        out_shape=[jax.ShapeDtypeStruct((b, n_q, l, d), BF16),
                   jax.ShapeDtypeStruct((b, n_kv, l, d), BF16),
                   jax.ShapeDtypeStruct((b, n_kv, l, 2 * d), BF16)],
        compiler_params=_cparams("parallel", "parallel"),
        name="gqa_prep",
    )(*args)


def _axial_rope_tables(l):
    nf = HEAD_DIM // 4
    t = jnp.arange(l)
    inv = ROPE_BASE ** (-jnp.arange(nf, dtype=F32) / nf)
    ang_r = (t // GRID_W).astype(F32)[:, None] * inv
    ang_c = (t % GRID_W).astype(F32)[:, None] * inv
    cr, sr, cc, sc = jnp.cos(ang_r), jnp.sin(ang_r), jnp.cos(ang_c), jnp.sin(ang_c)
    return (jnp.concatenate([cr, cr, cc, cc], -1), jnp.concatenate([-sr, sr, -sc, sc], -1))


FLASH_SUB_ROWS = 128


def _flash_kernel(q_ref, k_ref, v_ref, o_ref, m_ref, acc_ref):
    j = pl.program_id(3)
    g, tq, d = q_ref.shape

    @pl.when(j == 0)
    def _():
        m_ref[...] = jnp.full(m_ref.shape, NEG_BIG, F32)
        acc_ref[...] = jnp.zeros(acc_ref.shape, F32)

    q = q_ref[...].reshape(g * tq, d)
    k = k_ref[...]
    v = v_ref[...]
    n_sub = (g * tq) // FLASH_SUB_ROWS
    s, p, alpha = [None] * n_sub, [None] * n_sub, [None] * n_sub
    for t in range(n_sub + 2):
        if t < n_sub:
            s[t] = lax.dot_general(q[t * FLASH_SUB_ROWS:(t + 1) * FLASH_SUB_ROWS], k, NT_DIMS,
                                   preferred_element_type=F32)
        u = t - 1
        if 0 <= u < n_sub:
            rows = slice(u * FLASH_SUB_ROWS, (u + 1) * FLASH_SUB_ROWS)
            m_prev = m_ref[rows]
            m_new = jnp.maximum(m_prev, jnp.max(s[u], axis=-1, keepdims=True))
            alpha[u] = jnp.exp2(m_prev - m_new)
            m_ref[rows] = m_new
            p[u] = jnp.exp2((s[u] - m_new).astype(BF16))
            s[u] = None
        w = t - 2
        if 0 <= w < n_sub:
            rows = slice(w * FLASH_SUB_ROWS, (w + 1) * FLASH_SUB_ROWS)
            acc_ref[rows] = alpha[w] * acc_ref[rows] + jnp.dot(p[w], v, preferred_element_type=F32)
            p[w] = None

    @pl.when(j == pl.num_programs(3) - 1)
    def _():
        acc = acc_ref[...]
        o = acc[:, :d] / acc[:, d:d + 1]
        o_ref[...] = jnp.concatenate([o[h * tq:(h + 1) * tq] for h in range(g)], axis=1).astype(o_ref.dtype)


def flash_gqa(q, k, v):
    b, hq, l, d = q.shape
    hkv, lk = k.shape[1], k.shape[2]
    g = hq // hkv
    tq = _pick_tile(l, 1024, 16)
    tk = _pick_tile(lk, 3328, LANES)
    assert (g * tq) % FLASH_SUB_ROWS == 0
    return pl.pallas_call(
        _flash_kernel, grid=(b, hkv, l // tq, lk // tk),
        in_specs=[pl.BlockSpec((None, g, tq, d), lambda bi, h, i, j: (bi, h, i, 0)),
                  pl.BlockSpec((None, None, tk, d), lambda bi, h, i, j: (bi, h, j, 0)),
                  pl.BlockSpec((None, None, tk, 2 * d), lambda bi, h, i, j: (bi, h, j, 0))],
        out_specs=pl.BlockSpec((None, tq, g * d), lambda bi, h, i, j: (bi, i, h)),
        out_shape=jax.ShapeDtypeStruct((b, l, hq * d), BF16),
        scratch_shapes=[pltpu.VMEM((g * tq, 1), F32), pltpu.VMEM((g * tq, 2 * d), F32)],
        compiler_params=_cparams("parallel", "parallel", "parallel", "arbitrary"),
        name="flash_gqa",
    )(q, k, v)


def gqa_mixer(p_l, p_c, qn_w, kn_w, tables, with_ctx_out):
    n_q = p_l.shape[2] // (2 * HEAD_DIM)
    n_kv = n_q // 2
    scale = HEAD_DIM ** -0.5
    q_l, k_l, v_l = gqa_prep(p_l, qn_w, kn_w, n_q, n_kv, scale * math.log2(math.e), rope_tables=tables)
    q_c, k_c, v_c = gqa_prep(p_c, qn_w, kn_w, n_q, n_kv, scale)
    y_l = flash_gqa(q_l, jnp.concatenate([k_l, k_c], axis=2), jnp.concatenate([v_l, v_c], axis=2))
    y_c = ctx_attention(q_c, 0, k_c, 0, v_c, 0, n_q, n_q // n_kv) if with_ctx_out else None
    return y_l, y_c


def _split_w_in(w_in, d_model):
    mix_w = d_model // N_BRANCH
    h = mix_w // HEAD_DIM
    sizes = (3 * mix_w, 3 * h * HEAD_DIM, 2 * h * RET_DK + 2 * h * RET_DV,
             (h + 2 * max(h // 2, 1)) * HEAD_DIM, N_BRANCH * d_model)
    offs = np.cumsum((0,) + sizes)
    w_hy, w_na, w_rt, w_gq, w_gate = (w_in[:, offs[i]:offs[i + 1]] for i in range(5))
    na_scale = jnp.concatenate([jnp.full((h * HEAD_DIM,), HEAD_DIM ** -0.5, F32),
                                jnp.ones((2 * h * HEAD_DIM,), F32)])
    rt_scale = jnp.concatenate([jnp.ones((h * RET_DK,), F32), jnp.full((h * RET_DK,), RET_DK ** -0.5, F32),
                                jnp.ones((2 * h * RET_DV,), F32)])
    return ((w_hy).astype(BF16), (w_na * na_scale).astype(BF16), (w_rt * rt_scale).astype(BF16),
            w_gq.astype(BF16), w_gate.astype(BF16))


def kernel(x, c, ctx, c_ctx, ada_w, ada_b, norm1_w, norm2_w, w_in, hy_conv_w, hy_conv_b, hy_ffn_w1, hy_ffn_b1,
           hy_ffn_w2, hy_ffn_b2, hy_ffn_w3, hy_sin_freq, hy_bias, na_rpb, ret_log_decay, ret_gn_w, gqa_q_norm_w,
           gqa_k_norm_w, w_branch, w_out, ffn_w13, ffn_w2, final_norm_w):
    b, l, d = x.shape
    lc = ctx.shape[1]
    depth = ada_w.shape[0]
    ffn_hidden = ffn_w2.shape[1]
    n_ret_heads = (d // N_BRANCH) // RET_DV

    cs = jnp.zeros((8, d), F32).at[:b].set(c).at[b].set(c_ctx)
    mods = ada_mod(cs, ada_w, ada_b)
    ctx_row = b
    mats = hyena_mats(l, lc)
    ret_tables = _ret_rope_tables(l, n_ret_heads)
    gqa_tables = _axial_rope_tables(l)

    x_l = x.reshape(b * l, d)
    x_c = ctx.reshape(b * lc, d)
    for layer in range(depth):
        last = layer == depth - 1
        mod = mods[layer]
        w_hy, w_na, w_rt, w_gq, w_gate = _split_w_in(w_in[layer], d)
        hy_params = (hy_conv_w[layer], hy_conv_b[layer], hy_ffn_w1[layer], hy_ffn_b1[layer], hy_ffn_w2[layer],
                     hy_ffn_b2[layer], hy_ffn_w3[layer], hy_sin_freq[layer], hy_bias[layer])
        wb = w_branch[layer].astype(BF16)
        wo = w_out[layer].astype(BF16)
        w1 = ffn_w13[layer][:, :ffn_hidden].astype(BF16)
        w3 = ffn_w13[layer][:, ffn_hidden:].astype(BF16)
        w2 = ffn_w2[layer].astype(BF16)

        h_l = norm_mod(x_l, norm1_w[layer], mod, 0, 1, l, 0)
        h_c = norm_mod(x_c, norm1_w[layer], mod, 0, 1, None, ctx_row)

        na_l = matmul_heads(h_l, w_na, b)
        na_c = matmul_heads(h_c, w_na, b)
        rt_l = matmul(h_l, w_rt).reshape(b, l, -1)
        rt_c = matmul(h_c, w_rt).reshape(b, lc, -1)
        gq_l = matmul(h_l, w_gq).reshape(b, l, -1)
        gq_c = matmul(h_c, w_gq).reshape(b, lc, -1)
        hy_l = matmul(h_l, w_hy).reshape(b, l, -1)
        gate_l = matmul(h_l, w_gate)

        y_hy_l = hyena_mixer(hy_l, *hy_params, mats)
        y_na_l = na_attention(na_l, na_c, na_rpb[layer])
        y_rt_l, y_rt_c = retention_mixer(rt_l, rt_c, ret_log_decay[layer], ret_gn_w[layer], ret_tables, not last)
        y_gq_l, y_gq_c = gqa_mixer(gq_l, gq_c, gqa_q_norm_w[layer], gqa_k_norm_w[layer], gqa_tables, not last)

        m_l = merge_branches([y.reshape(b * l, -1) for y in (y_hy_l, y_na_l, y_rt_l, y_gq_l)], gate_l, wb)
        x_l = matmul_residual(m_l, wo, x_l, mod, 2, l, 0)
        h2 = norm_mod(x_l, norm2_w[layer], mod, 3, 4, l, 0)
        x_l = matmul_residual(matmul_swiglu(h2, w1, w3), w2, x_l, mod, 5, l, 0)

        if not last:
            n_na = na_c.shape[1] // 3
            hy_c = matmul(h_c, w_hy).reshape(b, lc, -1)
            gate_c = matmul(h_c, w_gate)
            y_hy_c = hyena_mixer(hy_c, *hy_params, mats)
            y_na_c = ctx_attention(na_c, 0, na_c, n_na, na_c, 2 * n_na, n_na, 1)
            m_c = merge_branches([y.reshape(b * lc, -1) for y in (y_hy_c, y_na_c, y_rt_c, y_gq_c)], gate_c, wb)
            x_c = matmul_residual(m_c, wo, x_c, mod, 2, None, ctx_row)
            h2c = norm_mod(x_c, norm2_w[layer], mod, 3, 4, None, ctx_row)
            x_c = matmul_residual(matmul_swiglu(h2c, w1, w3), w2, x_c, mod, 5, None, ctx_row)

    return final_norm(x_l, final_norm_w).reshape(b, l, d)
```

```python
import functools
import math

import numpy as np
import jax
import jax.numpy as jnp
from jax import lax
from jax.experimental import pallas as pl
from jax.experimental.pallas import tpu as pltpu

F32 = jnp.float32
BF16 = jnp.bfloat16

EPS = 1e-6
GRID_W = 64
HEAD_DIM = 128
N_BRANCH = 4
ROPE_BASE = 10000.0
HY_BANDS = 16
HY_DECAY_TARGET = 1e-2
HY_FAST_PCT = 0.3
HY_SLOW_PCT = 1.5
NA_WIN_R = 8
NA_WIN_C = 16
RET_DK = 64
RET_DV = 128
RET_CHUNK = 128
RET_ROPE_BASE = 10000.0
NEG_BIG = -1e30

LANES = 128
FFT_N2 = 128
VMEM_LIMIT = 56 * 1024 * 1024

NT_DIMS = (((1,), (1,)), ((), ()))
TN_DIMS = (((0,), (0,)), ((), ()))


def _cparams(*sem):
    return pltpu.CompilerParams(dimension_semantics=sem, vmem_limit_bytes=VMEM_LIMIT)


def _pick_tile(n, cap, mult):
    best = None
    for t in range(mult, min(n, cap) + 1, mult):
        if n % t == 0:
            best = t
    assert best is not None, (n, cap, mult)
    return best


def _swap32(x):
    n = x.shape[-1]
    lane = lax.broadcasted_iota(jnp.int32, x.shape, x.ndim - 1)
    up = pltpu.roll(x, n - 32, x.ndim - 1)
    down = pltpu.roll(x, 32, x.ndim - 1)
    return jnp.where((lane % 64) < 32, up, down)


def _silu(x):
    return x * jax.nn.sigmoid(x)


def _ada_kernel(c_ref, w_ref, b_ref, o_ref):
    a = _silu(c_ref[...]).astype(BF16)
    o_ref[...] = jnp.dot(a, w_ref[...].astype(BF16), preferred_element_type=F32) + b_ref[...]


def ada_mod(cs, ada_w, ada_b):
    depth, d, n = ada_w.shape
    tn = _pick_tile(n, 1536, LANES)
    return pl.pallas_call(
        _ada_kernel,
        grid=(depth, n // tn),
        in_specs=[pl.BlockSpec((8, d), lambda l, j: (0, 0)),
                  pl.BlockSpec((None, d, tn), lambda l, j: (l, 0, j)),
                  pl.BlockSpec((None, 1, tn), lambda l, j: (l, 0, j))],
        out_specs=pl.BlockSpec((None, 8, tn), lambda l, j: (l, 0, j)),
        out_shape=jax.ShapeDtypeStruct((depth, 8, n), F32),
        compiler_params=_cparams("parallel", "parallel"),
        name="ada_mod",
    )(cs, ada_w, ada_b.reshape(depth, 1, n))


def _mod_row(row_base, tiles_per_batch, axis):
    if tiles_per_batch is None:
        return row_base
    return row_base + pl.program_id(axis) // tiles_per_batch


def _norm_mod_kernel(x_ref, w_ref, sh_ref, sc_ref, o_ref, *, row_base, tiles_per_batch):
    x = x_ref[...]
    y = x * lax.rsqrt(jnp.mean(x * x, axis=-1, keepdims=True) + EPS) * w_ref[...]
    row = _mod_row(row_base, tiles_per_batch, 0)
    sh = sh_ref[pl.ds(row, 1), :]
    sc = sc_ref[pl.ds(row, 1), :]
    o_ref[...] = (y * (1.0 + sc) + sh).astype(o_ref.dtype)


def _norm_kernel(x_ref, w_ref, o_ref):
    x = x_ref[...]
    y = x * lax.rsqrt(jnp.mean(x * x, axis=-1, keepdims=True) + EPS) * w_ref[...]
    o_ref[...] = y.astype(o_ref.dtype)


def norm_mod(x, w, mod, sh_chunk, sc_chunk, rows_per_batch, row_base):
    m, d = x.shape
    tm = _pick_tile(m if rows_per_batch is None else rows_per_batch, 512, 8)
    tpb = None if rows_per_batch is None else rows_per_batch // tm
    kern = functools.partial(_norm_mod_kernel, row_base=row_base, tiles_per_batch=tpb)
    return pl.pallas_call(
        kern, grid=(m // tm,),
        in_specs=[pl.BlockSpec((tm, d), lambda i: (i, 0)),
                  pl.BlockSpec((1, d), lambda i: (0, 0)),
                  pl.BlockSpec((8, d), lambda i: (0, sh_chunk)),
                  pl.BlockSpec((8, d), lambda i: (0, sc_chunk))],
        out_specs=pl.BlockSpec((tm, d), lambda i: (i, 0)),
        out_shape=jax.ShapeDtypeStruct((m, d), BF16),
        compiler_params=_cparams("parallel"),
        name="norm_mod",
    )(x, w.reshape(1, d), mod, mod)


def final_norm(x, w):
    m, d = x.shape
    tm = _pick_tile(m, 512, 8)
    return pl.pallas_call(
        _norm_kernel, grid=(m // tm,),
        in_specs=[pl.BlockSpec((tm, d), lambda i: (i, 0)), pl.BlockSpec((1, d), lambda i: (0, 0))],
        out_specs=pl.BlockSpec((tm, d), lambda i: (i, 0)),
        out_shape=jax.ShapeDtypeStruct((m, d), F32),
        compiler_params=_cparams("parallel"),
        name="final_norm",
    )(x, w.reshape(1, d))


def _mm_kernel(a_ref, w_ref, o_ref):
    o_ref[...] = jnp.dot(a_ref[...], w_ref[...], preferred_element_type=F32).astype(o_ref.dtype)


def _mm_heads_kernel(a_ref, w_ref, o_ref):
    acc = jnp.dot(a_ref[...], w_ref[...], preferred_element_type=F32)
    for h in range(o_ref.shape[0]):
        o_ref[h] = acc[:, h * HEAD_DIM:(h + 1) * HEAD_DIM].astype(o_ref.dtype)


def _mm_swiglu_kernel(a_ref, w1_ref, w3_ref, o_ref):
    a = a_ref[...]
    u = jnp.dot(a, w1_ref[...], preferred_element_type=F32)
    g = jnp.dot(a, w3_ref[...], preferred_element_type=F32)
    o_ref[...] = (_silu(u) * g).astype(o_ref.dtype)


def _mm_residual_kernel(a_ref, w_ref, x_ref, g_ref, o_ref, *, row_base, tiles_per_batch):
    acc = jnp.dot(a_ref[...], w_ref[...], preferred_element_type=F32)
    row = _mod_row(row_base, tiles_per_batch, 1)
    o_ref[...] = x_ref[...] + g_ref[pl.ds(row, 1), :] * acc


class WCols:
    def __init__(self, arr, layer, off, n):
        self.arr, self.layer, self.off, self.n = arr, layer, off, n

    def tile(self, cap):
        tn = max(t for t in range(LANES, min(self.n, cap) + 1, LANES) if self.n % t == 0 and self.off % t == 0)
        return tn

    def spec(self, tn):
        layer, base, k = self.layer, self.off // tn, self.arr.shape[1]
        return pl.BlockSpec((None, k, tn), lambda j, i: (layer, 0, base + j))


def matmul(a, w, out_dtype=BF16):
    m, k = a.shape
    n = w.n
    tm = _pick_tile(m, 512, 16)
    tn = w.tile(2048)
    return pl.pallas_call(
        _mm_kernel, grid=(n // tn, m // tm),
        in_specs=[pl.BlockSpec((tm, k), lambda j, i: (i, 0)), w.spec(tn)],
        out_specs=pl.BlockSpec((tm, tn), lambda j, i: (i, j)),
        out_shape=jax.ShapeDtypeStruct((m, n), out_dtype),
        compiler_params=_cparams("parallel", "parallel"),
        name="matmul",
    )(a, w.arr)


def matmul_heads(a, w, batch):
    m, k = a.shape
    n = w.n
    lb = m // batch
    tm = _pick_tile(lb, 512, 16)
    tn = w.tile(1536)
    tpb = lb // tm
    nh = tn // HEAD_DIM
    return pl.pallas_call(
        _mm_heads_kernel, grid=(n // tn, m // tm),
        in_specs=[pl.BlockSpec((tm, k), lambda j, i: (i, 0)), w.spec(tn)],
        out_specs=pl.BlockSpec((None, nh, tm, HEAD_DIM), lambda j, i: (i // tpb, j, i % tpb, 0)),
        out_shape=jax.ShapeDtypeStruct((batch, n // HEAD_DIM, lb, HEAD_DIM), BF16),
        compiler_params=_cparams("parallel", "parallel"),
        name="matmul_heads",
    )(a, w.arr)


def matmul_swiglu(a, w1, w3):
    m, k = a.shape
    n = w1.n
    tm = _pick_tile(m, 1024, 16)
    tn = min(w1.tile(704), w3.tile(704))
    return pl.pallas_call(
        _mm_swiglu_kernel, grid=(n // tn, m // tm),
        in_specs=[pl.BlockSpec((tm, k), lambda j, i: (i, 0)), w1.spec(tn), w3.spec(tn)],
        out_specs=pl.BlockSpec((tm, tn), lambda j, i: (i, j)),
        out_shape=jax.ShapeDtypeStruct((m, n), BF16),
        compiler_params=_cparams("parallel", "parallel"),
        name="matmul_swiglu",
    )(a, w1.arr, w3.arr)


def matmul_residual(a, w, x, mod, gate_chunk, rows_per_batch, row_base):
    m, k = a.shape
    n = w.n
    small_k = k <= 2048
    tm = _pick_tile(m if rows_per_batch is None else rows_per_batch, 1024 if small_k else 512, 16)
    tn = w.tile(1024 if small_k else 512)
    tpb = None if rows_per_batch is None else rows_per_batch // tm
    kern = functools.partial(_mm_residual_kernel, row_base=row_base, tiles_per_batch=tpb)
    gblk = gate_chunk * (n // tn)
    return pl.pallas_call(
        kern, grid=(n // tn, m // tm),
        in_specs=[pl.BlockSpec((tm, k), lambda j, i: (i, 0)),
                  w.spec(tn),
                  pl.BlockSpec((tm, tn), lambda j, i: (i, j)),
                  pl.BlockSpec((8, tn), lambda j, i: (0, gblk + j))],
        out_specs=pl.BlockSpec((tm, tn), lambda j, i: (i, j)),
        out_shape=jax.ShapeDtypeStruct((m, n), F32),
        compiler_params=_cparams("parallel", "parallel"),
        name="matmul_residual",
    )(a, w.arr, x, mod)


def _merge_kernel(y0, y1, y2, y3, g0, g1, g2, g3, wb_ref, o_ref):
    acc = None
    for i, (y, g) in enumerate(((y0, g0), (y1, g1), (y2, g2), (y3, g3))):
        gate = 0.5 * jnp.tanh(0.5 * g[...].astype(F32)) + 0.5
        t = gate * jnp.dot(y[...], wb_ref[i], preferred_element_type=F32)
        acc = t if acc is None else acc + t
    o_ref[...] = acc.astype(o_ref.dtype)


def merge_branches(ys, gate_pre, wb, layer):
    m, w = ys[0].shape
    d = wb.shape[3]
    tm = _pick_tile(m, 512, 16)
    tn = _pick_tile(d, 1024, LANES)
    nb = d // tn
    y_spec = pl.BlockSpec((tm, w), lambda j, i: (i, 0))
    g_specs = [pl.BlockSpec((tm, tn), functools.partial(lambda j, i, b: (i, b * nb + j), b=b))
               for b in range(N_BRANCH)]
    return pl.pallas_call(
        _merge_kernel, grid=(nb, m // tm),
        in_specs=[y_spec] * N_BRANCH + g_specs + [pl.BlockSpec((None, N_BRANCH, w, tn),
                                                               lambda j, i: (layer, 0, 0, j))],
        out_specs=pl.BlockSpec((tm, tn), lambda j, i: (i, j)),
        out_shape=jax.ShapeDtypeStruct((m, d), BF16),
        compiler_params=_cparams("parallel", "parallel"),
        name="merge_branches",
    )(*ys, gate_pre, gate_pre, gate_pre, gate_pre, wb)


FFT_ROWS = 16


def _store_cols(o_ref, lead, y, n2):
    cw = y.shape[1]
    tiles = jnp.swapaxes(y.reshape(y.shape[0] // n2, n2, cw), 0, 1)
    for j in range(n2):
        o_ref[lead + (slice(None), slice(j * cw, (j + 1) * cw))] = tiles[j]


def _short_conv_kernel(x_ref, prev_ref, next_ref, w_ref, b_ref, *o_refs, n_tiles, col_n2):
    i = pl.program_id(1)
    t = x_ref.shape[0]
    halo = prev_ref.shape[0]
    cw = x_ref.shape[1] // len(o_refs)
    row = lax.broadcasted_iota(jnp.int32, (t, cw), 0)
    for n, o_ref in enumerate(o_refs):
        cols = slice(n * cw, (n + 1) * cw)
        x = x_ref[:, cols].astype(F32)
        before = jnp.where(i > 0, prev_ref[:, cols].astype(F32)[halo - 1:halo, :], 0.0)
        after = jnp.where(i < n_tiles - 1, next_ref[:, cols].astype(F32)[0:1, :], 0.0)
        xm1 = jnp.where(row == 0, before, pltpu.roll(x, 1, 0))
        xp1 = jnp.where(row == t - 1, after, pltpu.roll(x, t - 1, 0))
        w = w_ref[:, cols]
        y = (w[0:1, :] * xm1 + w[1:2, :] * x + w[2:3, :] * xp1 + b_ref[:, cols]).astype(o_ref.dtype)
        if col_n2 is None:
            o_ref[...] = y
        else:
            _store_cols(o_ref, (), y, col_n2)


def short_conv(p, conv_w, conv_b, n_out, col_n2=None):
    b, l, c = p.shape
    cw = c // n_out
    halo = 16
    t = _pick_tile(l, 512, halo) if col_n2 is None else FFT_ROWS * col_n2
    n_tiles = l // t
    hb = t // halo
    kern = functools.partial(_short_conv_kernel, n_tiles=n_tiles, col_n2=col_n2)
    if col_n2 is None:
        out_specs = [pl.BlockSpec((None, t, cw), lambda bi, i: (bi, i, 0))] * n_out
        out_shape = [jax.ShapeDtypeStruct((b, l, cw), BF16)] * n_out
    else:
        out_specs = [pl.BlockSpec((None, FFT_ROWS, col_n2 * cw), lambda bi, i: (bi, i, 0))] * n_out
        out_shape = [jax.ShapeDtypeStruct((b, l // col_n2, col_n2 * cw), BF16)] * n_out
    return pl.pallas_call(
        kern, grid=(b, n_tiles),
        in_specs=[pl.BlockSpec((None, t, c), lambda bi, i: (bi, i, 0)),
                  pl.BlockSpec((None, halo, c), lambda bi, i: (bi, jnp.maximum(i * hb - 1, 0), 0)),
                  pl.BlockSpec((None, halo, c), lambda bi, i: (bi, jnp.minimum((i + 1) * hb, l // halo - 1), 0)),
                  pl.BlockSpec((8, c), lambda bi, i: (0, 0)),
                  pl.BlockSpec((1, c), lambda bi, i: (0, 0))],
        out_specs=out_specs, out_shape=out_shape,
        compiler_params=_cparams("parallel", "parallel"),
        name="short_conv",
    )(p, p, p, jnp.pad(conv_w, ((0, 8 - conv_w.shape[0]), (0, 0))), conv_b.reshape(1, c))


def _hy_filter_kernel(fv_ref, c2_ref, s2_ref, w1_ref, b1_ref, sf1_ref, w2_ref, b2_ref, sf2_ref, w3f_ref, w3b_ref,
                      dl_ref, f_ref, ss_ref, cb_ref, sb_ref, *, seq_len, col_n2):
    i = pl.program_id(0)
    tl = cb_ref.shape[0]
    hi = lax.Precision.HIGHEST
    step = 2.0 * math.pi / float(seq_len)
    r = lax.broadcasted_iota(jnp.int32, (tl, 1), 0)
    lane = lax.broadcasted_iota(jnp.int32, (1, LANES), 1)

    @pl.when(i == 0)
    def _():
        ang_r = (step * r.astype(F32)) * fv_ref[...]
        cb_ref[...] = jnp.cos(ang_r)
        sb_ref[...] = jnp.sin(ang_r)
        ss_ref[...] = jnp.zeros(ss_ref.shape, F32)

    base = (step * (i * tl).astype(F32)) * fv_ref[...]
    ca, sa = jnp.cos(base), jnp.sin(base)
    cb, sb = cb_ref[...], sb_ref[...]
    cos_j, sin_j = ca * cb - sa * sb, sa * cb + ca * sb
    cos_m, sin_m = c2_ref[...] * cos_j + s2_ref[...] * sin_j, s2_ref[...] * cos_j - c2_ref[...] * sin_j
    j = (i * tl + r).astype(F32)

    def mlp(t, cos_t, sin_t):
        t_norm = t / float(max(seq_len - 1, 1))
        feat = jnp.where(lane == 0, t_norm,
                         jnp.where(lane <= HY_BANDS, cos_t, jnp.where(lane <= 2 * HY_BANDS, -sin_t, 0.0)))
        z = jnp.sin(sf1_ref[...] * (jnp.dot(feat, w1_ref[...], precision=hi, preferred_element_type=F32)
                                    + b1_ref[...]))
        z = jnp.sin(sf2_ref[...] * (jnp.dot(z, w2_ref[...], precision=hi, preferred_element_type=F32)
                                    + b2_ref[...]))
        return z.astype(BF16), t_norm

    zf, tnf = mlp(j, cos_j, sin_j)
    zb, tnb = mlp(float(seq_len) - j, cos_m, sin_m)
    cw = w3f_ref.shape[1] // f_ref.shape[0]
    for o in range(f_ref.shape[0]):
        cols = slice(o * cw, (o + 1) * cw)
        hf = jnp.dot(zf, w3f_ref[:, cols], preferred_element_type=F32) * jnp.exp(-tnf * dl_ref[:, cols])
        hb = jnp.dot(zb, w3b_ref[:, cols], preferred_element_type=F32) * jnp.exp(-tnb * dl_ref[:, cols])
        hb = jnp.where(j > 0.0, hb, 0.0)
        ss_ref[:, cols] += jnp.sum(hf * hf, axis=0, keepdims=True) + jnp.sum(hb * hb, axis=0, keepdims=True)
        for s, h in enumerate((hf, hb)):
            if col_n2 is None:
                f_ref[o, s] = h.astype(f_ref.dtype)
            else:
                _store_cols(f_ref, (o, s), h.astype(f_ref.dtype), col_n2)


def hy_filters(seq_len, w1, b1, w2, b2, w3, sin_freq, hy_w, col_n2=None):
    emb, ffn = w1.shape
    cw = 2 * hy_w
    f = np.linspace(1e-4, HY_BANDS - 1, HY_BANDS)
    fv = np.zeros((1, LANES), np.float32)
    fv[0, 1:1 + HY_BANDS] = f
    fv[0, 1 + HY_BANDS:1 + 2 * HY_BANDS] = f
    c2 = np.cos(2.0 * np.pi * fv.astype(np.float64)).astype(np.float32)
    s2 = np.sin(2.0 * np.pi * fv.astype(np.float64)).astype(np.float32)
    w1p = jnp.pad(w1, ((0, LANES - emb), (0, 0)))
    w3r = w3.reshape(ffn, 2, 2, hy_w)
    w3f = w3r[:, :, 0, :].reshape(ffn, cw).astype(BF16)
    w3b = w3r[:, :, 1, :].reshape(ffn, cw).astype(BF16)
    deltas = np.abs(np.linspace(math.log(HY_DECAY_TARGET) / HY_SLOW_PCT,
                                math.log(HY_DECAY_TARGET) / HY_FAST_PCT, hy_w)).astype(np.float32)
    dl = jnp.asarray(np.tile(deltas, 2).reshape(1, cw))
    const = lambda i: (0, 0)
    if col_n2 is None:
        tl = _pick_tile(seq_len, 512, 16)
        f_spec = pl.BlockSpec((2, 2, tl, hy_w), lambda i: (0, 0, i, 0))
        f_shape = jax.ShapeDtypeStruct((2, 2, seq_len, hy_w), BF16)
    else:
        tl = FFT_ROWS * col_n2
        f_spec = pl.BlockSpec((2, 2, FFT_ROWS, col_n2 * hy_w), lambda i: (0, 0, i, 0))
        f_shape = jax.ShapeDtypeStruct((2, 2, seq_len // col_n2, col_n2 * hy_w), BF16)
    kern = functools.partial(_hy_filter_kernel, seq_len=seq_len, col_n2=col_n2)
    return pl.pallas_call(
        kern, grid=(seq_len // tl,),
        in_specs=[pl.BlockSpec((1, LANES), const), pl.BlockSpec((1, LANES), const), pl.BlockSpec((1, LANES), const),
                  pl.BlockSpec((LANES, ffn), const),
                  pl.BlockSpec((1, ffn), const), pl.BlockSpec((1, ffn), const),
                  pl.BlockSpec((ffn, ffn), const), pl.BlockSpec((1, ffn), const), pl.BlockSpec((1, ffn), const),
                  pl.BlockSpec((ffn, cw), const), pl.BlockSpec((ffn, cw), const), pl.BlockSpec((1, cw), const)],
        out_specs=[f_spec, pl.BlockSpec((1, cw), const)],
        out_shape=[f_shape, jax.ShapeDtypeStruct((1, cw), F32)],
        scratch_shapes=[pltpu.VMEM((tl, LANES), F32), pltpu.VMEM((tl, LANES), F32)],
        compiler_params=_cparams("arbitrary"),
        name="hy_filters",
    )(jnp.asarray(fv), jnp.asarray(c2), jnp.asarray(s2), w1p, b1.reshape(1, ffn), sin_freq[0].reshape(1, ffn), w2,
      b2.reshape(1, ffn), sin_freq[1].reshape(1, ffn), w3f, w3b, dl)


def _phase_mats(phase_num, denom, conj):
    ang = np.pi * (phase_num % (2 * denom)).astype(np.float64) / denom
    cr, ci = np.cos(ang), (np.sin(ang) if conj else -np.sin(ang))
    return cr, ci


def _stage1_mats(n1):
    k = np.arange(n1)[:, None]
    n = np.arange(n1)[None, :]
    w1d, w1f, wfin = [], [], []
    for v in (0, 1):
        cr, ci = _phase_mats(2 * k * n + v * n, n1, conj=False)
        w1d.append(np.block([[cr, -ci], [ci, cr]]))
        sgn = 1.0 if v == 0 else -1.0
        w1f.append(np.block([[cr, sgn * cr], [ci, sgn * ci]]))
        cri, cii = _phase_mats(2 * n.T * k.T + v * n.T, n1, conj=True)
        wfin.append(np.block([[cri, -cii], [cii, cri]]))
    return (np.concatenate(w1d, 0).astype(np.float32), np.concatenate(w1f, 0).astype(np.float32),
            np.concatenate(wfin, 1).astype(np.float32))


def _stage2_mats(n1, n2):
    l = n1 * n2
    k2 = np.arange(n2)[:, None]
    nn = np.arange(n2)[None, :]
    base = (2 * n1 * k2 * nn) % (2 * l)
    br, bi = np.cos(np.pi * base / l), -np.sin(np.pi * base / l)
    k1 = np.arange(n1)[None, :, None]
    v = np.arange(2)[:, None, None]
    tw = (2 * np.arange(n2)[None, None, :] * k1 + v * np.arange(n2)[None, None, :]) % (2 * l)
    tr, ti = np.cos(np.pi * tw / l), -np.sin(np.pi * tw / l)
    br, bi, tr, ti = (jnp.asarray(a, F32) for a in (br, bi, tr, ti))
    cr = br[None, None] * tr[:, :, None, :] - bi[None, None] * ti[:, :, None, :]
    ci = br[None, None] * ti[:, :, None, :] + bi[None, None] * tr[:, :, None, :]
    fwd = jnp.concatenate([jnp.concatenate([cr, -ci], -1), jnp.concatenate([ci, cr], -1)], -2)
    crt, cit = jnp.swapaxes(cr, -1, -2), -jnp.swapaxes(ci, -1, -2)
    inv = jnp.concatenate([jnp.concatenate([crt, -cit], -1), jnp.concatenate([cit, crt], -1)], -2)
    return fwd.astype(BF16), inv.astype(BF16)


def _fft_stage1_kernel(z_ref, w_ref, o_ref):
    n1 = z_ref.shape[1]
    nj, cw = o_ref.shape[3], o_ref.shape[4]
    x = z_ref[...].reshape(2 * n1, nj * cw)
    for plane in range(4):
        y = jnp.dot(w_ref[plane * n1:(plane + 1) * n1, :], x, preferred_element_type=F32).astype(o_ref.dtype)
        tiles = jnp.stack([y[:, j * cw:(j + 1) * cw] for j in range(nj)], axis=0)
        o_ref[plane // 2, plane % 2] = jnp.swapaxes(tiles, 0, 1)


FFT_COLS = 2048
FFT_S1_N2 = 16


def fft_stage1(z, w1, n1, n2):
    grp = z.shape[0]
    w = z.shape[3] // n2
    nj = _pick_tile(n2, FFT_S1_N2, 16)
    return pl.pallas_call(
        _fft_stage1_kernel, grid=(grp, n2 // nj),
        in_specs=[pl.BlockSpec((None, 2, n1, nj * w), lambda gi, j: (gi, 0, 0, j)),
                  pl.BlockSpec((4 * n1, 2 * n1), lambda gi, j: (0, 0))],
        out_specs=pl.BlockSpec((None, 2, 2, n1, nj, w), lambda gi, j: (gi, 0, 0, 0, j, 0)),
        out_shape=jax.ShapeDtypeStruct((grp, 2, 2, n1, n2, w), BF16),
        compiler_params=_cparams("parallel", "parallel"),
        name="fft_stage1",
    )(z, w1)


def _fft_mid_kernel(a_ref, f_ref, ss_ref, wf_ref, wi_ref, o_ref, *, scale):
    kb, n2, cw = a_ref.shape[1], a_ref.shape[2], a_ref.shape[3]
    nrm = lax.rsqrt(ss_ref[...] + EPS) * scale
    us = []
    for kk in range(kb):
        wf = wf_ref[kk]
        t = jnp.dot(wf, a_ref[:, kk].reshape(2 * n2, cw), preferred_element_type=F32)
        g = jnp.dot(wf, f_ref[:, kk].reshape(2 * n2, cw), preferred_element_type=F32) * nrm
        tr, ti, gr, gi = t[:n2], t[n2:], g[:n2], g[n2:]
        p = jnp.concatenate([tr * gr - ti * gi, tr * gi + ti * gr], axis=0).astype(BF16)
        us.append(jnp.dot(wi_ref[kk], p, preferred_element_type=F32).astype(o_ref.dtype))
    for r in range(2):
        tiles = jnp.swapaxes(jnp.stack([u[r * n2:(r + 1) * n2] for u in us], axis=0), 0, 1)
        for j in range(n2):
            o_ref[r, :, j * cw:(j + 1) * cw] = tiles[j]


FFT_MID_K1 = 16


def fft_mid(a, af, ss, wf, wi, order, n1, n2):
    cw = a.shape[-1]
    kb = _pick_tile(n1, FFT_MID_K1, 16)
    kern = functools.partial(_fft_mid_kernel, scale=1.0 / (2.0 * n1 * n2))
    mat_spec = pl.BlockSpec((None, kb, 2 * n2, 2 * n2), lambda v, k: (v, k, 0, 0))
    return pl.pallas_call(
        kern, grid=(2, n1 // kb),
        in_specs=[pl.BlockSpec((None, 2, kb, n2, cw), lambda v, k: (v, 0, k, 0, 0)),
                  pl.BlockSpec((None, None, 2, kb, n2, cw), lambda v, k: (order, v, 0, k, 0, 0)),
                  pl.BlockSpec((1, cw), lambda v, k: (0, order)),
                  mat_spec, mat_spec],
        out_specs=pl.BlockSpec((None, 2, kb, n2 * cw), lambda v, k: (v, 0, k, 0)),
        out_shape=jax.ShapeDtypeStruct((2, 2, n1, n2 * cw), BF16),
        compiler_params=_cparams("parallel", "parallel"),
        name="fft_mid",
    )(a, af, ss, wf, wi)


def _fft_final_kernel(u_ref, w_ref, gate_ref, z_ref, bias_ref, o_ref):
    n1 = gate_ref.shape[1]
    cw = gate_ref.shape[2]
    u = u_ref[...].reshape(4 * n1, cw)
    y = jnp.dot(w_ref[...], u, preferred_element_type=F32).reshape(2, n1, cw)
    z = z_ref[...].astype(F32)
    o_ref[...] = (gate_ref[...].astype(F32) * (y + bias_ref[...] * z)).astype(o_ref.dtype)


def fft_final(u, wfin, gate, z, bias, n1, n2):
    cols = z.shape[2]
    cw = cols // n2
    tc = _pick_tile(cols, FFT_COLS, cw)
    return pl.pallas_call(
        _fft_final_kernel, grid=(cols // tc,),
        in_specs=[pl.BlockSpec((2, 2, n1, tc), lambda j: (0, 0, 0, j)),
                  pl.BlockSpec((2 * n1, 4 * n1), lambda j: (0, 0)),
                  pl.BlockSpec((2, n1, tc), lambda j: (0, 0, j)),
                  pl.BlockSpec((2, n1, tc), lambda j: (0, 0, j)),
                  pl.BlockSpec((1, tc), lambda j: (0, 0))],
        out_specs=pl.BlockSpec((2, n1, tc), lambda j: (0, 0, j)),
        out_shape=jax.ShapeDtypeStruct((2, n1, cols), BF16),
        compiler_params=_cparams("parallel"),
        name="fft_final",
    )(u, wfin, gate, z, jnp.tile(bias.reshape(1, cw), (1, tc // cw)))


def _hy_dense_kernel(z_ref, gate_ref, f_ref, ss_ref, bias_ref, wf_ref, wi_ref, o_ref):
    lc = z_ref.shape[1]
    cw = z_ref.shape[2]
    nrm = lax.rsqrt(ss_ref[...] + EPS) * (1.0 / (2.0 * lc))
    f1, f2 = f_ref[0].astype(F32), f_ref[1].astype(F32)
    filt = (((f1 + f2) * nrm).astype(BF16), ((f1 - f2) * nrm).astype(BF16))
    x = z_ref[...].reshape(2 * lc, cw)
    acc = None
    for v in (0, 1):
        wf = wf_ref[v]
        t = jnp.dot(wf, x, preferred_element_type=F32)
        g = jnp.dot(wf[:, :lc], filt[v], preferred_element_type=F32)
        tr, ti, gr, gi = t[:lc], t[lc:], g[:lc], g[lc:]
        p = jnp.concatenate([tr * gr - ti * gi, tr * gi + ti * gr], axis=0).astype(BF16)
        u = jnp.dot(wi_ref[v], p, preferred_element_type=F32)
        acc = u if acc is None else acc + u
    y = acc.reshape(2, lc, cw)
    o_ref[...] = (gate_ref[...].astype(F32) * (y + bias_ref[...] * z_ref[...].astype(F32))).astype(o_ref.dtype)


def hy_dense_conv(z_arr, gate_arr, f, ss, order, bias, wf, wi):
    lc, cw = z_arr.shape[1], z_arr.shape[2]
    return pl.pallas_call(
        _hy_dense_kernel, grid=(1,),
        in_specs=[pl.BlockSpec((2, lc, cw), lambda i: (0, 0, 0)),
                  pl.BlockSpec((2, lc, cw), lambda i: (0, 0, 0)),
                  pl.BlockSpec((None, 2, lc, cw), lambda i: (order, 0, 0, 0)),
                  pl.BlockSpec((1, cw), lambda i: (0, order)),
                  pl.BlockSpec((1, cw), lambda i: (0, 0)),
                  pl.BlockSpec((2, None, 2 * lc, 2 * lc), lambda i: (0, 0, 0, 0)),
                  pl.BlockSpec((2, None, 2 * lc, 2 * lc), lambda i: (0, 0, 0, 0))],
        out_specs=pl.BlockSpec((2, lc, cw), lambda i: (0, 0, 0)),
        out_shape=jax.ShapeDtypeStruct((2, lc, cw), BF16),
        compiler_params=_cparams("arbitrary"),
        name="hy_dense_conv",
    )(z_arr, gate_arr, f, ss, bias.reshape(1, cw), wf, wi)


def hyena_mixer(p, conv_w, conv_b, w1, b1, w2, b2, w3, sin_freq, bias, mats):
    b, l, c3 = p.shape
    assert b == 2, "batch elements are packed as the re/im parts of one complex signal"
    hy_w = c3 // 3
    if l <= 2 * FFT_N2:
        v, x1, x2 = short_conv(p, conv_w, conv_b, 3)
        f, ss = hy_filters(l, w1, b1, w2, b2, w3, sin_freq, hy_w)
        wf, wi = mats["dense"]
        z2 = hy_dense_conv(v, x1, f, ss, 0, bias[0], wf, wi)
        return hy_dense_conv(z2, x2, f, ss, 1, bias[1], wf, wi)
    n2 = FFT_N2
    n1 = l // n2
    w1d, w1f, wfin, wf, wi = mats["fft"]
    v, x1, x2 = short_conv(p, conv_w, conv_b, 3, col_n2=n2)
    f, ss = hy_filters(l, w1, b1, w2, b2, w3, sin_freq, hy_w, col_n2=n2)
    af = fft_stage1(f, w1f, n1, n2)
    a = fft_stage1(v[None], w1d, n1, n2)[0]
    z2 = fft_final(fft_mid(a, af, ss, wf, wi, 0, n1, n2), wfin, x1, v, bias[0], n1, n2)
    a = fft_stage1(z2[None], w1d, n1, n2)[0]
    y = fft_final(fft_mid(a, af, ss, wf, wi, 1, n1, n2), wfin, x2, z2, bias[1], n1, n2)
    return y.reshape(2, l, hy_w)


def hyena_mats(l_lat, l_ctx):
    n1 = l_lat // FFT_N2
    w1d, w1f, wfin = _stage1_mats(n1)
    wf, wi = _stage2_mats(n1, FFT_N2)
    mats = {"fft": (jnp.asarray(w1d, BF16), jnp.asarray(w1f, BF16), jnp.asarray(wfin, BF16), wf, wi)}
    mats["dense"] = _stage2_mats(1, l_ctx)
    return mats


def _ctx_attn_kernel(q_ref, k_ref, v_ref, o_ref):
    s = lax.dot_general(q_ref[...], k_ref[...], NT_DIMS, preferred_element_type=F32)
    m = jnp.max(s, axis=-1, keepdims=True)
    p = jnp.exp(s - m)
    l = jnp.sum(p, axis=-1, keepdims=True)
    o = jnp.dot(p.astype(BF16), v_ref[...], preferred_element_type=F32) / l
    o_ref[...] = o.astype(o_ref.dtype)


def ctx_attention(qa, q_off, ka, k_off, va, v_off, n_q_heads, group):
    b, _, lc, d = qa.shape
    return pl.pallas_call(
        _ctx_attn_kernel, grid=(b, n_q_heads),
        in_specs=[pl.BlockSpec((None, None, lc, d), lambda bi, h: (bi, q_off + h, 0, 0)),
                  pl.BlockSpec((None, None, lc, d), lambda bi, h: (bi, k_off + h // group, 0, 0)),
                  pl.BlockSpec((None, None, lc, d), lambda bi, h: (bi, v_off + h // group, 0, 0))],
        out_specs=pl.BlockSpec((None, lc, d), lambda bi, h: (bi, 0, h)),
        out_shape=jax.ShapeDtypeStruct((b, lc, n_q_heads * d), BF16),
        compiler_params=_cparams("parallel", "parallel"),
        name="ctx_attention",
    )(qa, ka, va)


NA_TILE_ROWS = 4
NA_KEY_ROWS = 12
NA_HEADS_PER_STEP = 2


def _na_geometry(n_rows):
    wr = min(NA_WIN_R, n_rows)
    nt = n_rows // NA_TILE_ROWS
    sigs = []
    for t in range(nt):
        rt = t * NA_TILE_ROWS
        w0 = int(np.clip(rt - wr // 2, 0, n_rows - NA_KEY_ROWS))
        r = rt + np.arange(NA_TILE_ROWS)
        r0 = np.clip(r - wr // 2, 0, n_rows - wr)
        sigs.append((w0 - rt, tuple((r0 - w0).tolist())))
    classes = (sigs[0], sigs[1], sigs[-1])
    for t, s in enumerate(sigs):
        assert s == classes[0 if t == 0 else (2 if t == nt - 1 else 1)], "tile does not match its bias class"
    return wr, nt, classes


def _na_bias_tables(rpb, n_rows):
    wr, _, classes = _na_geometry(n_rows)
    cols = np.arange(GRID_W)
    c0 = np.clip(cols - NA_WIN_C // 2, 0, GRID_W - NA_WIN_C)
    col_ok = (cols[None, :] >= c0[:, None]) & (cols[None, :] < c0[:, None] + NA_WIN_C)
    col_idx = np.clip(cols[None, :] - cols[:, None] + NA_WIN_C - 1, 0, 2 * NA_WIN_C - 2)
    col_sel = (col_idx[None] == np.arange(2 * NA_WIN_C - 1)[:, None, None]).astype(np.float32)
    row_sel, oks = [], []
    for off, rel_r0 in classes:
        qi = np.arange(NA_TILE_ROWS)[:, None]
        kw = np.arange(NA_KEY_ROWS)[None, :]
        r0 = np.asarray(rel_r0)[:, None]
        row_ok = (kw >= r0) & (kw < r0 + wr)
        row_idx = np.clip(kw + off - qi + NA_WIN_R - 1, 0, 2 * NA_WIN_R - 2)
        row_sel.append((row_idx[None] == np.arange(2 * NA_WIN_R - 1)[:, None, None]).astype(np.float32))
        oks.append(row_ok[:, None, :, None] & col_ok[None, :, None, :])
    vals = jnp.einsum("hab,saqk,bcd->hsqckd", rpb.astype(F32), jnp.asarray(np.stack(row_sel)),
                      jnp.asarray(col_sel), precision=lax.Precision.HIGHEST)
    tab = jnp.where(jnp.asarray(np.stack(oks))[None], vals, NEG_BIG)
    return tab.reshape(rpb.shape[0], 3, NA_TILE_ROWS * GRID_W, NA_KEY_ROWS * GRID_W)


def _na_kernel(q_ref, k_ref, v_ref, kc_ref, vc_ref, bias_ref, o_ref, *, n_rows, win_half):
    t = pl.program_id(2)
    w0 = jnp.clip(t * NA_TILE_ROWS - win_half, 0, n_rows - NA_KEY_ROWS)
    start = pl.multiple_of(w0 * GRID_W, GRID_W)
    outs = []
    for hh in range(q_ref.shape[0]):
        kwin = k_ref[hh, pl.ds(start, NA_KEY_ROWS * GRID_W), :]
        vwin = v_ref[hh, pl.ds(start, NA_KEY_ROWS * GRID_W), :]
        q = q_ref[hh]
        s_lat = lax.dot_general(q, kwin, NT_DIMS, preferred_element_type=F32) + bias_ref[hh]
        s_ctx = lax.dot_general(q, kc_ref[hh], NT_DIMS, preferred_element_type=F32)
        m = jnp.maximum(jnp.max(s_lat, axis=-1, keepdims=True), jnp.max(s_ctx, axis=-1, keepdims=True))
        p_lat = jnp.exp(s_lat - m)
        p_ctx = jnp.exp(s_ctx - m)
        l = jnp.sum(p_lat, axis=-1, keepdims=True) + jnp.sum(p_ctx, axis=-1, keepdims=True)
        o = (jnp.dot(p_lat.astype(BF16), vwin, preferred_element_type=F32)
             + jnp.dot(p_ctx.astype(BF16), vc_ref[hh], preferred_element_type=F32)) / l
        outs.append(o.astype(o_ref.dtype))
    o_ref[...] = jnp.concatenate(outs, axis=1)


def na_attention(pl_heads, pc_heads, rpb):
    b, h3, l, d = pl_heads.shape
    lc = pc_heads.shape[2]
    h = h3 // 3
    n_rows = l // GRID_W
    wr, nt, _ = _na_geometry(n_rows)
    bias = _na_bias_tables(rpb, n_rows)
    tq = NA_TILE_ROWS * GRID_W
    kw = NA_KEY_ROWS * GRID_W
    kern = functools.partial(_na_kernel, n_rows=n_rows, win_half=wr // 2)

    def cls(t):
        return jnp.where(t == 0, 0, jnp.where(t == nt - 1, 2, 1))

    hs = NA_HEADS_PER_STEP
    assert h % hs == 0
    hb = h // hs
    return pl.pallas_call(
        kern, grid=(b, hb, nt),
        in_specs=[pl.BlockSpec((None, hs, tq, d), lambda bi, hi, t: (bi, hi, t, 0)),
                  pl.BlockSpec((None, hs, l, d), lambda bi, hi, t: (bi, hb + hi, 0, 0)),
                  pl.BlockSpec((None, hs, l, d), lambda bi, hi, t: (bi, 2 * hb + hi, 0, 0)),
                  pl.BlockSpec((None, hs, lc, d), lambda bi, hi, t: (bi, hb + hi, 0, 0)),
                  pl.BlockSpec((None, hs, lc, d), lambda bi, hi, t: (bi, 2 * hb + hi, 0, 0)),
                  pl.BlockSpec((hs, None, tq, kw), lambda bi, hi, t: (hi, cls(t), 0, 0))],
        out_specs=pl.BlockSpec((None, tq, hs * d), lambda bi, hi, t: (bi, t, hi)),
        out_shape=jax.ShapeDtypeStruct((b, l, h * d), BF16),
        compiler_params=_cparams("parallel", "parallel", "parallel"),
        name="na_attention",
    )(pl_heads, pl_heads, pl_heads, pc_heads, pc_heads, bias)


def _ret_kernel(*refs, reverse, rope, finalize, n_heads):
    it = iter(refs)
    lg_ref, p_ref = next(it), next(it)
    cos_ref, sin_ref = (next(it), next(it)) if rope else (None, None)
    s0_ref = next(it)
    oprev_ref, gnw_ref = (next(it), next(it)) if finalize else (None, None)
    o_ref, sfin_ref, state_ref = next(it), next(it), next(it)

    i = pl.program_id(1)
    n_steps = pl.num_programs(1)

    @pl.when(i == 0)
    def _():
        state_ref[...] = s0_ref[...]

    c = RET_CHUNK
    ts = p_ref.shape[0]
    nc = ts // c
    dq = n_heads * RET_DK
    dv = n_heads * RET_DV
    x = p_ref[...]
    q = x[:, :dq].astype(F32)
    k = x[:, dq:2 * dq].astype(F32)
    v = x[:, 2 * dq:2 * dq + dv]
    if rope:
        q = q * cos_ref[...] + _swap32(q) * sin_ref[...]
        k = k * cos_ref[...] + _swap32(k) * sin_ref[...]
    jr = lax.broadcasted_iota(jnp.int32, (c, c), 0)
    jc = lax.broadcasted_iota(jnp.int32, (c, c), 1)
    rel = ((jc - jr) if reverse else (jr - jc)).astype(F32)
    jcol = lax.broadcasted_iota(jnp.int32, (c, 1), 0).astype(F32)
    order = range(nc - 1, -1, -1) if reverse else range(nc)
    blocks = [[None] * n_heads for _ in range(nc)]
    for h in range(n_heads):
        g = lg_ref[h]
        intra = jnp.where(rel >= 0.0, jnp.exp(g * jnp.maximum(rel, 0.0)), 0.0)
        if reverse:
            cross_f = jnp.exp(g * (float(c) - jcol))
            k_dec = jnp.exp(g * jcol)
        else:
            cross_f = jnp.exp(g * (jcol + 1.0))
            k_dec = jnp.exp(g * (float(c - 1) - jcol))
        chunk_decay = jnp.exp(g * float(c))
        state = state_ref[h]
        for ci in order:
            rows = slice(ci * c, (ci + 1) * c)
            qh = q[rows, h * RET_DK:(h + 1) * RET_DK].astype(BF16)
            kf = k[rows, h * RET_DK:(h + 1) * RET_DK]
            vh = v[rows, h * RET_DV:(h + 1) * RET_DV]
            scores = lax.dot_general(qh, kf.astype(BF16), NT_DIMS, preferred_element_type=F32) * intra
            inner = jnp.dot(scores.astype(BF16), vh, preferred_element_type=F32)
            cross = jnp.dot(qh, state.astype(BF16), preferred_element_type=F32) * cross_f
            blocks[ci][h] = inner + cross
            kv = lax.dot_general((kf * k_dec).astype(BF16), vh, TN_DIMS, preferred_element_type=F32)
            state = chunk_decay * state + kv
        state_ref[h] = state
    o = jnp.concatenate([jnp.concatenate(blocks[ci], axis=1) for ci in range(nc)], axis=0)
    if finalize:
        o = o + oprev_ref[...]
        gate = x[:, 2 * dq + dv:2 * dq + 2 * dv].astype(F32)
        normed = []
        for h in range(n_heads):
            oh = o[:, h * RET_DV:(h + 1) * RET_DV]
            normed.append(oh * lax.rsqrt(jnp.mean(oh * oh, axis=-1, keepdims=True) + EPS))
        o = jnp.concatenate(normed, axis=1) * gnw_ref[...] * _silu(gate)
    o_ref[...] = o.astype(o_ref.dtype)

    @pl.when(i == n_steps - 1)
    def _():
        sfin_ref[...] = state_ref[...]


def retention_pass(p, log_g, s0, reverse, rope_tables=None, o_prev=None, gn_w=None):
    b, l, w = p.shape
    n_heads = w // (2 * RET_DK + 2 * RET_DV)
    dq, dv = n_heads * RET_DK, n_heads * RET_DV
    ts = _pick_tile(l, 512, RET_CHUNK)
    n_steps = l // ts
    finalize = o_prev is not None
    rope = rope_tables is not None

    def tile(i):
        return (n_steps - 1 - i) if reverse else i

    in_specs = [pl.BlockSpec(memory_space=pltpu.SMEM),
                pl.BlockSpec((None, ts, w), lambda bi, i: (bi, tile(i), 0))]
    args = [log_g, p]
    if rope:
        in_specs += [pl.BlockSpec((ts, dq), lambda bi, i: (tile(i), 0))] * 2
        args += list(rope_tables)
    in_specs.append(pl.BlockSpec((None, n_heads, RET_DK, RET_DV), lambda bi, i: (bi, 0, 0, 0)))
    args.append(s0)
    if finalize:
        in_specs += [pl.BlockSpec((None, ts, dv), lambda bi, i: (bi, tile(i), 0)),
                     pl.BlockSpec((1, dv), lambda bi, i: (0, 0))]
        args += [o_prev, gn_w.reshape(1, dv)]
    kern = functools.partial(_ret_kernel, reverse=reverse, rope=rope, finalize=finalize, n_heads=n_heads)
    return pl.pallas_call(
        kern, grid=(b, n_steps),
        in_specs=in_specs,
        out_specs=[pl.BlockSpec((None, ts, dv), lambda bi, i: (bi, tile(i), 0)),
                   pl.BlockSpec((None, n_heads, RET_DK, RET_DV), lambda bi, i: (bi, 0, 0, 0))],
        out_shape=[jax.ShapeDtypeStruct((b, l, dv), BF16 if finalize else F32),
                   jax.ShapeDtypeStruct((b, n_heads, RET_DK, RET_DV), F32)],
        scratch_shapes=[pltpu.VMEM((n_heads, RET_DK, RET_DV), F32)],
        compiler_params=_cparams("parallel", "arbitrary"),
        name="retention_pass",
    )(*args)


def _ret_rope_tables(l, n_heads):
    half = RET_DK // 2
    inv = RET_ROPE_BASE ** (-jnp.linspace(0.0, 1.0, half, dtype=F32))
    ang = jnp.arange(l, dtype=F32)[:, None] * inv
    cos, sin = jnp.cos(ang), jnp.sin(ang)
    return (jnp.tile(jnp.concatenate([cos, cos], -1), (1, n_heads)),
            jnp.tile(jnp.concatenate([-sin, sin], -1), (1, n_heads)))


def retention_mixer(p_l, p_c, log_decay, gn_w, tables, with_ctx_out):
    b = p_l.shape[0]
    n_heads = p_l.shape[2] // (2 * RET_DK + 2 * RET_DV)
    log_g = -jnp.abs(log_decay.astype(F32))
    s0 = jnp.zeros((b, n_heads, RET_DK, RET_DV), F32)
    o_cf, s_fwd = retention_pass(p_c, log_g[0], s0, False)
    y_c, s_bwd = retention_pass(p_c, log_g[1], s0, True, o_prev=o_cf, gn_w=gn_w)
    o_lf, _ = retention_pass(p_l, log_g[0], s_fwd, False, rope_tables=tables)
    y_l, _ = retention_pass(p_l, log_g[1], s_bwd, True, rope_tables=tables, o_prev=o_lf, gn_w=gn_w)
    return y_l, (y_c if with_ctx_out else None)


def _gqa_prep_kernel(*refs, rope, n_q, n_kv, q_scale):
    if rope:
        p_ref, cos_ref, sin_ref, qw_ref, kw_ref, q_out, k_out, v_out = refs
    else:
        p_ref, qw_ref, kw_ref, q_out, k_out, v_out = refs
    x = p_ref[...]
    d = HEAD_DIM

    def norm_rope(xh, w):
        xh = xh.astype(F32)
        y = xh * lax.rsqrt(jnp.mean(xh * xh, axis=-1, keepdims=True) + EPS) * w
        if rope:
            y = y * cos_ref[...] + _swap32(y) * sin_ref[...]
        return y

    for h in range(n_q):
        q_out[h] = (norm_rope(x[:, h * d:(h + 1) * d], qw_ref[...]) * q_scale).astype(q_out.dtype)
    for h in range(n_kv):
        k_out[h] = norm_rope(x[:, (n_q + h) * d:(n_q + h + 1) * d], kw_ref[...]).astype(k_out.dtype)
        lane = lax.broadcasted_iota(jnp.int32, (x.shape[0], d), 1)
        ones_col = jnp.where(lane == 0, 1.0, 0.0).astype(v_out.dtype)
        v_out[h] = jnp.concatenate([x[:, (n_q + n_kv + h) * d:(n_q + n_kv + h + 1) * d], ones_col], axis=1)


def gqa_prep(p, qn_w, kn_w, n_q, n_kv, q_scale, rope_tables=None):
    b, l, w = p.shape
    d = HEAD_DIM
    t = _pick_tile(l, 512, 16)
    rope = rope_tables is not None
    in_specs = [pl.BlockSpec((None, t, w), lambda bi, i: (bi, i, 0))]
    args = [p]
    if rope:
        in_specs += [pl.BlockSpec((t, d), lambda bi, i: (i, 0))] * 2
        args += list(rope_tables)
    in_specs += [pl.BlockSpec((1, d), lambda bi, i: (0, 0))] * 2
    args += [qn_w.reshape(1, d), kn_w.reshape(1, d)]
    kern = functools.partial(_gqa_prep_kernel, rope=rope, n_q=n_q, n_kv=n_kv, q_scale=q_scale)
    return pl.pallas_call(
        kern, grid=(b, l // t),
        in_specs=in_specs,
        out_specs=[pl.BlockSpec((None, n_q, t, d), lambda bi, i: (bi, 0, i, 0)),
                   pl.BlockSpec((None, n_kv, t, d), lambda bi, i: (bi, 0, i, 0)),
                   pl.BlockSpec((None, n_kv, t, 2 * d), lambda bi, i: (bi, 0, i, 0))],
        out_shape=[jax.ShapeDtypeStruct((b, n_q, l, d), BF16),
                   jax.ShapeDtypeStruct((b, n_kv, l, d), BF16),
                   jax.ShapeDtypeStruct((b, n_kv, l, 2 * d), BF16)],
        compiler_params=_cparams("parallel", "parallel"),
        name="gqa_prep",
    )(*args)


def _axial_rope_tables(l):
    nf = HEAD_DIM // 4
    t = jnp.arange(l)
    inv = ROPE_BASE ** (-jnp.arange(nf, dtype=F32) / nf)
    ang_r = (t // GRID_W).astype(F32)[:, None] * inv
    ang_c = (t % GRID_W).astype(F32)[:, None] * inv
    cr, sr, cc, sc = jnp.cos(ang_r), jnp.sin(ang_r), jnp.cos(ang_c), jnp.sin(ang_c)
    return (jnp.concatenate([cr, cr, cc, cc], -1), jnp.concatenate([-sr, sr, -sc, sc], -1))


FLASH_SUB_ROWS = 256


def _flash_kernel(q_ref, k_ref, v_ref, o_ref, m_ref, acc_ref):
    j = pl.program_id(3)
    g, tq, d = q_ref.shape

    @pl.when(j == 0)
    def _():
        m_ref[...] = jnp.full(m_ref.shape, NEG_BIG, F32)
        acc_ref[...] = jnp.zeros(acc_ref.shape, F32)

    q = q_ref[...].reshape(g * tq, d)
    k = k_ref[...]
    v = v_ref[...]
    n_sub = (g * tq) // FLASH_SUB_ROWS
    s, p, alpha = [None] * n_sub, [None] * n_sub, [None] * n_sub
    for t in range(n_sub + 2):
        if t < n_sub:
            s[t] = lax.dot_general(q[t * FLASH_SUB_ROWS:(t + 1) * FLASH_SUB_ROWS], k, NT_DIMS,
                                   preferred_element_type=F32)
        u = t - 1
        if 0 <= u < n_sub:
            rows = slice(u * FLASH_SUB_ROWS, (u + 1) * FLASH_SUB_ROWS)
            m_prev = m_ref[rows]
            m_new = jnp.maximum(m_prev, jnp.max(s[u], axis=-1, keepdims=True))
            alpha[u] = jnp.exp2(m_prev - m_new)
            m_ref[rows] = m_new
            p[u] = jnp.exp2((s[u] - m_new).astype(BF16))
            s[u] = None
        w = t - 2
        if 0 <= w < n_sub:
            rows = slice(w * FLASH_SUB_ROWS, (w + 1) * FLASH_SUB_ROWS)
            acc_ref[rows] = alpha[w] * acc_ref[rows] + jnp.dot(p[w], v, preferred_element_type=F32)
            p[w] = None

    @pl.when(j == pl.num_programs(3) - 1)
    def _():
        acc = acc_ref[...]
        o = acc[:, :d] / acc[:, d:d + 1]
        o_ref[...] = jnp.concatenate([o[h * tq:(h + 1) * tq] for h in range(g)], axis=1).astype(o_ref.dtype)


def flash_gqa(q, k, v):
    b, hq, l, d = q.shape
    hkv, lk = k.shape[1], k.shape[2]
    g = hq // hkv
    tq = _pick_tile(l, 1024, 16)
    tk = _pick_tile(lk, 3328, LANES)
    assert (g * tq) % FLASH_SUB_ROWS == 0
    return pl.pallas_call(
        _flash_kernel, grid=(b, hkv, l // tq, lk // tk),
        in_specs=[pl.BlockSpec((None, g, tq, d), lambda bi, h, i, j: (bi, h, i, 0)),
                  pl.BlockSpec((None, None, tk, d), lambda bi, h, i, j: (bi, h, j, 0)),
                  pl.BlockSpec((None, None, tk, 2 * d), lambda bi, h, i, j: (bi, h, j, 0))],
        out_specs=pl.BlockSpec((None, tq, g * d), lambda bi, h, i, j: (bi, i, h)),
        out_shape=jax.ShapeDtypeStruct((b, l, hq * d), BF16),
        scratch_shapes=[pltpu.VMEM((g * tq, 1), F32), pltpu.VMEM((g * tq, 2 * d), F32)],
        compiler_params=_cparams("parallel", "parallel", "parallel", "arbitrary"),
        name="flash_gqa",
    )(q, k, v)


def gqa_mixer(p_l, p_c, qn_w, kn_w, tables, with_ctx_out):
    n_q = p_l.shape[2] // (2 * HEAD_DIM)
    n_kv = n_q // 2
    scale = HEAD_DIM ** -0.5
    q_l, k_l, v_l = gqa_prep(p_l, qn_w, kn_w, n_q, n_kv, scale * math.log2(math.e), rope_tables=tables)
    q_c, k_c, v_c = gqa_prep(p_c, qn_w, kn_w, n_q, n_kv, scale)
    y_l = flash_gqa(q_l, jnp.concatenate([k_l, k_c], axis=2), jnp.concatenate([v_l, v_c], axis=2))
    y_c = ctx_attention(q_c, 0, k_c, 0, v_c, 0, n_q, n_q // n_kv) if with_ctx_out else None
    return y_l, y_c


W_IN_GATE_ALIGN = 2048


def _prep_w_in(w_in, d_model):
    mix_w = d_model // N_BRANCH
    h = mix_w // HEAD_DIM
    sizes = (3 * mix_w, 3 * h * HEAD_DIM, 2 * h * RET_DK + 2 * h * RET_DV,
             (h + 2 * max(h // 2, 1)) * HEAD_DIM, N_BRANCH * d_model)
    scale = np.ones((sum(sizes),), np.float32)
    scale[sizes[0]:sizes[0] + h * HEAD_DIM] = HEAD_DIM ** -0.5
    k0 = sizes[0] + sizes[1] + h * RET_DK
    scale[k0:k0 + h * RET_DK] = RET_DK ** -0.5
    head = sum(sizes[:4])
    pad = -head % W_IN_GATE_ALIGN
    w = w_in * jnp.asarray(scale)
    w = jnp.concatenate([w[..., :head], jnp.zeros(w.shape[:2] + (pad,), w.dtype), w[..., head:]], axis=-1)
    offs = [0, sizes[0], sizes[0] + sizes[1], sizes[0] + sizes[1] + sizes[2], head + pad]
    return w.astype(BF16), list(zip(offs, sizes))


def kernel(x, c, ctx, c_ctx, ada_w, ada_b, norm1_w, norm2_w, w_in, hy_conv_w, hy_conv_b, hy_ffn_w1, hy_ffn_b1,
           hy_ffn_w2, hy_ffn_b2, hy_ffn_w3, hy_sin_freq, hy_bias, na_rpb, ret_log_decay, ret_gn_w, gqa_q_norm_w,
           gqa_k_norm_w, w_branch, w_out, ffn_w13, ffn_w2, final_norm_w):
    b, l, d = x.shape
    lc = ctx.shape[1]
    depth = ada_w.shape[0]
    ffn_hidden = ffn_w2.shape[1]
    n_ret_heads = (d // N_BRANCH) // RET_DV

    cs = jnp.zeros((8, d), F32).at[:b].set(c).at[b].set(c_ctx)
    mods = ada_mod(cs, ada_w, ada_b)
    ctx_row = b
    mats = hyena_mats(l, lc)
    ret_tables = _ret_rope_tables(l, n_ret_heads)
    gqa_tables = _axial_rope_tables(l)

    w_in_b, groups = _prep_w_in(w_in, d)
    wb = w_branch.astype(BF16)
    w_out_b = w_out.astype(BF16)
    w13_b = ffn_w13.astype(BF16)
    w2_b = ffn_w2.astype(BF16)

    x_l = x.reshape(b * l, d)
    x_c = ctx.reshape(b * lc, d)
    for layer in range(depth):
        last = layer == depth - 1
        mod = mods[layer]
        w_hy, w_na, w_rt, w_gq, w_gate = (WCols(w_in_b, layer, off, n) for off, n in groups)
        hy_params = (hy_conv_w[layer], hy_conv_b[layer], hy_ffn_w1[layer], hy_ffn_b1[layer], hy_ffn_w2[layer],
                     hy_ffn_b2[layer], hy_ffn_w3[layer], hy_sin_freq[layer], hy_bias[layer])
        wo = WCols(w_out_b, layer, 0, d)
        w1 = WCols(w13_b, layer, 0, ffn_hidden)
        w3 = WCols(w13_b, layer, ffn_hidden, ffn_hidden)
        w2 = WCols(w2_b, layer, 0, d)

        h_l = norm_mod(x_l, norm1_w[layer], mod, 0, 1, l, 0)
        h_c = norm_mod(x_c, norm1_w[layer], mod, 0, 1, None, ctx_row)

        na_l = matmul_heads(h_l, w_na, b)
        na_c = matmul_heads(h_c, w_na, b)
        rt_l = matmul(h_l, w_rt).reshape(b, l, -1)
        rt_c = matmul(h_c, w_rt).reshape(b, lc, -1)
        gq_l = matmul(h_l, w_gq).reshape(b, l, -1)
        gq_c = matmul(h_c, w_gq).reshape(b, lc, -1)
        hy_l = matmul(h_l, w_hy).reshape(b, l, -1)
        gate_l = matmul(h_l, w_gate)

        y_hy_l = hyena_mixer(hy_l, *hy_params, mats)
        y_na_l = na_attention(na_l, na_c, na_rpb[layer])
        y_rt_l, y_rt_c = retention_mixer(rt_l, rt_c, ret_log_decay[layer], ret_gn_w[layer], ret_tables, not last)
        y_gq_l, y_gq_c = gqa_mixer(gq_l, gq_c, gqa_q_norm_w[layer], gqa_k_norm_w[layer], gqa_tables, not last)

        m_l = merge_branches([y.reshape(b * l, -1) for y in (y_hy_l, y_na_l, y_rt_l, y_gq_l)], gate_l, wb, layer)
        x_l = matmul_residual(m_l, wo, x_l, mod, 2, l, 0)
        h2 = norm_mod(x_l, norm2_w[layer], mod, 3, 4, l, 0)
        x_l = matmul_residual(matmul_swiglu(h2, w1, w3), w2, x_l, mod, 5, l, 0)

        if not last:
            n_na = na_c.shape[1] // 3
            hy_c = matmul(h_c, w_hy).reshape(b, lc, -1)
            gate_c = matmul(h_c, w_gate)
            y_hy_c = hyena_mixer(hy_c, *hy_params, mats)
            y_na_c = ctx_attention(na_c, 0, na_c, n_na, na_c, 2 * n_na, n_na, 1)
            m_c = merge_branches([y.reshape(b * lc, -1) for y in (y_hy_c, y_na_c, y_rt_c, y_gq_c)], gate_c, wb,
                                 layer)
            x_c = matmul_residual(m_c, wo, x_c, mod, 2, None, ctx_row)
            h2c = norm_mod(x_c, norm2_w[layer], mod, 3, 4, None, ctx_row)
            x_c = matmul_residual(matmul_swiglu(h2c, w1, w3), w2, x_c, mod, 5, None, ctx_row)

    return final_norm(x_l, final_norm_w).reshape(b, l, d)
```

```python
import functools
import math

import numpy as np
import jax
import jax.numpy as jnp
from jax import lax
from jax.experimental import pallas as pl
from jax.experimental.pallas import tpu as pltpu

F32 = jnp.float32
BF16 = jnp.bfloat16

EPS = 1e-6
GRID_W = 64
HEAD_DIM = 128
N_BRANCH = 4
ROPE_BASE = 10000.0
HY_BANDS = 16
HY_DECAY_TARGET = 1e-2
HY_FAST_PCT = 0.3
HY_SLOW_PCT = 1.5
NA_WIN_R = 8
NA_WIN_C = 16
RET_DK = 64
RET_DV = 128
RET_CHUNK = 128
RET_ROPE_BASE = 10000.0
NEG_BIG = -1e30

LANES = 128
FFT_N2 = 128
VMEM_LIMIT = 56 * 1024 * 1024

NT_DIMS = (((1,), (1,)), ((), ()))
TN_DIMS = (((0,), (0,)), ((), ()))


def _cparams(*sem):
    return pltpu.CompilerParams(dimension_semantics=sem, vmem_limit_bytes=VMEM_LIMIT)


def _pick_tile(n, cap, mult):
    best = None
    for t in range(mult, min(n, cap) + 1, mult):
        if n % t == 0:
            best = t
    assert best is not None, (n, cap, mult)
    return best


def _swap32(x):
    n = x.shape[-1]
    lane = lax.broadcasted_iota(jnp.int32, x.shape, x.ndim - 1)
    up = pltpu.roll(x, n - 32, x.ndim - 1)
    down = pltpu.roll(x, 32, x.ndim - 1)
    return jnp.where((lane % 64) < 32, up, down)


def _silu(x):
    return x * jax.nn.sigmoid(x)


def _ada_kernel(c_ref, w_ref, b_ref, o_ref):
    a = _silu(c_ref[...]).astype(BF16)
    o_ref[...] = jnp.dot(a, w_ref[...].astype(BF16), preferred_element_type=F32) + b_ref[...]


def ada_mod(cs, ada_w, ada_b):
    depth, d, n = ada_w.shape
    tn = _pick_tile(n, 1536, LANES)
    return pl.pallas_call(
        _ada_kernel,
        grid=(depth, n // tn),
        in_specs=[pl.BlockSpec((8, d), lambda l, j: (0, 0)),
                  pl.BlockSpec((None, d, tn), lambda l, j: (l, 0, j)),
                  pl.BlockSpec((None, 1, tn), lambda l, j: (l, 0, j))],
        out_specs=pl.BlockSpec((None, 8, tn), lambda l, j: (l, 0, j)),
        out_shape=jax.ShapeDtypeStruct((depth, 8, n), F32),
        compiler_params=_cparams("parallel", "parallel"),
        name="ada_mod",
    )(cs, ada_w, ada_b.reshape(depth, 1, n))


def _mod_row(row_base, tiles_per_batch, axis):
    if tiles_per_batch is None:
        return row_base
    return row_base + pl.program_id(axis) // tiles_per_batch


def _norm_mod_kernel(x_ref, w_ref, sh_ref, sc_ref, o_ref, *, row_base, tiles_per_batch):
    x = x_ref[...]
    y = x * lax.rsqrt(jnp.mean(x * x, axis=-1, keepdims=True) + EPS) * w_ref[...]
    row = _mod_row(row_base, tiles_per_batch, 0)
    sh = sh_ref[pl.ds(row, 1), :]
    sc = sc_ref[pl.ds(row, 1), :]
    o_ref[...] = (y * (1.0 + sc) + sh).astype(o_ref.dtype)


def _norm_kernel(x_ref, w_ref, o_ref):
    x = x_ref[...]
    y = x * lax.rsqrt(jnp.mean(x * x, axis=-1, keepdims=True) + EPS) * w_ref[...]
    o_ref[...] = y.astype(o_ref.dtype)


def norm_mod(x, w, mod, sh_chunk, sc_chunk, rows_per_batch, row_base):
    m, d = x.shape
    tm = _pick_tile(m if rows_per_batch is None else rows_per_batch, 512, 8)
    tpb = None if rows_per_batch is None else rows_per_batch // tm
    kern = functools.partial(_norm_mod_kernel, row_base=row_base, tiles_per_batch=tpb)
    return pl.pallas_call(
        kern, grid=(m // tm,),
        in_specs=[pl.BlockSpec((tm, d), lambda i: (i, 0)),
                  pl.BlockSpec((1, d), lambda i: (0, 0)),
                  pl.BlockSpec((8, d), lambda i: (0, sh_chunk)),
                  pl.BlockSpec((8, d), lambda i: (0, sc_chunk))],
        out_specs=pl.BlockSpec((tm, d), lambda i: (i, 0)),
        out_shape=jax.ShapeDtypeStruct((m, d), BF16),
        compiler_params=_cparams("parallel"),
        name="norm_mod",
    )(x, w.reshape(1, d), mod, mod)


def final_norm(x, w):
    m, d = x.shape
    tm = _pick_tile(m, 512, 8)
    return pl.pallas_call(
        _norm_kernel, grid=(m // tm,),
        in_specs=[pl.BlockSpec((tm, d), lambda i: (i, 0)), pl.BlockSpec((1, d), lambda i: (0, 0))],
        out_specs=pl.BlockSpec((tm, d), lambda i: (i, 0)),
        out_shape=jax.ShapeDtypeStruct((m, d), F32),
        compiler_params=_cparams("parallel"),
        name="final_norm",
    )(x, w.reshape(1, d))


def _mm_kernel(a_ref, w_ref, o_ref):
    o_ref[...] = jnp.dot(a_ref[...], w_ref[...], preferred_element_type=F32).astype(o_ref.dtype)


def _mm_heads_kernel(a_ref, w_ref, o_ref):
    acc = jnp.dot(a_ref[...], w_ref[...], preferred_element_type=F32)
    for h in range(o_ref.shape[0]):
        o_ref[h] = acc[:, h * HEAD_DIM:(h + 1) * HEAD_DIM].astype(o_ref.dtype)


def _mm_swiglu_kernel(a_ref, w1_ref, w3_ref, o_ref):
    a = a_ref[...]
    u = jnp.dot(a, w1_ref[...], preferred_element_type=F32)
    g = jnp.dot(a, w3_ref[...], preferred_element_type=F32)
    o_ref[...] = (_silu(u) * g).astype(o_ref.dtype)


def _mm_residual_norm_kernel(*refs, row_base, tiles_per_batch, n_k, modulate, write_x):
    it = iter(refs)
    a_ref, w_ref, x_ref, g_ref, nw_ref = next(it), next(it), next(it), next(it), next(it)
    sh_ref, sc_ref = (next(it), next(it)) if modulate else (None, None)
    ox_ref = next(it) if write_x else None
    oh_ref = next(it)
    acc_ref = next(it) if n_k > 1 else None
    kk = pl.program_id(1)
    part = jnp.dot(a_ref[...], w_ref[...], preferred_element_type=F32)

    def finish(acc):
        row = _mod_row(row_base, tiles_per_batch, 0)
        xn = x_ref[...] + g_ref[pl.ds(row, 1), :] * acc
        if write_x:
            ox_ref[...] = xn
        y = xn * lax.rsqrt(jnp.mean(xn * xn, axis=-1, keepdims=True) + EPS) * nw_ref[...]
        if modulate:
            y = y * (1.0 + sc_ref[pl.ds(row, 1), :]) + sh_ref[pl.ds(row, 1), :]
        oh_ref[...] = y.astype(oh_ref.dtype)

    if n_k == 1:
        finish(part)
    else:
        @pl.when(kk == 0)
        def _():
            acc_ref[...] = part

        @pl.when(jnp.logical_and(kk > 0, kk < n_k - 1))
        def _():
            acc_ref[...] += part

        @pl.when(kk == n_k - 1)
        def _():
            finish(acc_ref[...] + part)


class WCols:
    def __init__(self, arr, layer, off, n):
        self.arr, self.layer, self.off, self.n = arr, layer, off, n

    def tile(self, cap):
        tn = max(t for t in range(LANES, min(self.n, cap) + 1, LANES) if self.n % t == 0 and self.off % t == 0)
        return tn

    def spec(self, tn):
        layer, base, k = self.layer, self.off // tn, self.arr.shape[1]
        return pl.BlockSpec((None, k, tn), lambda j, i: (layer, 0, base + j))


def matmul(a, w, out_dtype=BF16):
    m, k = a.shape
    n = w.n
    tm = _pick_tile(m, 512, 16)
    tn = w.tile(2048)
    return pl.pallas_call(
        _mm_kernel, grid=(n // tn, m // tm),
        in_specs=[pl.BlockSpec((tm, k), lambda j, i: (i, 0)), w.spec(tn)],
        out_specs=pl.BlockSpec((tm, tn), lambda j, i: (i, j)),
        out_shape=jax.ShapeDtypeStruct((m, n), out_dtype),
        compiler_params=_cparams("parallel", "parallel"),
        name="matmul",
    )(a, w.arr)


def matmul_heads(a, w, batch):
    m, k = a.shape
    n = w.n
    lb = m // batch
    tm = _pick_tile(lb, 512, 16)
    tn = w.tile(1536)
    tpb = lb // tm
    nh = tn // HEAD_DIM
    return pl.pallas_call(
        _mm_heads_kernel, grid=(n // tn, m // tm),
        in_specs=[pl.BlockSpec((tm, k), lambda j, i: (i, 0)), w.spec(tn)],
        out_specs=pl.BlockSpec((None, nh, tm, HEAD_DIM), lambda j, i: (i // tpb, j, i % tpb, 0)),
        out_shape=jax.ShapeDtypeStruct((batch, n // HEAD_DIM, lb, HEAD_DIM), BF16),
        compiler_params=_cparams("parallel", "parallel"),
        name="matmul_heads",
    )(a, w.arr)


def matmul_swiglu(a, w1, w3):
    m, k = a.shape
    n = w1.n
    tm = _pick_tile(m, 1024, 16)
    tn = min(w1.tile(704), w3.tile(704))
    return pl.pallas_call(
        _mm_swiglu_kernel, grid=(n // tn, m // tm),
        in_specs=[pl.BlockSpec((tm, k), lambda j, i: (i, 0)), w1.spec(tn), w3.spec(tn)],
        out_specs=pl.BlockSpec((tm, tn), lambda j, i: (i, j)),
        out_shape=jax.ShapeDtypeStruct((m, n), BF16),
        compiler_params=_cparams("parallel", "parallel"),
        name="matmul_swiglu",
    )(a, w1.arr, w3.arr)


RESIDUAL_K_TILE = 2048


def matmul_residual_norm(a, w, x, mod, gate_chunk, norm_w, next_mod, sh_chunk, sc_chunk, rows_per_batch, row_base,
                         write_x=True, out_dtype=BF16):
    m, k = a.shape
    n = w.n
    assert w.off == 0 and n == w.arr.shape[2]
    tm = _pick_tile(m if rows_per_batch is None else rows_per_batch, 512, 16)
    tk = max(t for t in range(LANES, min(k, RESIDUAL_K_TILE) + 1, LANES) if k % t == 0)
    n_k = k // tk
    tpb = None if rows_per_batch is None else rows_per_batch // tm
    modulate = next_mod is not None
    kern = functools.partial(_mm_residual_norm_kernel, row_base=row_base, tiles_per_batch=tpb, n_k=n_k,
                             modulate=modulate, write_x=write_x)
    layer = w.layer
    row_spec = pl.BlockSpec((tm, n), lambda i, kk: (i, 0))
    in_specs = [pl.BlockSpec((tm, tk), lambda i, kk: (i, kk)),
                pl.BlockSpec((None, tk, n), lambda i, kk: (layer, kk, 0)),
                row_spec,
                pl.BlockSpec((8, n), lambda i, kk: (0, gate_chunk)),
                pl.BlockSpec((1, n), lambda i, kk: (0, 0))]
    args = [a, w.arr, x, mod, norm_w.reshape(1, n)]
    if modulate:
        in_specs += [pl.BlockSpec((8, n), lambda i, kk: (0, sh_chunk)),
                     pl.BlockSpec((8, n), lambda i, kk: (0, sc_chunk))]
        args += [next_mod, next_mod]
    out_specs = ([row_spec] if write_x else []) + [row_spec]
    out_shape = ([jax.ShapeDtypeStruct((m, n), F32)] if write_x else []) + [jax.ShapeDtypeStruct((m, n), out_dtype)]
    outs = pl.pallas_call(
        kern, grid=(m // tm, n_k),
        in_specs=in_specs, out_specs=out_specs, out_shape=out_shape,
        scratch_shapes=[pltpu.VMEM((tm, n), F32)] if n_k > 1 else [],
        compiler_params=_cparams("parallel", "arbitrary"),
        name="matmul_residual_norm",
    )(*args)
    return (outs[0], outs[1]) if write_x else (None, outs[0])


def _merge_kernel(y0, y1, y2, y3, g0, g1, g2, g3, wb_ref, o_ref):
    acc = None
    for i, (y, g) in enumerate(((y0, g0), (y1, g1), (y2, g2), (y3, g3))):
        gate = 0.5 * jnp.tanh(0.5 * g[...].astype(F32)) + 0.5
        t = gate * jnp.dot(y[...], wb_ref[i], preferred_element_type=F32)
        acc = t if acc is None else acc + t
    o_ref[...] = acc.astype(o_ref.dtype)


def merge_branches(ys, gate_pre, wb, layer):
    m, w = ys[0].shape
    d = wb.shape[3]
    tm = _pick_tile(m, 512, 16)
    tn = _pick_tile(d, 1024, LANES)
    nb = d // tn
    y_spec = pl.BlockSpec((tm, w), lambda j, i: (i, 0))
    g_specs = [pl.BlockSpec((tm, tn), functools.partial(lambda j, i, b: (i, b * nb + j), b=b))
               for b in range(N_BRANCH)]
    return pl.pallas_call(
        _merge_kernel, grid=(nb, m // tm),
        in_specs=[y_spec] * N_BRANCH + g_specs + [pl.BlockSpec((None, N_BRANCH, w, tn),
                                                               lambda j, i: (layer, 0, 0, j))],
        out_specs=pl.BlockSpec((tm, tn), lambda j, i: (i, j)),
        out_shape=jax.ShapeDtypeStruct((m, d), BF16),
        compiler_params=_cparams("parallel", "parallel"),
        name="merge_branches",
    )(*ys, gate_pre, gate_pre, gate_pre, gate_pre, wb)


FFT_ROWS = 16


def _store_cols(o_ref, lead, y, n2):
    cw = y.shape[1]
    tiles = jnp.swapaxes(y.reshape(y.shape[0] // n2, n2, cw), 0, 1)
    for j in range(n2):
        o_ref[lead + (slice(None), slice(j * cw, (j + 1) * cw))] = tiles[j]


def _short_conv_kernel(x_ref, prev_ref, next_ref, w_ref, b_ref, *o_refs, n_tiles, col_n2):
    i = pl.program_id(1)
    t = x_ref.shape[0]
    halo = prev_ref.shape[0]
    cw = x_ref.shape[1] // len(o_refs)
    row = lax.broadcasted_iota(jnp.int32, (t, cw), 0)
    for n, o_ref in enumerate(o_refs):
        cols = slice(n * cw, (n + 1) * cw)
        x = x_ref[:, cols].astype(F32)
        before = jnp.where(i > 0, prev_ref[:, cols].astype(F32)[halo - 1:halo, :], 0.0)
        after = jnp.where(i < n_tiles - 1, next_ref[:, cols].astype(F32)[0:1, :], 0.0)
        xm1 = jnp.where(row == 0, before, pltpu.roll(x, 1, 0))
        xp1 = jnp.where(row == t - 1, after, pltpu.roll(x, t - 1, 0))
        w = w_ref[:, cols]
        y = (w[0:1, :] * xm1 + w[1:2, :] * x + w[2:3, :] * xp1 + b_ref[:, cols]).astype(o_ref.dtype)
        if col_n2 is None:
            o_ref[...] = y
        else:
            _store_cols(o_ref, (), y, col_n2)


def short_conv(p, conv_w, conv_b, n_out, col_n2=None):
    b, l, c = p.shape
    cw = c // n_out
    halo = 16
    t = _pick_tile(l, 512, halo) if col_n2 is None else FFT_ROWS * col_n2
    n_tiles = l // t
    hb = t // halo
    kern = functools.partial(_short_conv_kernel, n_tiles=n_tiles, col_n2=col_n2)
    if col_n2 is None:
        out_specs = [pl.BlockSpec((None, t, cw), lambda bi, i: (bi, i, 0))] * n_out
        out_shape = [jax.ShapeDtypeStruct((b, l, cw), BF16)] * n_out
    else:
        out_specs = [pl.BlockSpec((None, FFT_ROWS, col_n2 * cw), lambda bi, i: (bi, i, 0))] * n_out
        out_shape = [jax.ShapeDtypeStruct((b, l // col_n2, col_n2 * cw), BF16)] * n_out
    return pl.pallas_call(
        kern, grid=(b, n_tiles),
        in_specs=[pl.BlockSpec((None, t, c), lambda bi, i: (bi, i, 0)),
                  pl.BlockSpec((None, halo, c), lambda bi, i: (bi, jnp.maximum(i * hb - 1, 0), 0)),
                  pl.BlockSpec((None, halo, c), lambda bi, i: (bi, jnp.minimum((i + 1) * hb, l // halo - 1), 0)),
                  pl.BlockSpec((8, c), lambda bi, i: (0, 0)),
                  pl.BlockSpec((1, c), lambda bi, i: (0, 0))],
        out_specs=out_specs, out_shape=out_shape,
        compiler_params=_cparams("parallel", "parallel"),
        name="short_conv",
    )(p, p, p, jnp.pad(conv_w, ((0, 8 - conv_w.shape[0]), (0, 0))), conv_b.reshape(1, c))


def _hy_filter_kernel(fv_ref, c2_ref, s2_ref, w1_ref, b1_ref, sf1_ref, w2_ref, b2_ref, sf2_ref, w3f_ref, w3b_ref,
                      dl_ref, f_ref, ss_ref, cb_ref, sb_ref, *, seq_len, col_n2):
    i = pl.program_id(0)
    tl = cb_ref.shape[0]
    hi = lax.Precision.HIGHEST
    step = 2.0 * math.pi / float(seq_len)
    r = lax.broadcasted_iota(jnp.int32, (tl, 1), 0)
    lane = lax.broadcasted_iota(jnp.int32, (1, LANES), 1)

    @pl.when(i == 0)
    def _():
        ang_r = (step * r.astype(F32)) * fv_ref[...]
        cb_ref[...] = jnp.cos(ang_r)
        sb_ref[...] = jnp.sin(ang_r)
        ss_ref[...] = jnp.zeros(ss_ref.shape, F32)

    base = (step * (i * tl).astype(F32)) * fv_ref[...]
    ca, sa = jnp.cos(base), jnp.sin(base)
    cb, sb = cb_ref[...], sb_ref[...]
    cos_j, sin_j = ca * cb - sa * sb, sa * cb + ca * sb
    cos_m, sin_m = c2_ref[...] * cos_j + s2_ref[...] * sin_j, s2_ref[...] * cos_j - c2_ref[...] * sin_j
    j = (i * tl + r).astype(F32)

    def mlp(t, cos_t, sin_t):
        t_norm = t / float(max(seq_len - 1, 1))
        feat = jnp.where(lane == 0, t_norm,
                         jnp.where(lane <= HY_BANDS, cos_t, jnp.where(lane <= 2 * HY_BANDS, -sin_t, 0.0)))
        z = jnp.sin(sf1_ref[...] * (jnp.dot(feat, w1_ref[...], precision=hi, preferred_element_type=F32)
                                    + b1_ref[...]))
        z = jnp.sin(sf2_ref[...] * (jnp.dot(z, w2_ref[...], precision=hi, preferred_element_type=F32)
                                    + b2_ref[...]))
        return z.astype(BF16), t_norm

    zf, tnf = mlp(j, cos_j, sin_j)
    zb, tnb = mlp(float(seq_len) - j, cos_m, sin_m)
    cw = w3f_ref.shape[1] // f_ref.shape[0]
    for o in range(f_ref.shape[0]):
        cols = slice(o * cw, (o + 1) * cw)
        hf = jnp.dot(zf, w3f_ref[:, cols], preferred_element_type=F32) * jnp.exp(-tnf * dl_ref[:, cols])
        hb = jnp.dot(zb, w3b_ref[:, cols], preferred_element_type=F32) * jnp.exp(-tnb * dl_ref[:, cols])
        hb = jnp.where(j > 0.0, hb, 0.0)
        ss_ref[:, cols] += jnp.sum(hf * hf, axis=0, keepdims=True) + jnp.sum(hb * hb, axis=0, keepdims=True)
        for s, h in enumerate((hf, hb)):
            if col_n2 is None:
                f_ref[o, s] = h.astype(f_ref.dtype)
            else:
                _store_cols(f_ref, (o, s), h.astype(f_ref.dtype), col_n2)


def hy_filters(seq_len, w1, b1, w2, b2, w3, sin_freq, hy_w, col_n2=None):
    emb, ffn = w1.shape
    cw = 2 * hy_w
    f = np.linspace(1e-4, HY_BANDS - 1, HY_BANDS)
    fv = np.zeros((1, LANES), np.float32)
    fv[0, 1:1 + HY_BANDS] = f
    fv[0, 1 + HY_BANDS:1 + 2 * HY_BANDS] = f
    c2 = np.cos(2.0 * np.pi * fv.astype(np.float64)).astype(np.float32)
    s2 = np.sin(2.0 * np.pi * fv.astype(np.float64)).astype(np.float32)
    w1p = jnp.pad(w1, ((0, LANES - emb), (0, 0)))
    w3r = w3.reshape(ffn, 2, 2, hy_w)
    w3f = w3r[:, :, 0, :].reshape(ffn, cw).astype(BF16)
    w3b = w3r[:, :, 1, :].reshape(ffn, cw).astype(BF16)
    deltas = np.abs(np.linspace(math.log(HY_DECAY_TARGET) / HY_SLOW_PCT,
                                math.log(HY_DECAY_TARGET) / HY_FAST_PCT, hy_w)).astype(np.float32)
    dl = jnp.asarray(np.tile(deltas, 2).reshape(1, cw))
    const = lambda i: (0, 0)
    if col_n2 is None:
        tl = _pick_tile(seq_len, 512, 16)
        f_spec = pl.BlockSpec((2, 2, tl, hy_w), lambda i: (0, 0, i, 0))
        f_shape = jax.ShapeDtypeStruct((2, 2, seq_len, hy_w), BF16)
    else:
        tl = FFT_ROWS * col_n2
        f_spec = pl.BlockSpec((2, 2, FFT_ROWS, col_n2 * hy_w), lambda i: (0, 0, i, 0))
        f_shape = jax.ShapeDtypeStruct((2, 2, seq_len // col_n2, col_n2 * hy_w), BF16)
    kern = functools.partial(_hy_filter_kernel, seq_len=seq_len, col_n2=col_n2)
    return pl.pallas_call(
        kern, grid=(seq_len // tl,),
        in_specs=[pl.BlockSpec((1, LANES), const), pl.BlockSpec((1, LANES), const), pl.BlockSpec((1, LANES), const),
                  pl.BlockSpec((LANES, ffn), const),
                  pl.BlockSpec((1, ffn), const), pl.BlockSpec((1, ffn), const),
                  pl.BlockSpec((ffn, ffn), const), pl.BlockSpec((1, ffn), const), pl.BlockSpec((1, ffn), const),
                  pl.BlockSpec((ffn, cw), const), pl.BlockSpec((ffn, cw), const), pl.BlockSpec((1, cw), const)],
        out_specs=[f_spec, pl.BlockSpec((1, cw), const)],
        out_shape=[f_shape, jax.ShapeDtypeStruct((1, cw), F32)],
        scratch_shapes=[pltpu.VMEM((tl, LANES), F32), pltpu.VMEM((tl, LANES), F32)],
        compiler_params=_cparams("arbitrary"),
        name="hy_filters",
    )(jnp.asarray(fv), jnp.asarray(c2), jnp.asarray(s2), w1p, b1.reshape(1, ffn), sin_freq[0].reshape(1, ffn), w2,
      b2.reshape(1, ffn), sin_freq[1].reshape(1, ffn), w3f, w3b, dl)


def _phase_mats(phase_num, denom, conj):
    ang = np.pi * (phase_num % (2 * denom)).astype(np.float64) / denom
    cr, ci = np.cos(ang), (np.sin(ang) if conj else -np.sin(ang))
    return cr, ci


def _stage1_mats(n1):
    k = np.arange(n1)[:, None]
    n = np.arange(n1)[None, :]
    w1d, w1f, wfin = [], [], []
    for v in (0, 1):
        cr, ci = _phase_mats(2 * k * n + v * n, n1, conj=False)
        w1d.append(np.block([[cr, -ci], [ci, cr]]))
        sgn = 1.0 if v == 0 else -1.0
        w1f.append(np.block([[cr, sgn * cr], [ci, sgn * ci]]))
        cri, cii = _phase_mats(2 * n.T * k.T + v * n.T, n1, conj=True)
        wfin.append(np.block([[cri, -cii], [cii, cri]]))
    return (np.concatenate(w1d, 0).astype(np.float32), np.concatenate(w1f, 0).astype(np.float32),
            np.concatenate(wfin, 1).astype(np.float32))


def _stage2_mats(n1, n2):
    l = n1 * n2
    k2 = np.arange(n2)[:, None]
    nn = np.arange(n2)[None, :]
    base = (2 * n1 * k2 * nn) % (2 * l)
    br, bi = np.cos(np.pi * base / l), -np.sin(np.pi * base / l)
    k1 = np.arange(n1)[None, :, None]
    v = np.arange(2)[:, None, None]
    tw = (2 * np.arange(n2)[None, None, :] * k1 + v * np.arange(n2)[None, None, :]) % (2 * l)
    tr, ti = np.cos(np.pi * tw / l), -np.sin(np.pi * tw / l)
    br, bi, tr, ti = (jnp.asarray(a, F32) for a in (br, bi, tr, ti))
    cr = br[None, None] * tr[:, :, None, :] - bi[None, None] * ti[:, :, None, :]
    ci = br[None, None] * ti[:, :, None, :] + bi[None, None] * tr[:, :, None, :]
    fwd = jnp.concatenate([jnp.concatenate([cr, -ci], -1), jnp.concatenate([ci, cr], -1)], -2)
    crt, cit = jnp.swapaxes(cr, -1, -2), -jnp.swapaxes(ci, -1, -2)
    inv = jnp.concatenate([jnp.concatenate([crt, -cit], -1), jnp.concatenate([cit, crt], -1)], -2)
    return fwd.astype(BF16), inv.astype(BF16)


def _fft_stage1_kernel(z_ref, w_ref, o_ref):
    n1 = z_ref.shape[1]
    nj, cw = o_ref.shape[3], o_ref.shape[4]
    x = z_ref[...].reshape(2 * n1, nj * cw)
    for plane in range(4):
        y = jnp.dot(w_ref[plane * n1:(plane + 1) * n1, :], x, preferred_element_type=F32).astype(o_ref.dtype)
        tiles = jnp.stack([y[:, j * cw:(j + 1) * cw] for j in range(nj)], axis=0)
        o_ref[plane // 2, plane % 2] = jnp.swapaxes(tiles, 0, 1)


FFT_COLS = 2048
FFT_S1_N2 = 16


def fft_stage1(z, w1, n1, n2):
    grp = z.shape[0]
    w = z.shape[3] // n2
    nj = _pick_tile(n2, FFT_S1_N2, 16)
    return pl.pallas_call(
        _fft_stage1_kernel, grid=(grp, n2 // nj),
        in_specs=[pl.BlockSpec((None, 2, n1, nj * w), lambda gi, j: (gi, 0, 0, j)),
                  pl.BlockSpec((4 * n1, 2 * n1), lambda gi, j: (0, 0))],
        out_specs=pl.BlockSpec((None, 2, 2, n1, nj, w), lambda gi, j: (gi, 0, 0, 0, j, 0)),
        out_shape=jax.ShapeDtypeStruct((grp, 2, 2, n1, n2, w), BF16),
        compiler_params=_cparams("parallel", "parallel"),
        name="fft_stage1",
    )(z, w1)


def _fft_mid_kernel(a_ref, f_ref, ss_ref, wf_ref, wi_ref, o_ref, *, scale):
    kb, n2, cw = a_ref.shape[1], a_ref.shape[2], a_ref.shape[3]
    nrm = lax.rsqrt(ss_ref[...] + EPS) * scale
    us = []
    for kk in range(kb):
        wf = wf_ref[kk]
        t = jnp.dot(wf, a_ref[:, kk].reshape(2 * n2, cw), preferred_element_type=F32)
        g = jnp.dot(wf, f_ref[:, kk].reshape(2 * n2, cw), preferred_element_type=F32) * nrm
        tr, ti, gr, gi = t[:n2], t[n2:], g[:n2], g[n2:]
        p = jnp.concatenate([tr * gr - ti * gi, tr * gi + ti * gr], axis=0).astype(BF16)
        us.append(jnp.dot(wi_ref[kk], p, preferred_element_type=F32).astype(o_ref.dtype))
    for r in range(2):
        tiles = jnp.swapaxes(jnp.stack([u[r * n2:(r + 1) * n2] for u in us], axis=0), 0, 1)
        for j in range(n2):
            o_ref[r, :, j * cw:(j + 1) * cw] = tiles[j]


FFT_MID_K1 = 16


def fft_mid(a, af, ss, wf, wi, order, n1, n2):
    cw = a.shape[-1]
    kb = _pick_tile(n1, FFT_MID_K1, 16)
    kern = functools.partial(_fft_mid_kernel, scale=1.0 / (2.0 * n1 * n2))
    mat_spec = pl.BlockSpec((None, kb, 2 * n2, 2 * n2), lambda v, k: (v, k, 0, 0))
    return pl.pallas_call(
        kern, grid=(2, n1 // kb),
        in_specs=[pl.BlockSpec((None, 2, kb, n2, cw), lambda v, k: (v, 0, k, 0, 0)),
                  pl.BlockSpec((None, None, 2, kb, n2, cw), lambda v, k: (order, v, 0, k, 0, 0)),
                  pl.BlockSpec((1, cw), lambda v, k: (0, order)),
                  mat_spec, mat_spec],
        out_specs=pl.BlockSpec((None, 2, kb, n2 * cw), lambda v, k: (v, 0, k, 0)),
        out_shape=jax.ShapeDtypeStruct((2, 2, n1, n2 * cw), BF16),
        compiler_params=_cparams("parallel", "parallel"),
        name="fft_mid",
    )(a, af, ss, wf, wi)


def _fft_final_kernel(u_ref, w_ref, gate_ref, z_ref, bias_ref, o_ref):
    n1 = gate_ref.shape[1]
    cw = gate_ref.shape[2]
    u = u_ref[...].reshape(4 * n1, cw)
    y = jnp.dot(w_ref[...], u, preferred_element_type=F32).reshape(2, n1, cw)
    z = z_ref[...].astype(F32)
    o_ref[...] = (gate_ref[...].astype(F32) * (y + bias_ref[...] * z)).astype(o_ref.dtype)


def fft_final(u, wfin, gate, z, bias, n1, n2):
    cols = z.shape[2]
    cw = cols // n2
    tc = _pick_tile(cols, FFT_COLS, cw)
    return pl.pallas_call(
        _fft_final_kernel, grid=(cols // tc,),
        in_specs=[pl.BlockSpec((2, 2, n1, tc), lambda j: (0, 0, 0, j)),
                  pl.BlockSpec((2 * n1, 4 * n1), lambda j: (0, 0)),
                  pl.BlockSpec((2, n1, tc), lambda j: (0, 0, j)),
                  pl.BlockSpec((2, n1, tc), lambda j: (0, 0, j)),
                  pl.BlockSpec((1, tc), lambda j: (0, 0))],
        out_specs=pl.BlockSpec((2, n1, tc), lambda j: (0, 0, j)),
        out_shape=jax.ShapeDtypeStruct((2, n1, cols), BF16),
        compiler_params=_cparams("parallel"),
        name="fft_final",
    )(u, wfin, gate, z, jnp.tile(bias.reshape(1, cw), (1, tc // cw)))


def _hy_dense_kernel(z_ref, gate_ref, f_ref, ss_ref, bias_ref, wf_ref, wi_ref, o_ref):
    lc = z_ref.shape[1]
    cw = z_ref.shape[2]
    nrm = lax.rsqrt(ss_ref[...] + EPS) * (1.0 / (2.0 * lc))
    f1, f2 = f_ref[0].astype(F32), f_ref[1].astype(F32)
    filt = (((f1 + f2) * nrm).astype(BF16), ((f1 - f2) * nrm).astype(BF16))
    x = z_ref[...].reshape(2 * lc, cw)
    acc = None
    for v in (0, 1):
        wf = wf_ref[v]
        t = jnp.dot(wf, x, preferred_element_type=F32)
        g = jnp.dot(wf[:, :lc], filt[v], preferred_element_type=F32)
        tr, ti, gr, gi = t[:lc], t[lc:], g[:lc], g[lc:]
        p = jnp.concatenate([tr * gr - ti * gi, tr * gi + ti * gr], axis=0).astype(BF16)
        u = jnp.dot(wi_ref[v], p, preferred_element_type=F32)
        acc = u if acc is None else acc + u
    y = acc.reshape(2, lc, cw)
    o_ref[...] = (gate_ref[...].astype(F32) * (y + bias_ref[...] * z_ref[...].astype(F32))).astype(o_ref.dtype)


def hy_dense_conv(z_arr, gate_arr, f, ss, order, bias, wf, wi):
    lc, cw = z_arr.shape[1], z_arr.shape[2]
    return pl.pallas_call(
        _hy_dense_kernel, grid=(1,),
        in_specs=[pl.BlockSpec((2, lc, cw), lambda i: (0, 0, 0)),
                  pl.BlockSpec((2, lc, cw), lambda i: (0, 0, 0)),
                  pl.BlockSpec((None, 2, lc, cw), lambda i: (order, 0, 0, 0)),
                  pl.BlockSpec((1, cw), lambda i: (0, order)),
                  pl.BlockSpec((1, cw), lambda i: (0, 0)),
                  pl.BlockSpec((2, None, 2 * lc, 2 * lc), lambda i: (0, 0, 0, 0)),
                  pl.BlockSpec((2, None, 2 * lc, 2 * lc), lambda i: (0, 0, 0, 0))],
        out_specs=pl.BlockSpec((2, lc, cw), lambda i: (0, 0, 0)),
        out_shape=jax.ShapeDtypeStruct((2, lc, cw), BF16),
        compiler_params=_cparams("arbitrary"),
        name="hy_dense_conv",
    )(z_arr, gate_arr, f, ss, bias.reshape(1, cw), wf, wi)


def hyena_mixer(p, conv_w, conv_b, w1, b1, w2, b2, w3, sin_freq, bias, mats):
    b, l, c3 = p.shape
    assert b == 2, "batch elements are packed as the re/im parts of one complex signal"
    hy_w = c3 // 3
    if l <= 2 * FFT_N2:
        v, x1, x2 = short_conv(p, conv_w, conv_b, 3)
        f, ss = hy_filters(l, w1, b1, w2, b2, w3, sin_freq, hy_w)
        wf, wi = mats["dense"]
        z2 = hy_dense_conv(v, x1, f, ss, 0, bias[0], wf, wi)
        return hy_dense_conv(z2, x2, f, ss, 1, bias[1], wf, wi)
    n2 = FFT_N2
    n1 = l // n2
    w1d, w1f, wfin, wf, wi = mats["fft"]
    v, x1, x2 = short_conv(p, conv_w, conv_b, 3, col_n2=n2)
    f, ss = hy_filters(l, w1, b1, w2, b2, w3, sin_freq, hy_w, col_n2=n2)
    af = fft_stage1(f, w1f, n1, n2)
    a = fft_stage1(v[None], w1d, n1, n2)[0]
    z2 = fft_final(fft_mid(a, af, ss, wf, wi, 0, n1, n2), wfin, x1, v, bias[0], n1, n2)
    a = fft_stage1(z2[None], w1d, n1, n2)[0]
    y = fft_final(fft_mid(a, af, ss, wf, wi, 1, n1, n2), wfin, x2, z2, bias[1], n1, n2)
    return y.reshape(2, l, hy_w)


def hyena_mats(l_lat, l_ctx):
    n1 = l_lat // FFT_N2
    w1d, w1f, wfin = _stage1_mats(n1)
    wf, wi = _stage2_mats(n1, FFT_N2)
    mats = {"fft": (jnp.asarray(w1d, BF16), jnp.asarray(w1f, BF16), jnp.asarray(wfin, BF16), wf, wi)}
    mats["dense"] = _stage2_mats(1, l_ctx)
    return mats


def _ctx_attn_kernel(q_ref, k_ref, v_ref, o_ref):
    s = lax.dot_general(q_ref[...], k_ref[...], NT_DIMS, preferred_element_type=F32)
    m = jnp.max(s, axis=-1, keepdims=True)
    p = jnp.exp(s - m)
    l = jnp.sum(p, axis=-1, keepdims=True)
    o = jnp.dot(p.astype(BF16), v_ref[...], preferred_element_type=F32) / l
    o_ref[...] = o.astype(o_ref.dtype)


def ctx_attention(qa, q_off, ka, k_off, va, v_off, n_q_heads, group):
    b, _, lc, d = qa.shape
    return pl.pallas_call(
        _ctx_attn_kernel, grid=(b, n_q_heads),
        in_specs=[pl.BlockSpec((None, None, lc, d), lambda bi, h: (bi, q_off + h, 0, 0)),
                  pl.BlockSpec((None, None, lc, d), lambda bi, h: (bi, k_off + h // group, 0, 0)),
                  pl.BlockSpec((None, None, lc, d), lambda bi, h: (bi, v_off + h // group, 0, 0))],
        out_specs=pl.BlockSpec((None, lc, d), lambda bi, h: (bi, 0, h)),
        out_shape=jax.ShapeDtypeStruct((b, lc, n_q_heads * d), BF16),
        compiler_params=_cparams("parallel", "parallel"),
        name="ctx_attention",
    )(qa, ka, va)


NA_TILE_ROWS = 4
NA_KEY_ROWS = 12
NA_HEADS_PER_STEP = 2


def _na_geometry(n_rows):
    wr = min(NA_WIN_R, n_rows)
    nt = n_rows // NA_TILE_ROWS
    sigs = []
    for t in range(nt):
        rt = t * NA_TILE_ROWS
        w0 = int(np.clip(rt - wr // 2, 0, n_rows - NA_KEY_ROWS))
        r = rt + np.arange(NA_TILE_ROWS)
        r0 = np.clip(r - wr // 2, 0, n_rows - wr)
        sigs.append((w0 - rt, tuple((r0 - w0).tolist())))
    classes = (sigs[0], sigs[1], sigs[-1])
    for t, s in enumerate(sigs):
        assert s == classes[0 if t == 0 else (2 if t == nt - 1 else 1)], "tile does not match its bias class"
    return wr, nt, classes


def _na_bias_tables(rpb, n_rows):
    wr, _, classes = _na_geometry(n_rows)
    cols = np.arange(GRID_W)
    c0 = np.clip(cols - NA_WIN_C // 2, 0, GRID_W - NA_WIN_C)
    col_ok = (cols[None, :] >= c0[:, None]) & (cols[None, :] < c0[:, None] + NA_WIN_C)
    col_idx = np.clip(cols[None, :] - cols[:, None] + NA_WIN_C - 1, 0, 2 * NA_WIN_C - 2)
    col_sel = (col_idx[None] == np.arange(2 * NA_WIN_C - 1)[:, None, None]).astype(np.float32)
    row_sel, oks = [], []
    for off, rel_r0 in classes:
        qi = np.arange(NA_TILE_ROWS)[:, None]
        kw = np.arange(NA_KEY_ROWS)[None, :]
        r0 = np.asarray(rel_r0)[:, None]
        row_ok = (kw >= r0) & (kw < r0 + wr)
        row_idx = np.clip(kw + off - qi + NA_WIN_R - 1, 0, 2 * NA_WIN_R - 2)
        row_sel.append((row_idx[None] == np.arange(2 * NA_WIN_R - 1)[:, None, None]).astype(np.float32))
        oks.append(row_ok[:, None, :, None] & col_ok[None, :, None, :])
    vals = jnp.einsum("hab,saqk,bcd->hsqckd", rpb.astype(F32), jnp.asarray(np.stack(row_sel)),
                      jnp.asarray(col_sel), precision=lax.Precision.HIGHEST)
    tab = jnp.where(jnp.asarray(np.stack(oks))[None], vals, NEG_BIG)
    return tab.reshape(rpb.shape[0], 3, NA_TILE_ROWS * GRID_W, NA_KEY_ROWS * GRID_W)


def _na_kernel(q_ref, k_ref, v_ref, kc_ref, vc_ref, bias_ref, o_ref, *, n_rows, win_half):
    t = pl.program_id(2)
    w0 = jnp.clip(t * NA_TILE_ROWS - win_half, 0, n_rows - NA_KEY_ROWS)
    start = pl.multiple_of(w0 * GRID_W, GRID_W)
    outs = []
    for hh in range(q_ref.shape[0]):
        kwin = k_ref[hh, pl.ds(start, NA_KEY_ROWS * GRID_W), :]
        vwin = v_ref[hh, pl.ds(start, NA_KEY_ROWS * GRID_W), :]
        q = q_ref[hh]
        s_lat = lax.dot_general(q, kwin, NT_DIMS, preferred_element_type=F32) + bias_ref[hh]
        s_ctx = lax.dot_general(q, kc_ref[hh], NT_DIMS, preferred_element_type=F32)
        m = jnp.maximum(jnp.max(s_lat, axis=-1, keepdims=True), jnp.max(s_ctx, axis=-1, keepdims=True))
        p_lat = jnp.exp(s_lat - m)
        p_ctx = jnp.exp(s_ctx - m)
        l = jnp.sum(p_lat, axis=-1, keepdims=True) + jnp.sum(p_ctx, axis=-1, keepdims=True)
        o = (jnp.dot(p_lat.astype(BF16), vwin, preferred_element_type=F32)
             + jnp.dot(p_ctx.astype(BF16), vc_ref[hh], preferred_element_type=F32)) / l
        outs.append(o.astype(o_ref.dtype))
    o_ref[...] = jnp.concatenate(outs, axis=1)


def na_attention(pl_heads, pc_heads, rpb):
    b, h3, l, d = pl_heads.shape
    lc = pc_heads.shape[2]
    h = h3 // 3
    n_rows = l // GRID_W
    wr, nt, _ = _na_geometry(n_rows)
    bias = _na_bias_tables(rpb, n_rows)
    tq = NA_TILE_ROWS * GRID_W
    kw = NA_KEY_ROWS * GRID_W
    kern = functools.partial(_na_kernel, n_rows=n_rows, win_half=wr // 2)

    def cls(t):
        return jnp.where(t == 0, 0, jnp.where(t == nt - 1, 2, 1))

    hs = NA_HEADS_PER_STEP
    assert h % hs == 0
    hb = h // hs
    return pl.pallas_call(
        kern, grid=(b, hb, nt),
        in_specs=[pl.BlockSpec((None, hs, tq, d), lambda bi, hi, t: (bi, hi, t, 0)),
                  pl.BlockSpec((None, hs, l, d), lambda bi, hi, t: (bi, hb + hi, 0, 0)),
                  pl.BlockSpec((None, hs, l, d), lambda bi, hi, t: (bi, 2 * hb + hi, 0, 0)),
                  pl.BlockSpec((None, hs, lc, d), lambda bi, hi, t: (bi, hb + hi, 0, 0)),
                  pl.BlockSpec((None, hs, lc, d), lambda bi, hi, t: (bi, 2 * hb + hi, 0, 0)),
                  pl.BlockSpec((hs, None, tq, kw), lambda bi, hi, t: (hi, cls(t), 0, 0))],
        out_specs=pl.BlockSpec((None, tq, hs * d), lambda bi, hi, t: (bi, t, hi)),
        out_shape=jax.ShapeDtypeStruct((b, l, h * d), BF16),
        compiler_params=_cparams("parallel", "parallel", "parallel"),
        name="na_attention",
    )(pl_heads, pl_heads, pl_heads, pc_heads, pc_heads, bias)


def _ret_kernel(*refs, reverse, rope, finalize, n_heads):
    it = iter(refs)
    lg_ref, p_ref = next(it), next(it)
    cos_ref, sin_ref = (next(it), next(it)) if rope else (None, None)
    s0_ref = next(it)
    oprev_ref, gnw_ref = (next(it), next(it)) if finalize else (None, None)
    o_ref, sfin_ref, state_ref = next(it), next(it), next(it)

    i = pl.program_id(1)
    n_steps = pl.num_programs(1)

    @pl.when(i == 0)
    def _():
        state_ref[...] = s0_ref[...]

    c = RET_CHUNK
    ts = p_ref.shape[0]
    nc = ts // c
    dq = n_heads * RET_DK
    dv = n_heads * RET_DV
    x = p_ref[...]
    q = x[:, :dq].astype(F32)
    k = x[:, dq:2 * dq].astype(F32)
    v = x[:, 2 * dq:2 * dq + dv]
    if rope:
        q = q * cos_ref[...] + _swap32(q) * sin_ref[...]
        k = k * cos_ref[...] + _swap32(k) * sin_ref[...]
    jr = lax.broadcasted_iota(jnp.int32, (c, c), 0)
    jc = lax.broadcasted_iota(jnp.int32, (c, c), 1)
    rel = ((jc - jr) if reverse else (jr - jc)).astype(F32)
    jcol = lax.broadcasted_iota(jnp.int32, (c, 1), 0).astype(F32)
    order = range(nc - 1, -1, -1) if reverse else range(nc)
    blocks = [[None] * n_heads for _ in range(nc)]
    for h in range(n_heads):
        g = lg_ref[h]
        intra = jnp.where(rel >= 0.0, jnp.exp(g * jnp.maximum(rel, 0.0)), 0.0)
        if reverse:
            cross_f = jnp.exp(g * (float(c) - jcol))
            k_dec = jnp.exp(g * jcol)
        else:
            cross_f = jnp.exp(g * (jcol + 1.0))
            k_dec = jnp.exp(g * (float(c - 1) - jcol))
        chunk_decay = jnp.exp(g * float(c))
        state = state_ref[h]
        for ci in order:
            rows = slice(ci * c, (ci + 1) * c)
            qh = q[rows, h * RET_DK:(h + 1) * RET_DK].astype(BF16)
            kf = k[rows, h * RET_DK:(h + 1) * RET_DK]
            vh = v[rows, h * RET_DV:(h + 1) * RET_DV]
            scores = lax.dot_general(qh, kf.astype(BF16), NT_DIMS, preferred_element_type=F32) * intra
            inner = jnp.dot(scores.astype(BF16), vh, preferred_element_type=F32)
            cross = jnp.dot(qh, state.astype(BF16), preferred_element_type=F32) * cross_f
            blocks[ci][h] = inner + cross
            kv = lax.dot_general((kf * k_dec).astype(BF16), vh, TN_DIMS, preferred_element_type=F32)
            state = chunk_decay * state + kv
        state_ref[h] = state
    o = jnp.concatenate([jnp.concatenate(blocks[ci], axis=1) for ci in range(nc)], axis=0)
    if finalize:
        o = o + oprev_ref[...]
        gate = x[:, 2 * dq + dv:2 * dq + 2 * dv].astype(F32)
        normed = []
        for h in range(n_heads):
            oh = o[:, h * RET_DV:(h + 1) * RET_DV]
            normed.append(oh * lax.rsqrt(jnp.mean(oh * oh, axis=-1, keepdims=True) + EPS))
        o = jnp.concatenate(normed, axis=1) * gnw_ref[...] * _silu(gate)
    o_ref[...] = o.astype(o_ref.dtype)

    @pl.when(i == n_steps - 1)
    def _():
        sfin_ref[...] = state_ref[...]


def retention_pass(p, log_g, s0, reverse, rope_tables=None, o_prev=None, gn_w=None):
    b, l, w = p.shape
    n_heads = w // (2 * RET_DK + 2 * RET_DV)
    dq, dv = n_heads * RET_DK, n_heads * RET_DV
    ts = _pick_tile(l, 512, RET_CHUNK)
    n_steps = l // ts
    finalize = o_prev is not None
    rope = rope_tables is not None

    def tile(i):
        return (n_steps - 1 - i) if reverse else i

    in_specs = [pl.BlockSpec(memory_space=pltpu.SMEM),
                pl.BlockSpec((None, ts, w), lambda bi, i: (bi, tile(i), 0))]
    args = [log_g, p]
    if rope:
        in_specs += [pl.BlockSpec((ts, dq), lambda bi, i: (tile(i), 0))] * 2
        args += list(rope_tables)
    in_specs.append(pl.BlockSpec((None, n_heads, RET_DK, RET_DV), lambda bi, i: (bi, 0, 0, 0)))
    args.append(s0)
    if finalize:
        in_specs += [pl.BlockSpec((None, ts, dv), lambda bi, i: (bi, tile(i), 0)),
                     pl.BlockSpec((1, dv), lambda bi, i: (0, 0))]
        args += [o_prev, gn_w.reshape(1, dv)]
    kern = functools.partial(_ret_kernel, reverse=reverse, rope=rope, finalize=finalize, n_heads=n_heads)
    return pl.pallas_call(
        kern, grid=(b, n_steps),
        in_specs=in_specs,
        out_specs=[pl.BlockSpec((None, ts, dv), lambda bi, i: (bi, tile(i), 0)),
                   pl.BlockSpec((None, n_heads, RET_DK, RET_DV), lambda bi, i: (bi, 0, 0, 0))],
        out_shape=[jax.ShapeDtypeStruct((b, l, dv), BF16 if finalize else F32),
                   jax.ShapeDtypeStruct((b, n_heads, RET_DK, RET_DV), F32)],
        scratch_shapes=[pltpu.VMEM((n_heads, RET_DK, RET_DV), F32)],
        compiler_params=_cparams("parallel", "arbitrary"),
        name="retention_pass",
    )(*args)


def _ret_rope_tables(l, n_heads):
    half = RET_DK // 2
    inv = RET_ROPE_BASE ** (-jnp.linspace(0.0, 1.0, half, dtype=F32))
    ang = jnp.arange(l, dtype=F32)[:, None] * inv
    cos, sin = lax.optimization_barrier((jnp.cos(ang), jnp.sin(ang)))
    return (jnp.tile(jnp.concatenate([cos, cos], -1), (1, n_heads)),
            jnp.tile(jnp.concatenate([-sin, sin], -1), (1, n_heads)))


def retention_mixer(p_l, p_c, log_decay, gn_w, tables, with_ctx_out):
    b = p_l.shape[0]
    n_heads = p_l.shape[2] // (2 * RET_DK + 2 * RET_DV)
    log_g = -jnp.abs(log_decay.astype(F32))
    s0 = jnp.zeros((b, n_heads, RET_DK, RET_DV), F32)
    o_cf, s_fwd = retention_pass(p_c, log_g[0], s0, False)
    y_c, s_bwd = retention_pass(p_c, log_g[1], s0, True, o_prev=o_cf, gn_w=gn_w)
    o_lf, _ = retention_pass(p_l, log_g[0], s_fwd, False, rope_tables=tables)
    y_l, _ = retention_pass(p_l, log_g[1], s_bwd, True, rope_tables=tables, o_prev=o_lf, gn_w=gn_w)
    return y_l, (y_c if with_ctx_out else None)


def _gqa_prep_kernel(*refs, rope, n_q, n_kv, q_scale):
    if rope:
        p_ref, cos_ref, sin_ref, qw_ref, kw_ref, q_out, k_out, v_out = refs
    else:
        p_ref, qw_ref, kw_ref, q_out, k_out, v_out = refs
    x = p_ref[...]
    d = HEAD_DIM

    def norm_rope(xh, w):
        xh = xh.astype(F32)
        y = xh * lax.rsqrt(jnp.mean(xh * xh, axis=-1, keepdims=True) + EPS) * w
        if rope:
            y = y * cos_ref[...] + _swap32(y) * sin_ref[...]
        return y

    for h in range(n_q):
        q_out[h] = (norm_rope(x[:, h * d:(h + 1) * d], qw_ref[...]) * q_scale).astype(q_out.dtype)
    for h in range(n_kv):
        k_out[h] = norm_rope(x[:, (n_q + h) * d:(n_q + h + 1) * d], kw_ref[...]).astype(k_out.dtype)
        lane = lax.broadcasted_iota(jnp.int32, (x.shape[0], d), 1)
        ones_col = jnp.where(lane == 0, 1.0, 0.0).astype(v_out.dtype)
        v_out[h] = jnp.concatenate([x[:, (n_q + n_kv + h) * d:(n_q + n_kv + h + 1) * d], ones_col], axis=1)


def gqa_prep(p, qn_w, kn_w, n_q, n_kv, q_scale, rope_tables=None):
    b, l, w = p.shape
    d = HEAD_DIM
    t = _pick_tile(l, 512, 16)
    rope = rope_tables is not None
    in_specs = [pl.BlockSpec((None, t, w), lambda bi, i: (bi, i, 0))]
    args = [p]
    if rope:
        in_specs += [pl.BlockSpec((t, d), lambda bi, i: (i, 0))] * 2
        args += list(rope_tables)
    in_specs += [pl.BlockSpec((1, d), lambda bi, i: (0, 0))] * 2
    args += [qn_w.reshape(1, d), kn_w.reshape(1, d)]
    kern = functools.partial(_gqa_prep_kernel, rope=rope, n_q=n_q, n_kv=n_kv, q_scale=q_scale)
    return pl.pallas_call(
        kern, grid=(b, l // t),
        in_specs=in_specs,
        out_specs=[pl.BlockSpec((None, n_q, t, d), lambda bi, i: (bi, 0, i, 0)),
                   pl.BlockSpec((None, n_kv, t, d), lambda bi, i: (bi, 0, i, 0)),
                   pl.BlockSpec((None, n_kv, t, 2 * d), lambda bi, i: (bi, 0, i, 0))],
        out_shape=[jax.ShapeDtypeStruct((b, n_q, l, d), BF16),
                   jax.ShapeDtypeStruct((b, n_kv, l, d), BF16),
                   jax.ShapeDtypeStruct((b, n_kv, l, 2 * d), BF16)],
        compiler_params=_cparams("parallel", "parallel"),
        name="gqa_prep",
    )(*args)


def _axial_rope_tables(l):
    nf = HEAD_DIM // 4
    t = jnp.arange(l)
    inv = ROPE_BASE ** (-jnp.arange(nf, dtype=F32) / nf)
    ang_r = (t // GRID_W).astype(F32)[:, None] * inv
    ang_c = (t % GRID_W).astype(F32)[:, None] * inv
    cr, sr, cc, sc = lax.optimization_barrier((jnp.cos(ang_r), jnp.sin(ang_r), jnp.cos(ang_c), jnp.sin(ang_c)))
    return (jnp.concatenate([cr, cr, cc, cc], -1), jnp.concatenate([-sr, sr, -sc, sc], -1))


FLASH_SUB_ROWS = 256


def _flash_kernel(q_ref, k_ref, v_ref, o_ref, m_ref, acc_ref):
    j = pl.program_id(3)
    g, tq, d = q_ref.shape

    @pl.when(j == 0)
    def _():
        m_ref[...] = jnp.full(m_ref.shape, NEG_BIG, F32)
        acc_ref[...] = jnp.zeros(acc_ref.shape, F32)

    q = q_ref[...].reshape(g * tq, d)
    k = k_ref[...]
    v = v_ref[...]
    n_sub = (g * tq) // FLASH_SUB_ROWS
    s, p, alpha = [None] * n_sub, [None] * n_sub, [None] * n_sub
    for t in range(n_sub + 2):
        if t < n_sub:
            s[t] = lax.dot_general(q[t * FLASH_SUB_ROWS:(t + 1) * FLASH_SUB_ROWS], k, NT_DIMS,
                                   preferred_element_type=F32)
        u = t - 1
        if 0 <= u < n_sub:
            rows = slice(u * FLASH_SUB_ROWS, (u + 1) * FLASH_SUB_ROWS)
            m_prev = m_ref[rows]
            m_new = jnp.maximum(m_prev, jnp.max(s[u], axis=-1, keepdims=True))
            alpha[u] = jnp.exp2(m_prev - m_new)
            m_ref[rows] = m_new
            p[u] = jnp.exp2((s[u] - m_new).astype(BF16))
            s[u] = None
        w = t - 2
        if 0 <= w < n_sub:
            rows = slice(w * FLASH_SUB_ROWS, (w + 1) * FLASH_SUB_ROWS)
            acc_ref[rows] = alpha[w] * acc_ref[rows] + jnp.dot(p[w], v, preferred_element_type=F32)
            p[w] = None

    @pl.when(j == pl.num_programs(3) - 1)
    def _():
        acc = acc_ref[...]
        o = acc[:, :d] / acc[:, d:d + 1]
        o_ref[...] = jnp.concatenate([o[h * tq:(h + 1) * tq] for h in range(g)], axis=1).astype(o_ref.dtype)


def flash_gqa(q, k, v):
    b, hq, l, d = q.shape
    hkv, lk = k.shape[1], k.shape[2]
    g = hq // hkv
    tq = _pick_tile(l, 1024, 16)
    tk = _pick_tile(lk, 3328, LANES)
    assert (g * tq) % FLASH_SUB_ROWS == 0
    return pl.pallas_call(
        _flash_kernel, grid=(b, hkv, l // tq, lk // tk),
        in_specs=[pl.BlockSpec((None, g, tq, d), lambda bi, h, i, j: (bi, h, i, 0)),
                  pl.BlockSpec((None, None, tk, d), lambda bi, h, i, j: (bi, h, j, 0)),
                  pl.BlockSpec((None, None, tk, 2 * d), lambda bi, h, i, j: (bi, h, j, 0))],
        out_specs=pl.BlockSpec((None, tq, g * d), lambda bi, h, i, j: (bi, i, h)),
        out_shape=jax.ShapeDtypeStruct((b, l, hq * d), BF16),
        scratch_shapes=[pltpu.VMEM((g * tq, 1), F32), pltpu.VMEM((g * tq, 2 * d), F32)],
        compiler_params=_cparams("parallel", "parallel", "parallel", "arbitrary"),
        name="flash_gqa",
    )(q, k, v)


def gqa_mixer(p_l, p_c, qn_w, kn_w, tables, with_ctx_out):
    n_q = p_l.shape[2] // (2 * HEAD_DIM)
    n_kv = n_q // 2
    scale = HEAD_DIM ** -0.5
    q_l, k_l, v_l = gqa_prep(p_l, qn_w, kn_w, n_q, n_kv, scale * math.log2(math.e), rope_tables=tables)
    q_c, k_c, v_c = gqa_prep(p_c, qn_w, kn_w, n_q, n_kv, scale)
    y_l = flash_gqa(q_l, jnp.concatenate([k_l, k_c], axis=2), jnp.concatenate([v_l, v_c], axis=2))
    y_c = ctx_attention(q_c, 0, k_c, 0, v_c, 0, n_q, n_q // n_kv) if with_ctx_out else None
    return y_l, y_c


W_IN_GATE_ALIGN = 2048


def _prep_w_in(w_in, d_model):
    mix_w = d_model // N_BRANCH
    h = mix_w // HEAD_DIM
    sizes = (3 * mix_w, 3 * h * HEAD_DIM, 2 * h * RET_DK + 2 * h * RET_DV,
             (h + 2 * max(h // 2, 1)) * HEAD_DIM, N_BRANCH * d_model)
    scale = np.ones((sum(sizes),), np.float32)
    scale[sizes[0]:sizes[0] + h * HEAD_DIM] = HEAD_DIM ** -0.5
    k0 = sizes[0] + sizes[1] + h * RET_DK
    scale[k0:k0 + h * RET_DK] = RET_DK ** -0.5
    head = sum(sizes[:4])
    pad = -head % W_IN_GATE_ALIGN
    w = w_in * jnp.asarray(scale)
    w = jnp.concatenate([w[..., :head], jnp.zeros(w.shape[:2] + (pad,), w.dtype), w[..., head:]], axis=-1)
    offs = [0, sizes[0], sizes[0] + sizes[1], sizes[0] + sizes[1] + sizes[2], head + pad]
    return w.astype(BF16), list(zip(offs, sizes))


def kernel(x, c, ctx, c_ctx, ada_w, ada_b, norm1_w, norm2_w, w_in, hy_conv_w, hy_conv_b, hy_ffn_w1, hy_ffn_b1,
           hy_ffn_w2, hy_ffn_b2, hy_ffn_w3, hy_sin_freq, hy_bias, na_rpb, ret_log_decay, ret_gn_w, gqa_q_norm_w,
           gqa_k_norm_w, w_branch, w_out, ffn_w13, ffn_w2, final_norm_w):
    b, l, d = x.shape
    lc = ctx.shape[1]
    depth = ada_w.shape[0]
    ffn_hidden = ffn_w2.shape[1]
    n_ret_heads = (d // N_BRANCH) // RET_DV

    cs = jnp.zeros((8, d), F32).at[:b].set(c).at[b].set(c_ctx)
    mods = ada_mod(cs, ada_w, ada_b)
    ctx_row = b
    mats = hyena_mats(l, lc)
    ret_tables = _ret_rope_tables(l, n_ret_heads)
    gqa_tables = _axial_rope_tables(l)

    w_in_b, groups = _prep_w_in(w_in, d)
    wb = w_branch.astype(BF16)
    w_out_b = w_out.astype(BF16)
    w13_b = ffn_w13.astype(BF16)
    w2_b = ffn_w2.astype(BF16)

    x_l = x.reshape(b * l, d)
    x_c = ctx.reshape(b * lc, d)
    for layer in range(depth):
        last = layer == depth - 1
        mod = mods[layer]
        w_hy, w_na, w_rt, w_gq, w_gate = (WCols(w_in_b, layer, off, n) for off, n in groups)
        hy_params = (hy_conv_w[layer], hy_conv_b[layer], hy_ffn_w1[layer], hy_ffn_b1[layer], hy_ffn_w2[layer],
                     hy_ffn_b2[layer], hy_ffn_w3[layer], hy_sin_freq[layer], hy_bias[layer])
        wo = WCols(w_out_b, layer, 0, d)
        w1 = WCols(w13_b, layer, 0, ffn_hidden)
        w3 = WCols(w13_b, layer, ffn_hidden, ffn_hidden)
        w2 = WCols(w2_b, layer, 0, d)

        if layer == 0:
            h_l = norm_mod(x_l, norm1_w[layer], mod, 0, 1, l, 0)
            h_c = norm_mod(x_c, norm1_w[layer], mod, 0, 1, None, ctx_row)

        na_l = matmul_heads(h_l, w_na, b)
        na_c = matmul_heads(h_c, w_na, b)
        rt_l = matmul(h_l, w_rt).reshape(b, l, -1)
        rt_c = matmul(h_c, w_rt).reshape(b, lc, -1)
        gq_l = matmul(h_l, w_gq).reshape(b, l, -1)
        gq_c = matmul(h_c, w_gq).reshape(b, lc, -1)
        hy_l = matmul(h_l, w_hy).reshape(b, l, -1)
        gate_l = matmul(h_l, w_gate)

        y_hy_l = hyena_mixer(hy_l, *hy_params, mats)
        y_na_l = na_attention(na_l, na_c, na_rpb[layer])
        y_rt_l, y_rt_c = retention_mixer(rt_l, rt_c, ret_log_decay[layer], ret_gn_w[layer], ret_tables, not last)
        y_gq_l, y_gq_c = gqa_mixer(gq_l, gq_c, gqa_q_norm_w[layer], gqa_k_norm_w[layer], gqa_tables, not last)

        m_l = merge_branches([y.reshape(b * l, -1) for y in (y_hy_l, y_na_l, y_rt_l, y_gq_l)], gate_l, wb, layer)
        x_l, h2 = matmul_residual_norm(m_l, wo, x_l, mod, 2, norm2_w[layer], mod, 3, 4, l, 0)
        act = matmul_swiglu(h2, w1, w3)
        if last:
            _, out = matmul_residual_norm(act, w2, x_l, mod, 5, final_norm_w, None, 0, 0, l, 0,
                                          write_x=False, out_dtype=F32)
        else:
            x_l, h_l = matmul_residual_norm(act, w2, x_l, mod, 5, norm1_w[layer + 1], mods[layer + 1], 0, 1, l, 0)

        if not last:
            n_na = na_c.shape[1] // 3
            hy_c = matmul(h_c, w_hy).reshape(b, lc, -1)
            gate_c = matmul(h_c, w_gate)
            y_hy_c = hyena_mixer(hy_c, *hy_params, mats)
            y_na_c = ctx_attention(na_c, 0, na_c, n_na, na_c, 2 * n_na, n_na, 1)
            m_c = merge_branches([y.reshape(b * lc, -1) for y in (y_hy_c, y_na_c, y_rt_c, y_gq_c)], gate_c, wb,
                                 layer)
            x_c, h2c = matmul_residual_norm(m_c, wo, x_c, mod, 2, norm2_w[layer], mod, 3, 4, None, ctx_row)
            x_c, h_c = matmul_residual_norm(matmul_swiglu(h2c, w1, w3), w2, x_c, mod, 5, norm1_w[layer + 1],
                                            mods[layer + 1], 0, 1, None, ctx_row)

    return out.reshape(b, l, d)
```

```python
import functools
import math

import numpy as np
import jax
import jax.numpy as jnp
from jax import lax
from jax.experimental import pallas as pl
from jax.experimental.pallas import tpu as pltpu

F32 = jnp.float32
BF16 = jnp.bfloat16

EPS = 1e-6
GRID_W = 64
HEAD_DIM = 128
N_BRANCH = 4
ROPE_BASE = 10000.0
HY_BANDS = 16
HY_DECAY_TARGET = 1e-2
HY_FAST_PCT = 0.3
HY_SLOW_PCT = 1.5
NA_WIN_R = 8
NA_WIN_C = 16
RET_DK = 64
RET_DV = 128
RET_CHUNK = 128
RET_ROPE_BASE = 10000.0
NEG_BIG = -1e30

LANES = 128
FFT_N2 = 128
VMEM_LIMIT = 56 * 1024 * 1024

NT_DIMS = (((1,), (1,)), ((), ()))
TN_DIMS = (((0,), (0,)), ((), ()))


def _cparams(*sem):
    return pltpu.CompilerParams(dimension_semantics=sem, vmem_limit_bytes=VMEM_LIMIT)


def _pick_tile(n, cap, mult):
    best = None
    for t in range(mult, min(n, cap) + 1, mult):
        if n % t == 0:
            best = t
    assert best is not None, (n, cap, mult)
    return best


def _swap32(x):
    n = x.shape[-1]
    lane = lax.broadcasted_iota(jnp.int32, x.shape, x.ndim - 1)
    up = pltpu.roll(x, n - 32, x.ndim - 1)
    down = pltpu.roll(x, 32, x.ndim - 1)
    return jnp.where((lane % 64) < 32, up, down)


def _silu(x):
    return x * jax.nn.sigmoid(x)


def _ada_kernel(c_ref, w_ref, b_ref, o_ref):
    a = _silu(c_ref[...]).astype(BF16)
    o_ref[...] = jnp.dot(a, w_ref[...].astype(BF16), preferred_element_type=F32) + b_ref[...]


def ada_mod(cs, ada_w, ada_b):
    depth, d, n = ada_w.shape
    tn = _pick_tile(n, 1536, LANES)
    return pl.pallas_call(
        _ada_kernel,
        grid=(depth, n // tn),
        in_specs=[pl.BlockSpec((8, d), lambda l, j: (0, 0)),
                  pl.BlockSpec((None, d, tn), lambda l, j: (l, 0, j)),
                  pl.BlockSpec((None, 1, tn), lambda l, j: (l, 0, j))],
        out_specs=pl.BlockSpec((None, 8, tn), lambda l, j: (l, 0, j)),
        out_shape=jax.ShapeDtypeStruct((depth, 8, n), F32),
        compiler_params=_cparams("parallel", "parallel"),
        name="ada_mod",
    )(cs, ada_w, ada_b.reshape(depth, 1, n))


def _mod_row(row_base, tiles_per_batch, axis):
    if tiles_per_batch is None:
        return row_base
    return row_base + pl.program_id(axis) // tiles_per_batch


def _norm_mod_kernel(x_ref, w_ref, sh_ref, sc_ref, o_ref, *, row_base, tiles_per_batch):
    x = x_ref[...]
    y = x * lax.rsqrt(jnp.mean(x * x, axis=-1, keepdims=True) + EPS) * w_ref[...]
    row = _mod_row(row_base, tiles_per_batch, 0)
    sh = sh_ref[pl.ds(row, 1), :]
    sc = sc_ref[pl.ds(row, 1), :]
    o_ref[...] = (y * (1.0 + sc) + sh).astype(o_ref.dtype)


def _norm_kernel(x_ref, w_ref, o_ref):
    x = x_ref[...]
    y = x * lax.rsqrt(jnp.mean(x * x, axis=-1, keepdims=True) + EPS) * w_ref[...]
    o_ref[...] = y.astype(o_ref.dtype)


def norm_mod(x, w, mod, sh_chunk, sc_chunk, rows_per_batch, row_base):
    m, d = x.shape
    tm = _pick_tile(m if rows_per_batch is None else rows_per_batch, 512, 8)
    tpb = None if rows_per_batch is None else rows_per_batch // tm
    kern = functools.partial(_norm_mod_kernel, row_base=row_base, tiles_per_batch=tpb)
    return pl.pallas_call(
        kern, grid=(m // tm,),
        in_specs=[pl.BlockSpec((tm, d), lambda i: (i, 0)),
                  pl.BlockSpec((1, d), lambda i: (0, 0)),
                  pl.BlockSpec((8, d), lambda i: (0, sh_chunk)),
                  pl.BlockSpec((8, d), lambda i: (0, sc_chunk))],
        out_specs=pl.BlockSpec((tm, d), lambda i: (i, 0)),
        out_shape=jax.ShapeDtypeStruct((m, d), BF16),
        compiler_params=_cparams("parallel"),
        name="norm_mod",
    )(x, w.reshape(1, d), mod, mod)


def final_norm(x, w):
    m, d = x.shape
    tm = _pick_tile(m, 512, 8)
    return pl.pallas_call(
        _norm_kernel, grid=(m // tm,),
        in_specs=[pl.BlockSpec((tm, d), lambda i: (i, 0)), pl.BlockSpec((1, d), lambda i: (0, 0))],
        out_specs=pl.BlockSpec((tm, d), lambda i: (i, 0)),
        out_shape=jax.ShapeDtypeStruct((m, d), F32),
        compiler_params=_cparams("parallel"),
        name="final_norm",
    )(x, w.reshape(1, d))


def _mm_kernel(a_ref, w_ref, o_ref):
    o_ref[...] = jnp.dot(a_ref[...], w_ref[...], preferred_element_type=F32).astype(o_ref.dtype)


def _mm_heads_kernel(a_ref, w_ref, o_ref):
    acc = jnp.dot(a_ref[...], w_ref[...], preferred_element_type=F32)
    for h in range(o_ref.shape[0]):
        o_ref[h] = acc[:, h * HEAD_DIM:(h + 1) * HEAD_DIM].astype(o_ref.dtype)


def _mm_swiglu_kernel(a_ref, w1_ref, w3_ref, o_ref):
    a = a_ref[...]
    u = jnp.dot(a, w1_ref[...], preferred_element_type=F32)
    g = jnp.dot(a, w3_ref[...], preferred_element_type=F32)
    o_ref[...] = (_silu(u) * g).astype(o_ref.dtype)


def _mm_residual_norm_kernel(*refs, row_base, tiles_per_batch, n_k, modulate, write_x):
    it = iter(refs)
    a_ref, w_ref, x_ref, g_ref, nw_ref = next(it), next(it), next(it), next(it), next(it)
    sh_ref, sc_ref = (next(it), next(it)) if modulate else (None, None)
    ox_ref = next(it) if write_x else None
    oh_ref = next(it)
    acc_ref = next(it) if n_k > 1 else None
    kk = pl.program_id(1)
    part = jnp.dot(a_ref[...], w_ref[...], preferred_element_type=F32)

    def finish(acc):
        row = _mod_row(row_base, tiles_per_batch, 0)
        xn = x_ref[...] + g_ref[pl.ds(row, 1), :] * acc
        if write_x:
            ox_ref[...] = xn
        y = xn * lax.rsqrt(jnp.mean(xn * xn, axis=-1, keepdims=True) + EPS) * nw_ref[...]
        if modulate:
            y = y * (1.0 + sc_ref[pl.ds(row, 1), :]) + sh_ref[pl.ds(row, 1), :]
        oh_ref[...] = y.astype(oh_ref.dtype)

    if n_k == 1:
        finish(part)
    else:
        @pl.when(kk == 0)
        def _():
            acc_ref[...] = part

        @pl.when(jnp.logical_and(kk > 0, kk < n_k - 1))
        def _():
            acc_ref[...] += part

        @pl.when(kk == n_k - 1)
        def _():
            finish(acc_ref[...] + part)


class WCols:
    def __init__(self, arr, layer, off, n):
        self.arr, self.layer, self.off, self.n = arr, layer, off, n

    def tile(self, cap):
        tn = max(t for t in range(LANES, min(self.n, cap) + 1, LANES) if self.n % t == 0 and self.off % t == 0)
        return tn

    def spec(self, tn):
        layer, base, k = self.layer, self.off // tn, self.arr.shape[1]
        return pl.BlockSpec((None, k, tn), lambda j, i: (layer, 0, base + j))


def matmul(a, w, out_dtype=BF16):
    m, k = a.shape
    n = w.n
    tm = _pick_tile(m, 512, 16)
    tn = w.tile(2048)
    return pl.pallas_call(
        _mm_kernel, grid=(n // tn, m // tm),
        in_specs=[pl.BlockSpec((tm, k), lambda j, i: (i, 0)), w.spec(tn)],
        out_specs=pl.BlockSpec((tm, tn), lambda j, i: (i, j)),
        out_shape=jax.ShapeDtypeStruct((m, n), out_dtype),
        compiler_params=_cparams("parallel", "parallel"),
        name="matmul",
    )(a, w.arr)


def matmul_heads(a, w, batch):
    m, k = a.shape
    n = w.n
    lb = m // batch
    tm = _pick_tile(lb, 512, 16)
    tn = w.tile(1536)
    tpb = lb // tm
    nh = tn // HEAD_DIM
    return pl.pallas_call(
        _mm_heads_kernel, grid=(n // tn, m // tm),
        in_specs=[pl.BlockSpec((tm, k), lambda j, i: (i, 0)), w.spec(tn)],
        out_specs=pl.BlockSpec((None, nh, tm, HEAD_DIM), lambda j, i: (i // tpb, j, i % tpb, 0)),
        out_shape=jax.ShapeDtypeStruct((batch, n // HEAD_DIM, lb, HEAD_DIM), BF16),
        compiler_params=_cparams("parallel", "parallel"),
        name="matmul_heads",
    )(a, w.arr)


def matmul_swiglu(a, w1, w3):
    m, k = a.shape
    n = w1.n
    tm = _pick_tile(m, 1024, 16)
    tn = min(w1.tile(704), w3.tile(704))
    return pl.pallas_call(
        _mm_swiglu_kernel, grid=(n // tn, m // tm),
        in_specs=[pl.BlockSpec((tm, k), lambda j, i: (i, 0)), w1.spec(tn), w3.spec(tn)],
        out_specs=pl.BlockSpec((tm, tn), lambda j, i: (i, j)),
        out_shape=jax.ShapeDtypeStruct((m, n), BF16),
        compiler_params=_cparams("parallel", "parallel"),
        name="matmul_swiglu",
    )(a, w1.arr, w3.arr)


def _mm_residual_kernel(a_ref, w_ref, x_ref, g_ref, o_ref, *, row_base, tiles_per_batch):
    acc = jnp.dot(a_ref[...], w_ref[...], preferred_element_type=F32)
    row = _mod_row(row_base, tiles_per_batch, 1)
    o_ref[...] = x_ref[...] + g_ref[pl.ds(row, 1), :] * acc


def matmul_residual(a, w, x, mod, gate_chunk, rows_per_batch, row_base):
    m, k = a.shape
    n = w.n
    tm = _pick_tile(m if rows_per_batch is None else rows_per_batch, 512, 16)
    tn = w.tile(512)
    tpb = None if rows_per_batch is None else rows_per_batch // tm
    kern = functools.partial(_mm_residual_kernel, row_base=row_base, tiles_per_batch=tpb)
    gblk = gate_chunk * (n // tn)
    return pl.pallas_call(
        kern, grid=(n // tn, m // tm),
        in_specs=[pl.BlockSpec((tm, k), lambda j, i: (i, 0)),
                  w.spec(tn),
                  pl.BlockSpec((tm, tn), lambda j, i: (i, j)),
                  pl.BlockSpec((8, tn), lambda j, i: (0, gblk + j))],
        out_specs=pl.BlockSpec((tm, tn), lambda j, i: (i, j)),
        out_shape=jax.ShapeDtypeStruct((m, n), F32),
        compiler_params=_cparams("parallel", "parallel"),
        name="matmul_residual",
    )(a, w.arr, x, mod)


RESIDUAL_K_TILE = 2048


def matmul_residual_norm(a, w, x, mod, gate_chunk, norm_w, next_mod, sh_chunk, sc_chunk, rows_per_batch, row_base,
                         write_x=True, out_dtype=BF16):
    m, k = a.shape
    n = w.n
    assert w.off == 0 and n == w.arr.shape[2]
    tm = _pick_tile(m if rows_per_batch is None else rows_per_batch, 512, 16)
    tk = max(t for t in range(LANES, min(k, RESIDUAL_K_TILE) + 1, LANES) if k % t == 0)
    n_k = k // tk
    tpb = None if rows_per_batch is None else rows_per_batch // tm
    modulate = next_mod is not None
    kern = functools.partial(_mm_residual_norm_kernel, row_base=row_base, tiles_per_batch=tpb, n_k=n_k,
                             modulate=modulate, write_x=write_x)
    layer = w.layer
    row_spec = pl.BlockSpec((tm, n), lambda i, kk: (i, 0))
    in_specs = [pl.BlockSpec((tm, tk), lambda i, kk: (i, kk)),
                pl.BlockSpec((None, tk, n), lambda i, kk: (layer, kk, 0)),
                row_spec,
                pl.BlockSpec((8, n), lambda i, kk: (0, gate_chunk)),
                pl.BlockSpec((1, n), lambda i, kk: (0, 0))]
    args = [a, w.arr, x, mod, norm_w.reshape(1, n)]
    if modulate:
        in_specs += [pl.BlockSpec((8, n), lambda i, kk: (0, sh_chunk)),
                     pl.BlockSpec((8, n), lambda i, kk: (0, sc_chunk))]
        args += [next_mod, next_mod]
    out_specs = ([row_spec] if write_x else []) + [row_spec]
    out_shape = ([jax.ShapeDtypeStruct((m, n), F32)] if write_x else []) + [jax.ShapeDtypeStruct((m, n), out_dtype)]
    outs = pl.pallas_call(
        kern, grid=(m // tm, n_k),
        in_specs=in_specs, out_specs=out_specs, out_shape=out_shape,
        scratch_shapes=[pltpu.VMEM((tm, n), F32)] if n_k > 1 else [],
        compiler_params=_cparams("parallel", "arbitrary"),
        name="matmul_residual_norm",
    )(*args)
    return (outs[0], outs[1]) if write_x else (None, outs[0])


def _merge_kernel(y0, y1, y2, y3, g0, g1, g2, g3, wb_ref, o_ref):
    acc = None
    for i, (y, g) in enumerate(((y0, g0), (y1, g1), (y2, g2), (y3, g3))):
        gate = 0.5 * jnp.tanh(0.5 * g[...].astype(F32)) + 0.5
        t = gate * jnp.dot(y[...], wb_ref[i], preferred_element_type=F32)
        acc = t if acc is None else acc + t
    o_ref[...] = acc.astype(o_ref.dtype)


def merge_branches(ys, gate_pre, wb, layer):
    m, w = ys[0].shape
    d = wb.shape[3]
    tm = _pick_tile(m, 512, 16)
    tn = _pick_tile(d, 1024, LANES)
    nb = d // tn
    y_spec = pl.BlockSpec((tm, w), lambda j, i: (i, 0))
    g_specs = [pl.BlockSpec((tm, tn), functools.partial(lambda j, i, b: (i, b * nb + j), b=b))
               for b in range(N_BRANCH)]
    return pl.pallas_call(
        _merge_kernel, grid=(nb, m // tm),
        in_specs=[y_spec] * N_BRANCH + g_specs + [pl.BlockSpec((None, N_BRANCH, w, tn),
                                                               lambda j, i: (layer, 0, 0, j))],
        out_specs=pl.BlockSpec((tm, tn), lambda j, i: (i, j)),
        out_shape=jax.ShapeDtypeStruct((m, d), BF16),
        compiler_params=_cparams("parallel", "parallel"),
        name="merge_branches",
    )(*ys, gate_pre, gate_pre, gate_pre, gate_pre, wb)


FFT_ROWS = 16


def _store_cols(o_ref, lead, y, n2):
    cw = y.shape[1]
    tiles = jnp.swapaxes(y.reshape(y.shape[0] // n2, n2, cw), 0, 1)
    for j in range(n2):
        o_ref[lead + (slice(None), slice(j * cw, (j + 1) * cw))] = tiles[j]


def _short_conv_kernel(x_ref, prev_ref, next_ref, w_ref, b_ref, *o_refs, n_tiles, col_n2):
    i = pl.program_id(1)
    t = x_ref.shape[0]
    halo = prev_ref.shape[0]
    cw = x_ref.shape[1] // len(o_refs)
    row = lax.broadcasted_iota(jnp.int32, (t, cw), 0)
    for n, o_ref in enumerate(o_refs):
        cols = slice(n * cw, (n + 1) * cw)
        x = x_ref[:, cols].astype(F32)
        before = jnp.where(i > 0, prev_ref[:, cols].astype(F32)[halo - 1:halo, :], 0.0)
        after = jnp.where(i < n_tiles - 1, next_ref[:, cols].astype(F32)[0:1, :], 0.0)
        xm1 = jnp.where(row == 0, before, pltpu.roll(x, 1, 0))
        xp1 = jnp.where(row == t - 1, after, pltpu.roll(x, t - 1, 0))
        w = w_ref[:, cols]
        y = (w[0:1, :] * xm1 + w[1:2, :] * x + w[2:3, :] * xp1 + b_ref[:, cols]).astype(o_ref.dtype)
        if col_n2 is None:
            o_ref[...] = y
        else:
            _store_cols(o_ref, (), y, col_n2)


def short_conv(p, conv_w, conv_b, n_out, col_n2=None):
    b, l, c = p.shape
    cw = c // n_out
    halo = 16
    t = _pick_tile(l, 512, halo) if col_n2 is None else FFT_ROWS * col_n2
    n_tiles = l // t
    hb = t // halo
    kern = functools.partial(_short_conv_kernel, n_tiles=n_tiles, col_n2=col_n2)
    if col_n2 is None:
        out_specs = [pl.BlockSpec((None, t, cw), lambda bi, i: (bi, i, 0))] * n_out
        out_shape = [jax.ShapeDtypeStruct((b, l, cw), BF16)] * n_out
    else:
        out_specs = [pl.BlockSpec((None, FFT_ROWS, col_n2 * cw), lambda bi, i: (bi, i, 0))] * n_out
        out_shape = [jax.ShapeDtypeStruct((b, l // col_n2, col_n2 * cw), BF16)] * n_out
    return pl.pallas_call(
        kern, grid=(b, n_tiles),
        in_specs=[pl.BlockSpec((None, t, c), lambda bi, i: (bi, i, 0)),
                  pl.BlockSpec((None, halo, c), lambda bi, i: (bi, jnp.maximum(i * hb - 1, 0), 0)),
                  pl.BlockSpec((None, halo, c), lambda bi, i: (bi, jnp.minimum((i + 1) * hb, l // halo - 1), 0)),
                  pl.BlockSpec((8, c), lambda bi, i: (0, 0)),
                  pl.BlockSpec((1, c), lambda bi, i: (0, 0))],
        out_specs=out_specs, out_shape=out_shape,
        compiler_params=_cparams("parallel", "parallel"),
        name="short_conv",
    )(p, p, p, jnp.pad(conv_w, ((0, 8 - conv_w.shape[0]), (0, 0))), conv_b.reshape(1, c))


def _hy_filter_kernel(fv_ref, c2_ref, s2_ref, w1_ref, b1_ref, sf1_ref, w2_ref, b2_ref, sf2_ref, w3f_ref, w3b_ref,
                      dl_ref, f_ref, ss_ref, cb_ref, sb_ref, *, seq_len, col_n2):
    i = pl.program_id(0)
    tl = cb_ref.shape[0]
    hi = lax.Precision.HIGHEST
    step = 2.0 * math.pi / float(seq_len)
    r = lax.broadcasted_iota(jnp.int32, (tl, 1), 0)
    lane = lax.broadcasted_iota(jnp.int32, (1, LANES), 1)

    @pl.when(i == 0)
    def _():
        ang_r = (step * r.astype(F32)) * fv_ref[...]
        cb_ref[...] = jnp.cos(ang_r)
        sb_ref[...] = jnp.sin(ang_r)
        ss_ref[...] = jnp.zeros(ss_ref.shape, F32)

    base = (step * (i * tl).astype(F32)) * fv_ref[...]
    ca, sa = jnp.cos(base), jnp.sin(base)
    cb, sb = cb_ref[...], sb_ref[...]
    cos_j, sin_j = ca * cb - sa * sb, sa * cb + ca * sb
    cos_m, sin_m = c2_ref[...] * cos_j + s2_ref[...] * sin_j, s2_ref[...] * cos_j - c2_ref[...] * sin_j
    j = (i * tl + r).astype(F32)

    def mlp(t, cos_t, sin_t):
        t_norm = t / float(max(seq_len - 1, 1))
        feat = jnp.where(lane == 0, t_norm,
                         jnp.where(lane <= HY_BANDS, cos_t, jnp.where(lane <= 2 * HY_BANDS, -sin_t, 0.0)))
        z = jnp.sin(sf1_ref[...] * (jnp.dot(feat, w1_ref[...], precision=hi, preferred_element_type=F32)
                                    + b1_ref[...]))
        z = jnp.sin(sf2_ref[...] * (jnp.dot(z, w2_ref[...], precision=hi, preferred_element_type=F32)
                                    + b2_ref[...]))
        return z.astype(BF16), t_norm

    zf, tnf = mlp(j, cos_j, sin_j)
    zb, tnb = mlp(float(seq_len) - j, cos_m, sin_m)
    cw = w3f_ref.shape[1] // f_ref.shape[0]
    for o in range(f_ref.shape[0]):
        cols = slice(o * cw, (o + 1) * cw)
        hf = jnp.dot(zf, w3f_ref[:, cols], preferred_element_type=F32) * jnp.exp(-tnf * dl_ref[:, cols])
        hb = jnp.dot(zb, w3b_ref[:, cols], preferred_element_type=F32) * jnp.exp(-tnb * dl_ref[:, cols])
        hb = jnp.where(j > 0.0, hb, 0.0)
        ss_ref[:, cols] += jnp.sum(hf * hf, axis=0, keepdims=True) + jnp.sum(hb * hb, axis=0, keepdims=True)
        for s, h in enumerate((hf, hb)):
            if col_n2 is None:
                f_ref[o, s] = h.astype(f_ref.dtype)
            else:
                _store_cols(f_ref, (o, s), h.astype(f_ref.dtype), col_n2)


def hy_filters(seq_len, w1, b1, w2, b2, w3, sin_freq, hy_w, col_n2=None):
    emb, ffn = w1.shape
    cw = 2 * hy_w
    f = np.linspace(1e-4, HY_BANDS - 1, HY_BANDS)
    fv = np.zeros((1, LANES), np.float32)
    fv[0, 1:1 + HY_BANDS] = f
    fv[0, 1 + HY_BANDS:1 + 2 * HY_BANDS] = f
    c2 = np.cos(2.0 * np.pi * fv.astype(np.float64)).astype(np.float32)
    s2 = np.sin(2.0 * np.pi * fv.astype(np.float64)).astype(np.float32)
    w1p = jnp.pad(w1, ((0, LANES - emb), (0, 0)))
    w3r = w3.reshape(ffn, 2, 2, hy_w)
    w3f = w3r[:, :, 0, :].reshape(ffn, cw).astype(BF16)
    w3b = w3r[:, :, 1, :].reshape(ffn, cw).astype(BF16)
    deltas = np.abs(np.linspace(math.log(HY_DECAY_TARGET) / HY_SLOW_PCT,
                                math.log(HY_DECAY_TARGET) / HY_FAST_PCT, hy_w)).astype(np.float32)
    dl = jnp.asarray(np.tile(deltas, 2).reshape(1, cw))
    const = lambda i: (0, 0)
    if col_n2 is None:
        tl = _pick_tile(seq_len, 512, 16)
        f_spec = pl.BlockSpec((2, 2, tl, hy_w), lambda i: (0, 0, i, 0))
        f_shape = jax.ShapeDtypeStruct((2, 2, seq_len, hy_w), BF16)
    else:
        tl = FFT_ROWS * col_n2
        f_spec = pl.BlockSpec((2, 2, FFT_ROWS, col_n2 * hy_w), lambda i: (0, 0, i, 0))
        f_shape = jax.ShapeDtypeStruct((2, 2, seq_len // col_n2, col_n2 * hy_w), BF16)
    kern = functools.partial(_hy_filter_kernel, seq_len=seq_len, col_n2=col_n2)
    return pl.pallas_call(
        kern, grid=(seq_len // tl,),
        in_specs=[pl.BlockSpec((1, LANES), const), pl.BlockSpec((1, LANES), const), pl.BlockSpec((1, LANES), const),
                  pl.BlockSpec((LANES, ffn), const),
                  pl.BlockSpec((1, ffn), const), pl.BlockSpec((1, ffn), const),
                  pl.BlockSpec((ffn, ffn), const), pl.BlockSpec((1, ffn), const), pl.BlockSpec((1, ffn), const),
                  pl.BlockSpec((ffn, cw), const), pl.BlockSpec((ffn, cw), const), pl.BlockSpec((1, cw), const)],
        out_specs=[f_spec, pl.BlockSpec((1, cw), const)],
        out_shape=[f_shape, jax.ShapeDtypeStruct((1, cw), F32)],
        scratch_shapes=[pltpu.VMEM((tl, LANES), F32), pltpu.VMEM((tl, LANES), F32)],
        compiler_params=_cparams("arbitrary"),
        name="hy_filters",
    )(jnp.asarray(fv), jnp.asarray(c2), jnp.asarray(s2), w1p, b1.reshape(1, ffn), sin_freq[0].reshape(1, ffn), w2,
      b2.reshape(1, ffn), sin_freq[1].reshape(1, ffn), w3f, w3b, dl)


def _phase_mats(phase_num, denom, conj):
    ang = np.pi * (phase_num % (2 * denom)).astype(np.float64) / denom
    cr, ci = np.cos(ang), (np.sin(ang) if conj else -np.sin(ang))
    return cr, ci


def _stage1_mats(n1):
    k = np.arange(n1)[:, None]
    n = np.arange(n1)[None, :]
    w1d, w1f, wfin = [], [], []
    for v in (0, 1):
        cr, ci = _phase_mats(2 * k * n + v * n, n1, conj=False)
        w1d.append(np.block([[cr, -ci], [ci, cr]]))
        sgn = 1.0 if v == 0 else -1.0
        w1f.append(np.block([[cr, sgn * cr], [ci, sgn * ci]]))
        cri, cii = _phase_mats(2 * n.T * k.T + v * n.T, n1, conj=True)
        wfin.append(np.block([[cri, -cii], [cii, cri]]))
    return (np.concatenate(w1d, 0).astype(np.float32), np.concatenate(w1f, 0).astype(np.float32),
            np.concatenate(wfin, 1).astype(np.float32))


def _stage2_mats(n1, n2):
    l = n1 * n2
    k2 = np.arange(n2)[:, None]
    nn = np.arange(n2)[None, :]
    base = (2 * n1 * k2 * nn) % (2 * l)
    br, bi = np.cos(np.pi * base / l), -np.sin(np.pi * base / l)
    k1 = np.arange(n1)[None, :, None]
    v = np.arange(2)[:, None, None]
    tw = (2 * np.arange(n2)[None, None, :] * k1 + v * np.arange(n2)[None, None, :]) % (2 * l)
    tr, ti = np.cos(np.pi * tw / l), -np.sin(np.pi * tw / l)
    br, bi, tr, ti = (jnp.asarray(a, F32) for a in (br, bi, tr, ti))
    cr = br[None, None] * tr[:, :, None, :] - bi[None, None] * ti[:, :, None, :]
    ci = br[None, None] * ti[:, :, None, :] + bi[None, None] * tr[:, :, None, :]
    fwd = jnp.concatenate([jnp.concatenate([cr, -ci], -1), jnp.concatenate([ci, cr], -1)], -2)
    crt, cit = jnp.swapaxes(cr, -1, -2), -jnp.swapaxes(ci, -1, -2)
    inv = jnp.concatenate([jnp.concatenate([crt, -cit], -1), jnp.concatenate([cit, crt], -1)], -2)
    return fwd.astype(BF16), inv.astype(BF16)


def _fft_stage1_kernel(z_ref, w_ref, o_ref):
    n1 = z_ref.shape[1]
    nj, cw = o_ref.shape[3], o_ref.shape[4]
    x = z_ref[...].reshape(2 * n1, nj * cw)
    for plane in range(4):
        y = jnp.dot(w_ref[plane * n1:(plane + 1) * n1, :], x, preferred_element_type=F32).astype(o_ref.dtype)
        tiles = jnp.stack([y[:, j * cw:(j + 1) * cw] for j in range(nj)], axis=0)
        o_ref[plane // 2, plane % 2] = jnp.swapaxes(tiles, 0, 1)


FFT_COLS = 2048
FFT_S1_N2 = 16


def fft_stage1(z, w1, n1, n2):
    grp = z.shape[0]
    w = z.shape[3] // n2
    nj = _pick_tile(n2, FFT_S1_N2, 16)
    return pl.pallas_call(
        _fft_stage1_kernel, grid=(grp, n2 // nj),
        in_specs=[pl.BlockSpec((None, 2, n1, nj * w), lambda gi, j: (gi, 0, 0, j)),
                  pl.BlockSpec((4 * n1, 2 * n1), lambda gi, j: (0, 0))],
        out_specs=pl.BlockSpec((None, 2, 2, n1, nj, w), lambda gi, j: (gi, 0, 0, 0, j, 0)),
        out_shape=jax.ShapeDtypeStruct((grp, 2, 2, n1, n2, w), BF16),
        compiler_params=_cparams("parallel", "parallel"),
        name="fft_stage1",
    )(z, w1)


def _fft_mid_kernel(a_ref, f_ref, ss_ref, wf_ref, wi_ref, o_ref, *, scale):
    kb, n2, cw = a_ref.shape[1], a_ref.shape[2], a_ref.shape[3]
    nrm = lax.rsqrt(ss_ref[...] + EPS) * scale
    us = []
    for kk in range(kb):
        wf = wf_ref[kk]
        t = jnp.dot(wf, a_ref[:, kk].reshape(2 * n2, cw), preferred_element_type=F32)
        g = jnp.dot(wf, f_ref[:, kk].reshape(2 * n2, cw), preferred_element_type=F32) * nrm
        tr, ti, gr, gi = t[:n2], t[n2:], g[:n2], g[n2:]
        p = jnp.concatenate([tr * gr - ti * gi, tr * gi + ti * gr], axis=0).astype(BF16)
        us.append(jnp.dot(wi_ref[kk], p, preferred_element_type=F32).astype(o_ref.dtype))
    for r in range(2):
        tiles = jnp.swapaxes(jnp.stack([u[r * n2:(r + 1) * n2] for u in us], axis=0), 0, 1)
        for j in range(n2):
            o_ref[r, :, j * cw:(j + 1) * cw] = tiles[j]


FFT_MID_K1 = 16


def fft_mid(a, af, ss, wf, wi, order, n1, n2):
    cw = a.shape[-1]
    kb = _pick_tile(n1, FFT_MID_K1, 16)
    kern = functools.partial(_fft_mid_kernel, scale=1.0 / (2.0 * n1 * n2))
    mat_spec = pl.BlockSpec((None, kb, 2 * n2, 2 * n2), lambda v, k: (v, k, 0, 0))
    return pl.pallas_call(
        kern, grid=(2, n1 // kb),
        in_specs=[pl.BlockSpec((None, 2, kb, n2, cw), lambda v, k: (v, 0, k, 0, 0)),
                  pl.BlockSpec((None, None, 2, kb, n2, cw), lambda v, k: (order, v, 0, k, 0, 0)),
                  pl.BlockSpec((1, cw), lambda v, k: (0, order)),
                  mat_spec, mat_spec],
        out_specs=pl.BlockSpec((None, 2, kb, n2 * cw), lambda v, k: (v, 0, k, 0)),
        out_shape=jax.ShapeDtypeStruct((2, 2, n1, n2 * cw), BF16),
        compiler_params=_cparams("parallel", "parallel"),
        name="fft_mid",
    )(a, af, ss, wf, wi)


def _fft_final_kernel(u_ref, w_ref, gate_ref, z_ref, bias_ref, o_ref):
    n1 = gate_ref.shape[1]
    cw = gate_ref.shape[2]
    u = u_ref[...].reshape(4 * n1, cw)
    y = jnp.dot(w_ref[...], u, preferred_element_type=F32).reshape(2, n1, cw)
    z = z_ref[...].astype(F32)
    o_ref[...] = (gate_ref[...].astype(F32) * (y + bias_ref[...] * z)).astype(o_ref.dtype)


def fft_final(u, wfin, gate, z, bias, n1, n2):
    cols = z.shape[2]
    cw = cols // n2
    tc = _pick_tile(cols, FFT_COLS, cw)
    return pl.pallas_call(
        _fft_final_kernel, grid=(cols // tc,),
        in_specs=[pl.BlockSpec((2, 2, n1, tc), lambda j: (0, 0, 0, j)),
                  pl.BlockSpec((2 * n1, 4 * n1), lambda j: (0, 0)),
                  pl.BlockSpec((2, n1, tc), lambda j: (0, 0, j)),
                  pl.BlockSpec((2, n1, tc), lambda j: (0, 0, j)),
                  pl.BlockSpec((1, tc), lambda j: (0, 0))],
        out_specs=pl.BlockSpec((2, n1, tc), lambda j: (0, 0, j)),
        out_shape=jax.ShapeDtypeStruct((2, n1, cols), BF16),
        compiler_params=_cparams("parallel"),
        name="fft_final",
    )(u, wfin, gate, z, jnp.tile(bias.reshape(1, cw), (1, tc // cw)))


def _hy_dense_kernel(z_ref, gate_ref, f_ref, ss_ref, bias_ref, wf_ref, wi_ref, o_ref):
    lc = z_ref.shape[1]
    cw = z_ref.shape[2]
    nrm = lax.rsqrt(ss_ref[...] + EPS) * (1.0 / (2.0 * lc))
    f1, f2 = f_ref[0].astype(F32), f_ref[1].astype(F32)
    filt = (((f1 + f2) * nrm).astype(BF16), ((f1 - f2) * nrm).astype(BF16))
    x = z_ref[...].reshape(2 * lc, cw)
    acc = None
    for v in (0, 1):
        wf = wf_ref[v]
        t = jnp.dot(wf, x, preferred_element_type=F32)
        g = jnp.dot(wf[:, :lc], filt[v], preferred_element_type=F32)
        tr, ti, gr, gi = t[:lc], t[lc:], g[:lc], g[lc:]
        p = jnp.concatenate([tr * gr - ti * gi, tr * gi + ti * gr], axis=0).astype(BF16)
        u = jnp.dot(wi_ref[v], p, preferred_element_type=F32)
        acc = u if acc is None else acc + u
    y = acc.reshape(2, lc, cw)
    o_ref[...] = (gate_ref[...].astype(F32) * (y + bias_ref[...] * z_ref[...].astype(F32))).astype(o_ref.dtype)


def hy_dense_conv(z_arr, gate_arr, f, ss, order, bias, wf, wi):
    lc, cw = z_arr.shape[1], z_arr.shape[2]
    return pl.pallas_call(
        _hy_dense_kernel, grid=(1,),
        in_specs=[pl.BlockSpec((2, lc, cw), lambda i: (0, 0, 0)),
                  pl.BlockSpec((2, lc, cw), lambda i: (0, 0, 0)),
                  pl.BlockSpec((None, 2, lc, cw), lambda i: (order, 0, 0, 0)),
                  pl.BlockSpec((1, cw), lambda i: (0, order)),
                  pl.BlockSpec((1, cw), lambda i: (0, 0)),
                  pl.BlockSpec((2, None, 2 * lc, 2 * lc), lambda i: (0, 0, 0, 0)),
                  pl.BlockSpec((2, None, 2 * lc, 2 * lc), lambda i: (0, 0, 0, 0))],
        out_specs=pl.BlockSpec((2, lc, cw), lambda i: (0, 0, 0)),
        out_shape=jax.ShapeDtypeStruct((2, lc, cw), BF16),
        compiler_params=_cparams("arbitrary"),
        name="hy_dense_conv",
    )(z_arr, gate_arr, f, ss, bias.reshape(1, cw), wf, wi)


def hyena_mixer(p, conv_w, conv_b, w1, b1, w2, b2, w3, sin_freq, bias, mats):
    b, l, c3 = p.shape
    assert b == 2, "batch elements are packed as the re/im parts of one complex signal"
    hy_w = c3 // 3
    if l <= 2 * FFT_N2:
        v, x1, x2 = short_conv(p, conv_w, conv_b, 3)
        f, ss = hy_filters(l, w1, b1, w2, b2, w3, sin_freq, hy_w)
        wf, wi = mats["dense"]
        z2 = hy_dense_conv(v, x1, f, ss, 0, bias[0], wf, wi)
        return hy_dense_conv(z2, x2, f, ss, 1, bias[1], wf, wi)
    n2 = FFT_N2
    n1 = l // n2
    w1d, w1f, wfin, wf, wi = mats["fft"]
    v, x1, x2 = short_conv(p, conv_w, conv_b, 3, col_n2=n2)
    f, ss = hy_filters(l, w1, b1, w2, b2, w3, sin_freq, hy_w, col_n2=n2)
    af = fft_stage1(f, w1f, n1, n2)
    a = fft_stage1(v[None], w1d, n1, n2)[0]
    z2 = fft_final(fft_mid(a, af, ss, wf, wi, 0, n1, n2), wfin, x1, v, bias[0], n1, n2)
    a = fft_stage1(z2[None], w1d, n1, n2)[0]
    y = fft_final(fft_mid(a, af, ss, wf, wi, 1, n1, n2), wfin, x2, z2, bias[1], n1, n2)
    return y.reshape(2, l, hy_w)


def hyena_mats(l_lat, l_ctx):
    n1 = l_lat // FFT_N2
    w1d, w1f, wfin = _stage1_mats(n1)
    wf, wi = _stage2_mats(n1, FFT_N2)
    mats = {"fft": (jnp.asarray(w1d, BF16), jnp.asarray(w1f, BF16), jnp.asarray(wfin, BF16), wf, wi)}
    mats["dense"] = _stage2_mats(1, l_ctx)
    return mats


def _ctx_attn_kernel(q_ref, k_ref, v_ref, o_ref):
    s = lax.dot_general(q_ref[...], k_ref[...], NT_DIMS, preferred_element_type=F32)
    m = jnp.max(s, axis=-1, keepdims=True)
    p = jnp.exp(s - m)
    l = jnp.sum(p, axis=-1, keepdims=True)
    o = jnp.dot(p.astype(BF16), v_ref[...], preferred_element_type=F32) / l
    o_ref[...] = o.astype(o_ref.dtype)


def ctx_attention(qa, q_off, ka, k_off, va, v_off, n_q_heads, group):
    b, _, lc, d = qa.shape
    return pl.pallas_call(
        _ctx_attn_kernel, grid=(b, n_q_heads),
        in_specs=[pl.BlockSpec((None, None, lc, d), lambda bi, h: (bi, q_off + h, 0, 0)),
                  pl.BlockSpec((None, None, lc, d), lambda bi, h: (bi, k_off + h // group, 0, 0)),
                  pl.BlockSpec((None, None, lc, d), lambda bi, h: (bi, v_off + h // group, 0, 0))],
        out_specs=pl.BlockSpec((None, lc, d), lambda bi, h: (bi, 0, h)),
        out_shape=jax.ShapeDtypeStruct((b, lc, n_q_heads * d), BF16),
        compiler_params=_cparams("parallel", "parallel"),
        name="ctx_attention",
    )(qa, ka, va)


NA_TILE_ROWS = 4
NA_KEY_ROWS = 12
NA_HEADS_PER_STEP = 2


def _na_geometry(n_rows):
    wr = min(NA_WIN_R, n_rows)
    nt = n_rows // NA_TILE_ROWS
    sigs = []
    for t in range(nt):
        rt = t * NA_TILE_ROWS
        w0 = int(np.clip(rt - wr // 2, 0, n_rows - NA_KEY_ROWS))
        r = rt + np.arange(NA_TILE_ROWS)
        r0 = np.clip(r - wr // 2, 0, n_rows - wr)
        sigs.append((w0 - rt, tuple((r0 - w0).tolist())))
    classes = (sigs[0], sigs[1], sigs[-1])
    for t, s in enumerate(sigs):
        assert s == classes[0 if t == 0 else (2 if t == nt - 1 else 1)], "tile does not match its bias class"
    return wr, nt, classes


def _na_bias_tables(rpb, n_rows):
    wr, _, classes = _na_geometry(n_rows)
    cols = np.arange(GRID_W)
    c0 = np.clip(cols - NA_WIN_C // 2, 0, GRID_W - NA_WIN_C)
    col_ok = (cols[None, :] >= c0[:, None]) & (cols[None, :] < c0[:, None] + NA_WIN_C)
    col_idx = np.clip(cols[None, :] - cols[:, None] + NA_WIN_C - 1, 0, 2 * NA_WIN_C - 2)
    col_sel = (col_idx[None] == np.arange(2 * NA_WIN_C - 1)[:, None, None]).astype(np.float32)
    row_sel, oks = [], []
    for off, rel_r0 in classes:
        qi = np.arange(NA_TILE_ROWS)[:, None]
        kw = np.arange(NA_KEY_ROWS)[None, :]
        r0 = np.asarray(rel_r0)[:, None]
        row_ok = (kw >= r0) & (kw < r0 + wr)
        row_idx = np.clip(kw + off - qi + NA_WIN_R - 1, 0, 2 * NA_WIN_R - 2)
        row_sel.append((row_idx[None] == np.arange(2 * NA_WIN_R - 1)[:, None, None]).astype(np.float32))
        oks.append(row_ok[:, None, :, None] & col_ok[None, :, None, :])
    vals = jnp.einsum("hab,saqk,bcd->hsqckd", rpb.astype(F32), jnp.asarray(np.stack(row_sel)),
                      jnp.asarray(col_sel), precision=lax.Precision.HIGHEST)
    tab = jnp.where(jnp.asarray(np.stack(oks))[None], vals, NEG_BIG)
    return tab.reshape(rpb.shape[0], 3, NA_TILE_ROWS * GRID_W, NA_KEY_ROWS * GRID_W)


def _na_kernel(q_ref, k_ref, v_ref, kc_ref, vc_ref, bias_ref, o_ref, *, n_rows, win_half):
    t = pl.program_id(2)
    w0 = jnp.clip(t * NA_TILE_ROWS - win_half, 0, n_rows - NA_KEY_ROWS)
    start = pl.multiple_of(w0 * GRID_W, GRID_W)
    outs = []
    for hh in range(q_ref.shape[0]):
        kwin = k_ref[hh, pl.ds(start, NA_KEY_ROWS * GRID_W), :]
        vwin = v_ref[hh, pl.ds(start, NA_KEY_ROWS * GRID_W), :]
        q = q_ref[hh]
        s_lat = lax.dot_general(q, kwin, NT_DIMS, preferred_element_type=F32) + bias_ref[hh]
        s_ctx = lax.dot_general(q, kc_ref[hh], NT_DIMS, preferred_element_type=F32)
        m = jnp.maximum(jnp.max(s_lat, axis=-1, keepdims=True), jnp.max(s_ctx, axis=-1, keepdims=True))
        p_lat = jnp.exp(s_lat - m)
        p_ctx = jnp.exp(s_ctx - m)
        l = jnp.sum(p_lat, axis=-1, keepdims=True) + jnp.sum(p_ctx, axis=-1, keepdims=True)
        o = (jnp.dot(p_lat.astype(BF16), vwin, preferred_element_type=F32)
             + jnp.dot(p_ctx.astype(BF16), vc_ref[hh], preferred_element_type=F32)) / l
        outs.append(o.astype(o_ref.dtype))
    o_ref[...] = jnp.concatenate(outs, axis=1)


def na_attention(pl_heads, pc_heads, rpb):
    b, h3, l, d = pl_heads.shape
    lc = pc_heads.shape[2]
    h = h3 // 3
    n_rows = l // GRID_W
    wr, nt, _ = _na_geometry(n_rows)
    bias = _na_bias_tables(rpb, n_rows)
    tq = NA_TILE_ROWS * GRID_W
    kw = NA_KEY_ROWS * GRID_W
    kern = functools.partial(_na_kernel, n_rows=n_rows, win_half=wr // 2)

    def cls(t):
        return jnp.where(t == 0, 0, jnp.where(t == nt - 1, 2, 1))

    hs = NA_HEADS_PER_STEP
    assert h % hs == 0
    hb = h // hs
    return pl.pallas_call(
        kern, grid=(b, hb, nt),
        in_specs=[pl.BlockSpec((None, hs, tq, d), lambda bi, hi, t: (bi, hi, t, 0)),
                  pl.BlockSpec((None, hs, l, d), lambda bi, hi, t: (bi, hb + hi, 0, 0)),
                  pl.BlockSpec((None, hs, l, d), lambda bi, hi, t: (bi, 2 * hb + hi, 0, 0)),
                  pl.BlockSpec((None, hs, lc, d), lambda bi, hi, t: (bi, hb + hi, 0, 0)),
                  pl.BlockSpec((None, hs, lc, d), lambda bi, hi, t: (bi, 2 * hb + hi, 0, 0)),
                  pl.BlockSpec((hs, None, tq, kw), lambda bi, hi, t: (hi, cls(t), 0, 0))],
        out_specs=pl.BlockSpec((None, tq, hs * d), lambda bi, hi, t: (bi, t, hi)),
        out_shape=jax.ShapeDtypeStruct((b, l, h * d), BF16),
        compiler_params=_cparams("parallel", "parallel", "parallel"),
        name="na_attention",
    )(pl_heads, pl_heads, pl_heads, pc_heads, pc_heads, bias)


def _ret_kernel(*refs, reverse, rope, finalize, n_heads):
    it = iter(refs)
    lg_ref, p_ref = next(it), next(it)
    cos_ref, sin_ref = (next(it), next(it)) if rope else (None, None)
    s0_ref = next(it)
    oprev_ref, gnw_ref = (next(it), next(it)) if finalize else (None, None)
    o_ref, sfin_ref, state_ref = next(it), next(it), next(it)

    i = pl.program_id(1)
    n_steps = pl.num_programs(1)

    @pl.when(i == 0)
    def _():
        state_ref[...] = s0_ref[...]

    c = RET_CHUNK
    ts = p_ref.shape[0]
    nc = ts // c
    dq = n_heads * RET_DK
    dv = n_heads * RET_DV
    x = p_ref[...]
    q = x[:, :dq].astype(F32)
    k = x[:, dq:2 * dq].astype(F32)
    v = x[:, 2 * dq:2 * dq + dv]
    if rope:
        q = q * cos_ref[...] + _swap32(q) * sin_ref[...]
        k = k * cos_ref[...] + _swap32(k) * sin_ref[...]
    jr = lax.broadcasted_iota(jnp.int32, (c, c), 0)
    jc = lax.broadcasted_iota(jnp.int32, (c, c), 1)
    rel = ((jc - jr) if reverse else (jr - jc)).astype(F32)
    jcol = lax.broadcasted_iota(jnp.int32, (c, 1), 0).astype(F32)
    order = range(nc - 1, -1, -1) if reverse else range(nc)
    blocks = [[None] * n_heads for _ in range(nc)]
    for h in range(n_heads):
        g = lg_ref[h]
        intra = jnp.where(rel >= 0.0, jnp.exp(g * jnp.maximum(rel, 0.0)), 0.0)
        if reverse:
            cross_f = jnp.exp(g * (float(c) - jcol))
            k_dec = jnp.exp(g * jcol)
        else:
            cross_f = jnp.exp(g * (jcol + 1.0))
            k_dec = jnp.exp(g * (float(c - 1) - jcol))
        chunk_decay = jnp.exp(g * float(c))
        state = state_ref[h]
        for ci in order:
            rows = slice(ci * c, (ci + 1) * c)
            qh = q[rows, h * RET_DK:(h + 1) * RET_DK].astype(BF16)
            kf = k[rows, h * RET_DK:(h + 1) * RET_DK]
            vh = v[rows, h * RET_DV:(h + 1) * RET_DV]
            scores = lax.dot_general(qh, kf.astype(BF16), NT_DIMS, preferred_element_type=F32) * intra
            inner = jnp.dot(scores.astype(BF16), vh, preferred_element_type=F32)
            cross = jnp.dot(qh, state.astype(BF16), preferred_element_type=F32) * cross_f
            blocks[ci][h] = inner + cross
            kv = lax.dot_general((kf * k_dec).astype(BF16), vh, TN_DIMS, preferred_element_type=F32)
            state = chunk_decay * state + kv
        state_ref[h] = state
    o = jnp.concatenate([jnp.concatenate(blocks[ci], axis=1) for ci in range(nc)], axis=0)
    if finalize:
        o = o + oprev_ref[...]
        gate = x[:, 2 * dq + dv:2 * dq + 2 * dv].astype(F32)
        normed = []
        for h in range(n_heads):
            oh = o[:, h * RET_DV:(h + 1) * RET_DV]
            normed.append(oh * lax.rsqrt(jnp.mean(oh * oh, axis=-1, keepdims=True) + EPS))
        o = jnp.concatenate(normed, axis=1) * gnw_ref[...] * _silu(gate)
    o_ref[...] = o.astype(o_ref.dtype)

    @pl.when(i == n_steps - 1)
    def _():
        sfin_ref[...] = state_ref[...]


def retention_pass(p, log_g, s0, reverse, rope_tables=None, o_prev=None, gn_w=None):
    b, l, w = p.shape
    n_heads = w // (2 * RET_DK + 2 * RET_DV)
    dq, dv = n_heads * RET_DK, n_heads * RET_DV
    ts = _pick_tile(l, 512, RET_CHUNK)
    n_steps = l // ts
    finalize = o_prev is not None
    rope = rope_tables is not None

    def tile(i):
        return (n_steps - 1 - i) if reverse else i

    in_specs = [pl.BlockSpec(memory_space=pltpu.SMEM),
                pl.BlockSpec((None, ts, w), lambda bi, i: (bi, tile(i), 0))]
    args = [log_g, p]
    if rope:
        in_specs += [pl.BlockSpec((ts, dq), lambda bi, i: (tile(i), 0))] * 2
        args += list(rope_tables)
    in_specs.append(pl.BlockSpec((None, n_heads, RET_DK, RET_DV), lambda bi, i: (bi, 0, 0, 0)))
    args.append(s0)
    if finalize:
        in_specs += [pl.BlockSpec((None, ts, dv), lambda bi, i: (bi, tile(i), 0)),
                     pl.BlockSpec((1, dv), lambda bi, i: (0, 0))]
        args += [o_prev, gn_w.reshape(1, dv)]
    kern = functools.partial(_ret_kernel, reverse=reverse, rope=rope, finalize=finalize, n_heads=n_heads)
    return pl.pallas_call(
        kern, grid=(b, n_steps),
        in_specs=in_specs,
        out_specs=[pl.BlockSpec((None, ts, dv), lambda bi, i: (bi, tile(i), 0)),
                   pl.BlockSpec((None, n_heads, RET_DK, RET_DV), lambda bi, i: (bi, 0, 0, 0))],
        out_shape=[jax.ShapeDtypeStruct((b, l, dv), BF16 if finalize else F32),
                   jax.ShapeDtypeStruct((b, n_heads, RET_DK, RET_DV), F32)],
        scratch_shapes=[pltpu.VMEM((n_heads, RET_DK, RET_DV), F32)],
        compiler_params=_cparams("parallel", "arbitrary"),
        name="retention_pass",
    )(*args)


def _ret_rope_tables(l, n_heads):
    half = RET_DK // 2
    inv = RET_ROPE_BASE ** (-jnp.linspace(0.0, 1.0, half, dtype=F32))
    ang = jnp.arange(l, dtype=F32)[:, None] * inv
    cos, sin = lax.optimization_barrier((jnp.cos(ang), jnp.sin(ang)))
    return (jnp.tile(jnp.concatenate([cos, cos], -1), (1, n_heads)),
            jnp.tile(jnp.concatenate([-sin, sin], -1), (1, n_heads)))


def retention_mixer(p_l, p_c, log_decay, gn_w, tables, with_ctx_out):
    b = p_l.shape[0]
    n_heads = p_l.shape[2] // (2 * RET_DK + 2 * RET_DV)
    log_g = -jnp.abs(log_decay.astype(F32))
    s0 = jnp.zeros((b, n_heads, RET_DK, RET_DV), F32)
    o_cf, s_fwd = retention_pass(p_c, log_g[0], s0, False)
    y_c, s_bwd = retention_pass(p_c, log_g[1], s0, True, o_prev=o_cf, gn_w=gn_w)
    o_lf, _ = retention_pass(p_l, log_g[0], s_fwd, False, rope_tables=tables)
    y_l, _ = retention_pass(p_l, log_g[1], s_bwd, True, rope_tables=tables, o_prev=o_lf, gn_w=gn_w)
    return y_l, (y_c if with_ctx_out else None)


def _gqa_prep_kernel(*refs, rope, n_q, n_kv, q_scale):
    if rope:
        p_ref, cos_ref, sin_ref, qw_ref, kw_ref, q_out, k_out, v_out = refs
    else:
        p_ref, qw_ref, kw_ref, q_out, k_out, v_out = refs
    x = p_ref[...]
    d = HEAD_DIM

    def norm_rope(xh, w):
        xh = xh.astype(F32)
        y = xh * lax.rsqrt(jnp.mean(xh * xh, axis=-1, keepdims=True) + EPS) * w
        if rope:
            y = y * cos_ref[...] + _swap32(y) * sin_ref[...]
        return y

    for h in range(n_q):
        q_out[h] = (norm_rope(x[:, h * d:(h + 1) * d], qw_ref[...]) * q_scale).astype(q_out.dtype)
    for h in range(n_kv):
        k_out[h] = norm_rope(x[:, (n_q + h) * d:(n_q + h + 1) * d], kw_ref[...]).astype(k_out.dtype)
        lane = lax.broadcasted_iota(jnp.int32, (x.shape[0], d), 1)
        ones_col = jnp.where(lane == 0, 1.0, 0.0).astype(v_out.dtype)
        v_out[h] = jnp.concatenate([x[:, (n_q + n_kv + h) * d:(n_q + n_kv + h + 1) * d], ones_col], axis=1)


def gqa_prep(p, qn_w, kn_w, n_q, n_kv, q_scale, rope_tables=None):
    b, l, w = p.shape
    d = HEAD_DIM
    t = _pick_tile(l, 512, 16)
    rope = rope_tables is not None
    in_specs = [pl.BlockSpec((None, t, w), lambda bi, i: (bi, i, 0))]
    args = [p]
    if rope:
        in_specs += [pl.BlockSpec((t, d), lambda bi, i: (i, 0))] * 2
        args += list(rope_tables)
    in_specs += [pl.BlockSpec((1, d), lambda bi, i: (0, 0))] * 2
    args += [qn_w.reshape(1, d), kn_w.reshape(1, d)]
    kern = functools.partial(_gqa_prep_kernel, rope=rope, n_q=n_q, n_kv=n_kv, q_scale=q_scale)
    return pl.pallas_call(
        kern, grid=(b, l // t),
        in_specs=in_specs,
        out_specs=[pl.BlockSpec((None, n_q, t, d), lambda bi, i: (bi, 0, i, 0)),
                   pl.BlockSpec((None, n_kv, t, d), lambda bi, i: (bi, 0, i, 0)),
                   pl.BlockSpec((None, n_kv, t, 2 * d), lambda bi, i: (bi, 0, i, 0))],
        out_shape=[jax.ShapeDtypeStruct((b, n_q, l, d), BF16),
                   jax.ShapeDtypeStruct((b, n_kv, l, d), BF16),
                   jax.ShapeDtypeStruct((b, n_kv, l, 2 * d), BF16)],
        compiler_params=_cparams("parallel", "parallel"),
        name="gqa_prep",
    )(*args)


def _axial_rope_tables(l):
    nf = HEAD_DIM // 4
    t = jnp.arange(l)
    inv = ROPE_BASE ** (-jnp.arange(nf, dtype=F32) / nf)
    ang_r = (t // GRID_W).astype(F32)[:, None] * inv
    ang_c = (t % GRID_W).astype(F32)[:, None] * inv
    cr, sr, cc, sc = lax.optimization_barrier((jnp.cos(ang_r), jnp.sin(ang_r), jnp.cos(ang_c), jnp.sin(ang_c)))
    return (jnp.concatenate([cr, cr, cc, cc], -1), jnp.concatenate([-sr, sr, -sc, sc], -1))


FLASH_SUB_ROWS = 256


def _flash_kernel(q_ref, k_ref, v_ref, o_ref, m_ref, acc_ref):
    j = pl.program_id(3)
    g, tq, d = q_ref.shape

    @pl.when(j == 0)
    def _():
        m_ref[...] = jnp.full(m_ref.shape, NEG_BIG, F32)
        acc_ref[...] = jnp.zeros(acc_ref.shape, F32)

    q = q_ref[...].reshape(g * tq, d)
    k = k_ref[...]
    v = v_ref[...]
    n_sub = (g * tq) // FLASH_SUB_ROWS
    s, p, alpha = [None] * n_sub, [None] * n_sub, [None] * n_sub
    for t in range(n_sub + 2):
        if t < n_sub:
            s[t] = lax.dot_general(q[t * FLASH_SUB_ROWS:(t + 1) * FLASH_SUB_ROWS], k, NT_DIMS,
                                   preferred_element_type=F32)
        u = t - 1
        if 0 <= u < n_sub:
            rows = slice(u * FLASH_SUB_ROWS, (u + 1) * FLASH_SUB_ROWS)
            m_prev = m_ref[rows]
            m_new = jnp.maximum(m_prev, jnp.max(s[u], axis=-1, keepdims=True))
            alpha[u] = jnp.exp2(m_prev - m_new)
            m_ref[rows] = m_new
            p[u] = jnp.exp2((s[u] - m_new).astype(BF16))
            s[u] = None
        w = t - 2
        if 0 <= w < n_sub:
            rows = slice(w * FLASH_SUB_ROWS, (w + 1) * FLASH_SUB_ROWS)
            acc_ref[rows] = alpha[w] * acc_ref[rows] + jnp.dot(p[w], v, preferred_element_type=F32)
            p[w] = None

    @pl.when(j == pl.num_programs(3) - 1)
    def _():
        acc = acc_ref[...]
        o = acc[:, :d] / acc[:, d:d + 1]
        o_ref[...] = jnp.concatenate([o[h * tq:(h + 1) * tq] for h in range(g)], axis=1).astype(o_ref.dtype)


def flash_gqa(q, k, v):
    b, hq, l, d = q.shape
    hkv, lk = k.shape[1], k.shape[2]
    g = hq // hkv
    tq = _pick_tile(l, 1024, 16)
    tk = _pick_tile(lk, 8320, LANES)
    assert (g * tq) % FLASH_SUB_ROWS == 0
    return pl.pallas_call(
        _flash_kernel, grid=(b, hkv, l // tq, lk // tk),
        in_specs=[pl.BlockSpec((None, g, tq, d), lambda bi, h, i, j: (bi, h, i, 0)),
                  pl.BlockSpec((None, None, tk, d), lambda bi, h, i, j: (bi, h, j, 0)),
                  pl.BlockSpec((None, None, tk, 2 * d), lambda bi, h, i, j: (bi, h, j, 0))],
        out_specs=pl.BlockSpec((None, tq, g * d), lambda bi, h, i, j: (bi, i, h)),
        out_shape=jax.ShapeDtypeStruct((b, l, hq * d), BF16),
        scratch_shapes=[pltpu.VMEM((g * tq, 1), F32), pltpu.VMEM((g * tq, 2 * d), F32)],
        compiler_params=_cparams("parallel", "parallel", "parallel", "arbitrary"),
        name="flash_gqa",
    )(q, k, v)


def gqa_mixer(p_l, p_c, qn_w, kn_w, tables, with_ctx_out):
    n_q = p_l.shape[2] // (2 * HEAD_DIM)
    n_kv = n_q // 2
    scale = HEAD_DIM ** -0.5
    q_l, k_l, v_l = gqa_prep(p_l, qn_w, kn_w, n_q, n_kv, scale * math.log2(math.e), rope_tables=tables)
    q_c, k_c, v_c = gqa_prep(p_c, qn_w, kn_w, n_q, n_kv, scale)
    y_l = flash_gqa(q_l, jnp.concatenate([k_l, k_c], axis=2), jnp.concatenate([v_l, v_c], axis=2))
    y_c = ctx_attention(q_c, 0, k_c, 0, v_c, 0, n_q, n_q // n_kv) if with_ctx_out else None
    return y_l, y_c


W_IN_GATE_ALIGN = 2048


def _prep_w_in(w_in, d_model):
    mix_w = d_model // N_BRANCH
    h = mix_w // HEAD_DIM
    sizes = (3 * mix_w, 3 * h * HEAD_DIM, 2 * h * RET_DK + 2 * h * RET_DV,
             (h + 2 * max(h // 2, 1)) * HEAD_DIM, N_BRANCH * d_model)
    scale = np.ones((sum(sizes),), np.float32)
    scale[sizes[0]:sizes[0] + h * HEAD_DIM] = HEAD_DIM ** -0.5
    k0 = sizes[0] + sizes[1] + h * RET_DK
    scale[k0:k0 + h * RET_DK] = RET_DK ** -0.5
    head = sum(sizes[:4])
    pad = -head % W_IN_GATE_ALIGN
    w = w_in * jnp.asarray(scale)
    w = jnp.concatenate([w[..., :head], jnp.zeros(w.shape[:2] + (pad,), w.dtype), w[..., head:]], axis=-1)
    offs = [0, sizes[0], sizes[0] + sizes[1], sizes[0] + sizes[1] + sizes[2], head + pad]
    return w.astype(BF16), list(zip(offs, sizes))


def kernel(x, c, ctx, c_ctx, ada_w, ada_b, norm1_w, norm2_w, w_in, hy_conv_w, hy_conv_b, hy_ffn_w1, hy_ffn_b1,
           hy_ffn_w2, hy_ffn_b2, hy_ffn_w3, hy_sin_freq, hy_bias, na_rpb, ret_log_decay, ret_gn_w, gqa_q_norm_w,
           gqa_k_norm_w, w_branch, w_out, ffn_w13, ffn_w2, final_norm_w):
    b, l, d = x.shape
    lc = ctx.shape[1]
    depth = ada_w.shape[0]
    ffn_hidden = ffn_w2.shape[1]
    n_ret_heads = (d // N_BRANCH) // RET_DV

    cs = jnp.zeros((8, d), F32).at[:b].set(c).at[b].set(c_ctx)
    mods = ada_mod(cs, ada_w, ada_b)
    ctx_row = b
    mats = hyena_mats(l, lc)
    ret_tables = _ret_rope_tables(l, n_ret_heads)
    gqa_tables = _axial_rope_tables(l)

    w_in_b, groups = _prep_w_in(w_in, d)
    wb = w_branch.astype(BF16)
    w_out_b = w_out.astype(BF16)
    w13_b = ffn_w13.astype(BF16)
    w2_b = ffn_w2.astype(BF16)

    x_l = x.reshape(b * l, d)
    x_c = ctx.reshape(b * lc, d)
    for layer in range(depth):
        last = layer == depth - 1
        mod = mods[layer]
        w_hy, w_na, w_rt, w_gq, w_gate = (WCols(w_in_b, layer, off, n) for off, n in groups)
        hy_params = (hy_conv_w[layer], hy_conv_b[layer], hy_ffn_w1[layer], hy_ffn_b1[layer], hy_ffn_w2[layer],
                     hy_ffn_b2[layer], hy_ffn_w3[layer], hy_sin_freq[layer], hy_bias[layer])
        wo = WCols(w_out_b, layer, 0, d)
        w1 = WCols(w13_b, layer, 0, ffn_hidden)
        w3 = WCols(w13_b, layer, ffn_hidden, ffn_hidden)
        w2 = WCols(w2_b, layer, 0, d)

        h_l = norm_mod(x_l, norm1_w[layer], mod, 0, 1, l, 0)
        h_c = norm_mod(x_c, norm1_w[layer], mod, 0, 1, None, ctx_row)

        na_l = matmul_heads(h_l, w_na, b)
        na_c = matmul_heads(h_c, w_na, b)
        rt_l = matmul(h_l, w_rt).reshape(b, l, -1)
        rt_c = matmul(h_c, w_rt).reshape(b, lc, -1)
        gq_l = matmul(h_l, w_gq).reshape(b, l, -1)
        gq_c = matmul(h_c, w_gq).reshape(b, lc, -1)
        hy_l = matmul(h_l, w_hy).reshape(b, l, -1)
        gate_l = matmul(h_l, w_gate)

        y_hy_l = hyena_mixer(hy_l, *hy_params, mats)
        y_na_l = na_attention(na_l, na_c, na_rpb[layer])
        y_rt_l, y_rt_c = retention_mixer(rt_l, rt_c, ret_log_decay[layer], ret_gn_w[layer], ret_tables, not last)
        y_gq_l, y_gq_c = gqa_mixer(gq_l, gq_c, gqa_q_norm_w[layer], gqa_k_norm_w[layer], gqa_tables, not last)

        m_l = merge_branches([y.reshape(b * l, -1) for y in (y_hy_l, y_na_l, y_rt_l, y_gq_l)], gate_l, wb, layer)
        x_l, h2 = matmul_residual_norm(m_l, wo, x_l, mod, 2, norm2_w[layer], mod, 3, 4, l, 0)
        x_l = matmul_residual(matmul_swiglu(h2, w1, w3), w2, x_l, mod, 5, l, 0)

        if not last:
            n_na = na_c.shape[1] // 3
            hy_c = matmul(h_c, w_hy).reshape(b, lc, -1)
            gate_c = matmul(h_c, w_gate)
            y_hy_c = hyena_mixer(hy_c, *hy_params, mats)
            y_na_c = ctx_attention(na_c, 0, na_c, n_na, na_c, 2 * n_na, n_na, 1)
            m_c = merge_branches([y.reshape(b * lc, -1) for y in (y_hy_c, y_na_c, y_rt_c, y_gq_c)], gate_c, wb,
                                 layer)
            x_c, h2c = matmul_residual_norm(m_c, wo, x_c, mod, 2, norm2_w[layer], mod, 3, 4, None, ctx_row)
            x_c = matmul_residual(matmul_swiglu(h2c, w1, w3), w2, x_c, mod, 5, None, ctx_row)

    return final_norm(x_l, final_norm_w).reshape(b, l, d)
```

```python
import functools
import math

import numpy as np
import jax
import jax.numpy as jnp
from jax import lax
from jax.experimental import pallas as pl
from jax.experimental.pallas import tpu as pltpu

F32 = jnp.float32
BF16 = jnp.bfloat16

EPS = 1e-6
GRID_W = 64
HEAD_DIM = 128
N_BRANCH = 4
ROPE_BASE = 10000.0
HY_BANDS = 16
HY_DECAY_TARGET = 1e-2
HY_FAST_PCT = 0.3
HY_SLOW_PCT = 1.5
NA_WIN_R = 8
NA_WIN_C = 16
RET_DK = 64
RET_DV = 128
RET_CHUNK = 128
RET_ROPE_BASE = 10000.0
NEG_BIG = -1e30

LANES = 128
FFT_N2 = 128
VMEM_LIMIT = 56 * 1024 * 1024

NT_DIMS = (((1,), (1,)), ((), ()))
TN_DIMS = (((0,), (0,)), ((), ()))


def _cparams(*sem):
    return pltpu.CompilerParams(dimension_semantics=sem, vmem_limit_bytes=VMEM_LIMIT)


def _pick_tile(n, cap, mult):
    best = None
    for t in range(mult, min(n, cap) + 1, mult):
        if n % t == 0:
            best = t
    assert best is not None, (n, cap, mult)
    return best


def _swap32(x):
    n = x.shape[-1]
    lane = lax.broadcasted_iota(jnp.int32, x.shape, x.ndim - 1)
    up = pltpu.roll(x, n - 32, x.ndim - 1)
    down = pltpu.roll(x, 32, x.ndim - 1)
    return jnp.where((lane % 64) < 32, up, down)


def _silu(x):
    return x * jax.nn.sigmoid(x)


def _ada_kernel(c_ref, w_ref, b_ref, o_ref):
    a = _silu(c_ref[...]).astype(BF16)
    o_ref[...] = jnp.dot(a, w_ref[...].astype(BF16), preferred_element_type=F32) + b_ref[...]


def ada_mod(cs, ada_w, ada_b):
    depth, d, n = ada_w.shape
    tn = _pick_tile(n, 1536, LANES)
    return pl.pallas_call(
        _ada_kernel,
        grid=(depth, n // tn),
        in_specs=[pl.BlockSpec((8, d), lambda l, j: (0, 0)),
                  pl.BlockSpec((None, d, tn), lambda l, j: (l, 0, j)),
                  pl.BlockSpec((None, 1, tn), lambda l, j: (l, 0, j))],
        out_specs=pl.BlockSpec((None, 8, tn), lambda l, j: (l, 0, j)),
        out_shape=jax.ShapeDtypeStruct((depth, 8, n), F32),
        compiler_params=_cparams("parallel", "parallel"),
        name="ada_mod",
    )(cs, ada_w, ada_b.reshape(depth, 1, n))


def _mod_row(row_base, tiles_per_batch, axis):
    if tiles_per_batch is None:
        return row_base
    return row_base + pl.program_id(axis) // tiles_per_batch


def _norm_mod_kernel(x_ref, w_ref, sh_ref, sc_ref, o_ref, *, row_base, tiles_per_batch):
    x = x_ref[...]
    y = x * lax.rsqrt(jnp.mean(x * x, axis=-1, keepdims=True) + EPS) * w_ref[...]
    row = _mod_row(row_base, tiles_per_batch, 0)
    sh = sh_ref[pl.ds(row, 1), :]
    sc = sc_ref[pl.ds(row, 1), :]
    o_ref[...] = (y * (1.0 + sc) + sh).astype(o_ref.dtype)


def _norm_kernel(x_ref, w_ref, o_ref):
    x = x_ref[...]
    y = x * lax.rsqrt(jnp.mean(x * x, axis=-1, keepdims=True) + EPS) * w_ref[...]
    o_ref[...] = y.astype(o_ref.dtype)


def norm_mod(x, w, mod, sh_chunk, sc_chunk, rows_per_batch, row_base):
    m, d = x.shape
    tm = _pick_tile(m if rows_per_batch is None else rows_per_batch, 512, 8)
    tpb = None if rows_per_batch is None else rows_per_batch // tm
    kern = functools.partial(_norm_mod_kernel, row_base=row_base, tiles_per_batch=tpb)
    return pl.pallas_call(
        kern, grid=(m // tm,),
        in_specs=[pl.BlockSpec((tm, d), lambda i: (i, 0)),
                  pl.BlockSpec((1, d), lambda i: (0, 0)),
                  pl.BlockSpec((8, d), lambda i: (0, sh_chunk)),
                  pl.BlockSpec((8, d), lambda i: (0, sc_chunk))],
        out_specs=pl.BlockSpec((tm, d), lambda i: (i, 0)),
        out_shape=jax.ShapeDtypeStruct((m, d), BF16),
        compiler_params=_cparams("parallel"),
        name="norm_mod",
    )(x, w.reshape(1, d), mod, mod)


def final_norm(x, w):
    m, d = x.shape
    tm = _pick_tile(m, 512, 8)
    return pl.pallas_call(
        _norm_kernel, grid=(m // tm,),
        in_specs=[pl.BlockSpec((tm, d), lambda i: (i, 0)), pl.BlockSpec((1, d), lambda i: (0, 0))],
        out_specs=pl.BlockSpec((tm, d), lambda i: (i, 0)),
        out_shape=jax.ShapeDtypeStruct((m, d), F32),
        compiler_params=_cparams("parallel"),
        name="final_norm",
    )(x, w.reshape(1, d))


def _mm_kernel(a_ref, w_ref, o_ref):
    o_ref[...] = jnp.dot(a_ref[...], w_ref[...], preferred_element_type=F32).astype(o_ref.dtype)


def _mm_heads_kernel(a_ref, w_ref, o_ref):
    acc = jnp.dot(a_ref[...], w_ref[...], preferred_element_type=F32)
    for h in range(o_ref.shape[0]):
        o_ref[h] = acc[:, h * HEAD_DIM:(h + 1) * HEAD_DIM].astype(o_ref.dtype)


def _mm_swiglu_kernel(a_ref, w1_ref, w3_ref, o_ref):
    a = a_ref[...]
    u = jnp.dot(a, w1_ref[...], preferred_element_type=F32)
    g = jnp.dot(a, w3_ref[...], preferred_element_type=F32)
    o_ref[...] = (_silu(u) * g).astype(o_ref.dtype)


def _mm_residual_norm_kernel(*refs, row_base, tiles_per_batch, n_k, modulate, write_x):
    it = iter(refs)
    a_ref, w_ref, x_ref, g_ref, nw_ref = next(it), next(it), next(it), next(it), next(it)
    sh_ref, sc_ref = (next(it), next(it)) if modulate else (None, None)
    ox_ref = next(it) if write_x else None
    oh_ref = next(it)
    acc_ref = next(it) if n_k > 1 else None
    kk = pl.program_id(1)
    part = jnp.dot(a_ref[...], w_ref[...], preferred_element_type=F32)

    def finish(acc):
        row = _mod_row(row_base, tiles_per_batch, 0)
        xn = x_ref[...] + g_ref[pl.ds(row, 1), :] * acc
        if write_x:
            ox_ref[...] = xn
        y = xn * lax.rsqrt(jnp.mean(xn * xn, axis=-1, keepdims=True) + EPS) * nw_ref[...]
        if modulate:
            y = y * (1.0 + sc_ref[pl.ds(row, 1), :]) + sh_ref[pl.ds(row, 1), :]
        oh_ref[...] = y.astype(oh_ref.dtype)

    if n_k == 1:
        finish(part)
    else:
        @pl.when(kk == 0)
        def _():
            acc_ref[...] = part

        @pl.when(jnp.logical_and(kk > 0, kk < n_k - 1))
        def _():
            acc_ref[...] += part

        @pl.when(kk == n_k - 1)
        def _():
            finish(acc_ref[...] + part)


class WCols:
    def __init__(self, arr, layer, off, n):
        self.arr, self.layer, self.off, self.n = arr, layer, off, n

    def tile(self, cap):
        tn = max(t for t in range(LANES, min(self.n, cap) + 1, LANES) if self.n % t == 0 and self.off % t == 0)
        return tn

    def spec(self, tn):
        layer, base, k = self.layer, self.off // tn, self.arr.shape[1]
        return pl.BlockSpec((None, k, tn), lambda j, i: (layer, 0, base + j))


def matmul(a, w, out_dtype=BF16):
    m, k = a.shape
    n = w.n
    tm = _pick_tile(m, 512, 16)
    tn = w.tile(2048)
    return pl.pallas_call(
        _mm_kernel, grid=(n // tn, m // tm),
        in_specs=[pl.BlockSpec((tm, k), lambda j, i: (i, 0)), w.spec(tn)],
        out_specs=pl.BlockSpec((tm, tn), lambda j, i: (i, j)),
        out_shape=jax.ShapeDtypeStruct((m, n), out_dtype),
        compiler_params=_cparams("parallel", "parallel"),
        name="matmul",
    )(a, w.arr)


def matmul_heads(a, w, batch):
    m, k = a.shape
    n = w.n
    lb = m // batch
    tm = _pick_tile(lb, 512, 16)
    tn = w.tile(1536)
    tpb = lb // tm
    nh = tn // HEAD_DIM
    return pl.pallas_call(
        _mm_heads_kernel, grid=(n // tn, m // tm),
        in_specs=[pl.BlockSpec((tm, k), lambda j, i: (i, 0)), w.spec(tn)],
        out_specs=pl.BlockSpec((None, nh, tm, HEAD_DIM), lambda j, i: (i // tpb, j, i % tpb, 0)),
        out_shape=jax.ShapeDtypeStruct((batch, n // HEAD_DIM, lb, HEAD_DIM), BF16),
        compiler_params=_cparams("parallel", "parallel"),
        name="matmul_heads",
    )(a, w.arr)


def matmul_swiglu(a, w1, w3):
    m, k = a.shape
    n = w1.n
    tm = _pick_tile(m, 1024, 16)
    tn = min(w1.tile(704), w3.tile(704))
    return pl.pallas_call(
        _mm_swiglu_kernel, grid=(n // tn, m // tm),
        in_specs=[pl.BlockSpec((tm, k), lambda j, i: (i, 0)), w1.spec(tn), w3.spec(tn)],
        out_specs=pl.BlockSpec((tm, tn), lambda j, i: (i, j)),
        out_shape=jax.ShapeDtypeStruct((m, n), BF16),
        compiler_params=_cparams("parallel", "parallel"),
        name="matmul_swiglu",
    )(a, w1.arr, w3.arr)


def _mm_residual_kernel(a_ref, w_ref, x_ref, g_ref, o_ref, *, row_base, tiles_per_batch):
    acc = jnp.dot(a_ref[...], w_ref[...], preferred_element_type=F32)
    row = _mod_row(row_base, tiles_per_batch, 1)
    o_ref[...] = x_ref[...] + g_ref[pl.ds(row, 1), :] * acc


def matmul_residual(a, w, x, mod, gate_chunk, rows_per_batch, row_base):
    m, k = a.shape
    n = w.n
    tm = _pick_tile(m if rows_per_batch is None else rows_per_batch, 512, 16)
    tn = w.tile(512)
    tpb = None if rows_per_batch is None else rows_per_batch // tm
    kern = functools.partial(_mm_residual_kernel, row_base=row_base, tiles_per_batch=tpb)
    gblk = gate_chunk * (n // tn)
    return pl.pallas_call(
        kern, grid=(n // tn, m // tm),
        in_specs=[pl.BlockSpec((tm, k), lambda j, i: (i, 0)),
                  w.spec(tn),
                  pl.BlockSpec((tm, tn), lambda j, i: (i, j)),
                  pl.BlockSpec((8, tn), lambda j, i: (0, gblk + j))],
        out_specs=pl.BlockSpec((tm, tn), lambda j, i: (i, j)),
        out_shape=jax.ShapeDtypeStruct((m, n), F32),
        compiler_params=_cparams("parallel", "parallel"),
        name="matmul_residual",
    )(a, w.arr, x, mod)


RESIDUAL_K_TILE = 2048


def matmul_residual_norm(a, w, x, mod, gate_chunk, norm_w, next_mod, sh_chunk, sc_chunk, rows_per_batch, row_base,
                         write_x=True, out_dtype=BF16):
    m, k = a.shape
    n = w.n
    assert w.off == 0 and n == w.arr.shape[2]
    tm = _pick_tile(m if rows_per_batch is None else rows_per_batch, 512, 16)
    tk = max(t for t in range(LANES, min(k, RESIDUAL_K_TILE) + 1, LANES) if k % t == 0)
    n_k = k // tk
    tpb = None if rows_per_batch is None else rows_per_batch // tm
    modulate = next_mod is not None
    kern = functools.partial(_mm_residual_norm_kernel, row_base=row_base, tiles_per_batch=tpb, n_k=n_k,
                             modulate=modulate, write_x=write_x)
    layer = w.layer
    row_spec = pl.BlockSpec((tm, n), lambda i, kk: (i, 0))
    in_specs = [pl.BlockSpec((tm, tk), lambda i, kk: (i, kk)),
                pl.BlockSpec((None, tk, n), lambda i, kk: (layer, kk, 0)),
                row_spec,
                pl.BlockSpec((8, n), lambda i, kk: (0, gate_chunk)),
                pl.BlockSpec((1, n), lambda i, kk: (0, 0))]
    args = [a, w.arr, x, mod, norm_w.reshape(1, n)]
    if modulate:
        in_specs += [pl.BlockSpec((8, n), lambda i, kk: (0, sh_chunk)),
                     pl.BlockSpec((8, n), lambda i, kk: (0, sc_chunk))]
        args += [next_mod, next_mod]
    out_specs = ([row_spec] if write_x else []) + [row_spec]
    out_shape = ([jax.ShapeDtypeStruct((m, n), F32)] if write_x else []) + [jax.ShapeDtypeStruct((m, n), out_dtype)]
    outs = pl.pallas_call(
        kern, grid=(m // tm, n_k),
        in_specs=in_specs, out_specs=out_specs, out_shape=out_shape,
        scratch_shapes=[pltpu.VMEM((tm, n), F32)] if n_k > 1 else [],
        compiler_params=_cparams("parallel", "arbitrary"),
        name="matmul_residual_norm",
    )(*args)
    return (outs[0], outs[1]) if write_x else (None, outs[0])


def _merge_kernel(y0, y1, y2, y3, g0, g1, g2, g3, wb_ref, o_ref):
    acc = None
    for i, (y, g) in enumerate(((y0, g0), (y1, g1), (y2, g2), (y3, g3))):
        gate = 0.5 * jnp.tanh(0.5 * g[...].astype(F32)) + 0.5
        t = gate * jnp.dot(y[...], wb_ref[i], preferred_element_type=F32)
        acc = t if acc is None else acc + t
    o_ref[...] = acc.astype(o_ref.dtype)


def merge_branches(ys, gate_pre, wb, layer):
    m, w = ys[0].shape
    d = wb.shape[3]
    tm = _pick_tile(m, 512, 16)
    tn = _pick_tile(d, 1024, LANES)
    nb = d // tn
    y_spec = pl.BlockSpec((tm, w), lambda j, i: (i, 0))
    g_specs = [pl.BlockSpec((tm, tn), functools.partial(lambda j, i, b: (i, b * nb + j), b=b))
               for b in range(N_BRANCH)]
    return pl.pallas_call(
        _merge_kernel, grid=(nb, m // tm),
        in_specs=[y_spec] * N_BRANCH + g_specs + [pl.BlockSpec((None, N_BRANCH, w, tn),
                                                               lambda j, i: (layer, 0, 0, j))],
        out_specs=pl.BlockSpec((tm, tn), lambda j, i: (i, j)),
        out_shape=jax.ShapeDtypeStruct((m, d), BF16),
        compiler_params=_cparams("parallel", "parallel"),
        name="merge_branches",
    )(*ys, gate_pre, gate_pre, gate_pre, gate_pre, wb)


FFT_ROWS = 16


def _store_cols(o_ref, lead, y, n2):
    cw = y.shape[1]
    tiles = jnp.swapaxes(y.reshape(y.shape[0] // n2, n2, cw), 0, 1)
    for j in range(n2):
        o_ref[lead + (slice(None), slice(j * cw, (j + 1) * cw))] = tiles[j]


def _short_conv_kernel(x_ref, prev_ref, next_ref, w_ref, b_ref, *o_refs, n_tiles, col_n2):
    i = pl.program_id(1)
    t = x_ref.shape[0]
    halo = prev_ref.shape[0]
    cw = x_ref.shape[1] // len(o_refs)
    row = lax.broadcasted_iota(jnp.int32, (t, cw), 0)
    for n, o_ref in enumerate(o_refs):
        cols = slice(n * cw, (n + 1) * cw)
        x = x_ref[:, cols].astype(F32)
        before = jnp.where(i > 0, prev_ref[:, cols].astype(F32)[halo - 1:halo, :], 0.0)
        after = jnp.where(i < n_tiles - 1, next_ref[:, cols].astype(F32)[0:1, :], 0.0)
        xm1 = jnp.where(row == 0, before, pltpu.roll(x, 1, 0))
        xp1 = jnp.where(row == t - 1, after, pltpu.roll(x, t - 1, 0))
        w = w_ref[:, cols]
        y = (w[0:1, :] * xm1 + w[1:2, :] * x + w[2:3, :] * xp1 + b_ref[:, cols]).astype(o_ref.dtype)
        if col_n2 is None:
            o_ref[...] = y
        else:
            _store_cols(o_ref, (), y, col_n2)


def short_conv(p, conv_w, conv_b, n_out, col_n2=None):
    b, l, c = p.shape
    cw = c // n_out
    halo = 16
    t = _pick_tile(l, 512, halo) if col_n2 is None else FFT_ROWS * col_n2
    n_tiles = l // t
    hb = t // halo
    kern = functools.partial(_short_conv_kernel, n_tiles=n_tiles, col_n2=col_n2)
    if col_n2 is None:
        out_specs = [pl.BlockSpec((None, t, cw), lambda bi, i: (bi, i, 0))] * n_out
        out_shape = [jax.ShapeDtypeStruct((b, l, cw), BF16)] * n_out
    else:
        out_specs = [pl.BlockSpec((None, FFT_ROWS, col_n2 * cw), lambda bi, i: (bi, i, 0))] * n_out
        out_shape = [jax.ShapeDtypeStruct((b, l // col_n2, col_n2 * cw), BF16)] * n_out
    return pl.pallas_call(
        kern, grid=(b, n_tiles),
        in_specs=[pl.BlockSpec((None, t, c), lambda bi, i: (bi, i, 0)),
                  pl.BlockSpec((None, halo, c), lambda bi, i: (bi, jnp.maximum(i * hb - 1, 0), 0)),
                  pl.BlockSpec((None, halo, c), lambda bi, i: (bi, jnp.minimum((i + 1) * hb, l // halo - 1), 0)),
                  pl.BlockSpec((8, c), lambda bi, i: (0, 0)),
                  pl.BlockSpec((1, c), lambda bi, i: (0, 0))],
        out_specs=out_specs, out_shape=out_shape,
        compiler_params=_cparams("parallel", "parallel"),
        name="short_conv",
    )(p, p, p, jnp.pad(conv_w, ((0, 8 - conv_w.shape[0]), (0, 0))), conv_b.reshape(1, c))


def _hy_filter_kernel(fv_ref, c2_ref, s2_ref, w1_ref, b1_ref, sf1_ref, w2_ref, b2_ref, sf2_ref, w3f_ref, w3b_ref,
                      dl_ref, f_ref, ss_ref, cb_ref, sb_ref, *, seq_len, col_n2):
    i = pl.program_id(0)
    tl = cb_ref.shape[0]
    hi = lax.Precision.HIGHEST
    step = 2.0 * math.pi / float(seq_len)
    r = lax.broadcasted_iota(jnp.int32, (tl, 1), 0)
    lane = lax.broadcasted_iota(jnp.int32, (1, LANES), 1)

    @pl.when(i == 0)
    def _():
        ang_r = (step * r.astype(F32)) * fv_ref[...]
        cb_ref[...] = jnp.cos(ang_r)
        sb_ref[...] = jnp.sin(ang_r)
        ss_ref[...] = jnp.zeros(ss_ref.shape, F32)

    base = (step * (i * tl).astype(F32)) * fv_ref[...]
    ca, sa = jnp.cos(base), jnp.sin(base)
    cb, sb = cb_ref[...], sb_ref[...]
    cos_j, sin_j = ca * cb - sa * sb, sa * cb + ca * sb
    cos_m, sin_m = c2_ref[...] * cos_j + s2_ref[...] * sin_j, s2_ref[...] * cos_j - c2_ref[...] * sin_j
    j = (i * tl + r).astype(F32)

    def mlp(t, cos_t, sin_t):
        t_norm = t / float(max(seq_len - 1, 1))
        feat = jnp.where(lane == 0, t_norm,
                         jnp.where(lane <= HY_BANDS, cos_t, jnp.where(lane <= 2 * HY_BANDS, -sin_t, 0.0)))
        z = jnp.sin(sf1_ref[...] * (jnp.dot(feat, w1_ref[...], precision=hi, preferred_element_type=F32)
                                    + b1_ref[...]))
        z = jnp.sin(sf2_ref[...] * (jnp.dot(z, w2_ref[...], precision=hi, preferred_element_type=F32)
                                    + b2_ref[...]))
        return z.astype(BF16), t_norm

    zf, tnf = mlp(j, cos_j, sin_j)
    zb, tnb = mlp(float(seq_len) - j, cos_m, sin_m)
    cw = w3f_ref.shape[1] // f_ref.shape[0]
    for o in range(f_ref.shape[0]):
        cols = slice(o * cw, (o + 1) * cw)
        hf = jnp.dot(zf, w3f_ref[:, cols], preferred_element_type=F32) * jnp.exp(-tnf * dl_ref[:, cols])
        hb = jnp.dot(zb, w3b_ref[:, cols], preferred_element_type=F32) * jnp.exp(-tnb * dl_ref[:, cols])
        hb = jnp.where(j > 0.0, hb, 0.0)
        ss_ref[:, cols] += jnp.sum(hf * hf, axis=0, keepdims=True) + jnp.sum(hb * hb, axis=0, keepdims=True)
        for s, h in enumerate((hf, hb)):
            if col_n2 is None:
                f_ref[o, s] = h.astype(f_ref.dtype)
            else:
                _store_cols(f_ref, (o, s), h.astype(f_ref.dtype), col_n2)


def hy_filters(seq_len, w1, b1, w2, b2, w3, sin_freq, hy_w, col_n2=None):
    emb, ffn = w1.shape
    cw = 2 * hy_w
    f = np.linspace(1e-4, HY_BANDS - 1, HY_BANDS)
    fv = np.zeros((1, LANES), np.float32)
    fv[0, 1:1 + HY_BANDS] = f
    fv[0, 1 + HY_BANDS:1 + 2 * HY_BANDS] = f
    c2 = np.cos(2.0 * np.pi * fv.astype(np.float64)).astype(np.float32)
    s2 = np.sin(2.0 * np.pi * fv.astype(np.float64)).astype(np.float32)
    w1p = jnp.pad(w1, ((0, LANES - emb), (0, 0)))
    w3r = w3.reshape(ffn, 2, 2, hy_w)
    w3f = w3r[:, :, 0, :].reshape(ffn, cw).astype(BF16)
    w3b = w3r[:, :, 1, :].reshape(ffn, cw).astype(BF16)
    deltas = np.abs(np.linspace(math.log(HY_DECAY_TARGET) / HY_SLOW_PCT,
                                math.log(HY_DECAY_TARGET) / HY_FAST_PCT, hy_w)).astype(np.float32)
    dl = jnp.asarray(np.tile(deltas, 2).reshape(1, cw))
    const = lambda i: (0, 0)
    if col_n2 is None:
        tl = _pick_tile(seq_len, 512, 16)
        f_spec = pl.BlockSpec((2, 2, tl, hy_w), lambda i: (0, 0, i, 0))
        f_shape = jax.ShapeDtypeStruct((2, 2, seq_len, hy_w), BF16)
    else:
        tl = FFT_ROWS * col_n2
        f_spec = pl.BlockSpec((2, 2, FFT_ROWS, col_n2 * hy_w), lambda i: (0, 0, i, 0))
        f_shape = jax.ShapeDtypeStruct((2, 2, seq_len // col_n2, col_n2 * hy_w), BF16)
    kern = functools.partial(_hy_filter_kernel, seq_len=seq_len, col_n2=col_n2)
    return pl.pallas_call(
        kern, grid=(seq_len // tl,),
        in_specs=[pl.BlockSpec((1, LANES), const), pl.BlockSpec((1, LANES), const), pl.BlockSpec((1, LANES), const),
                  pl.BlockSpec((LANES, ffn), const),
                  pl.BlockSpec((1, ffn), const), pl.BlockSpec((1, ffn), const),
                  pl.BlockSpec((ffn, ffn), const), pl.BlockSpec((1, ffn), const), pl.BlockSpec((1, ffn), const),
                  pl.BlockSpec((ffn, cw), const), pl.BlockSpec((ffn, cw), const), pl.BlockSpec((1, cw), const)],
        out_specs=[f_spec, pl.BlockSpec((1, cw), const)],
        out_shape=[f_shape, jax.ShapeDtypeStruct((1, cw), F32)],
        scratch_shapes=[pltpu.VMEM((tl, LANES), F32), pltpu.VMEM((tl, LANES), F32)],
        compiler_params=_cparams("arbitrary"),
        name="hy_filters",
    )(jnp.asarray(fv), jnp.asarray(c2), jnp.asarray(s2), w1p, b1.reshape(1, ffn), sin_freq[0].reshape(1, ffn), w2,
      b2.reshape(1, ffn), sin_freq[1].reshape(1, ffn), w3f, w3b, dl)


def _phase_mats(phase_num, denom, conj):
    ang = np.pi * (phase_num % (2 * denom)).astype(np.float64) / denom
    cr, ci = np.cos(ang), (np.sin(ang) if conj else -np.sin(ang))
    return cr, ci


def _stage1_mats(n1):
    k = np.arange(n1)[:, None]
    n = np.arange(n1)[None, :]
    w1d, w1f, wfin = [], [], []
    for v in (0, 1):
        cr, ci = _phase_mats(2 * k * n + v * n, n1, conj=False)
        w1d.append(np.block([[cr, -ci], [ci, cr]]))
        sgn = 1.0 if v == 0 else -1.0
        w1f.append(np.block([[cr, sgn * cr], [ci, sgn * ci]]))
        cri, cii = _phase_mats(2 * n.T * k.T + v * n.T, n1, conj=True)
        wfin.append(np.block([[cri, -cii], [cii, cri]]))
    return (np.concatenate(w1d, 0).astype(np.float32), np.concatenate(w1f, 0).astype(np.float32),
            np.concatenate(wfin, 1).astype(np.float32))


def _stage2_mats(n1, n2):
    l = n1 * n2
    k2 = np.arange(n2)[:, None]
    nn = np.arange(n2)[None, :]
    base = (2 * n1 * k2 * nn) % (2 * l)
    br, bi = np.cos(np.pi * base / l), -np.sin(np.pi * base / l)
    k1 = np.arange(n1)[None, :, None]
    v = np.arange(2)[:, None, None]
    tw = (2 * np.arange(n2)[None, None, :] * k1 + v * np.arange(n2)[None, None, :]) % (2 * l)
    tr, ti = np.cos(np.pi * tw / l), -np.sin(np.pi * tw / l)
    br, bi, tr, ti = (jnp.asarray(a, F32) for a in (br, bi, tr, ti))
    cr = br[None, None] * tr[:, :, None, :] - bi[None, None] * ti[:, :, None, :]
    ci = br[None, None] * ti[:, :, None, :] + bi[None, None] * tr[:, :, None, :]
    fwd = jnp.concatenate([jnp.concatenate([cr, -ci], -1), jnp.concatenate([ci, cr], -1)], -2)
    crt, cit = jnp.swapaxes(cr, -1, -2), -jnp.swapaxes(ci, -1, -2)
    inv = jnp.concatenate([jnp.concatenate([crt, -cit], -1), jnp.concatenate([cit, crt], -1)], -2)
    return fwd.astype(BF16), inv.astype(BF16)


def _fft_stage1_kernel(z_ref, w_ref, o_ref):
    n1 = z_ref.shape[1]
    nj, cw = o_ref.shape[3], o_ref.shape[4]
    x = z_ref[...].reshape(2 * n1, nj * cw)
    for plane in range(4):
        y = jnp.dot(w_ref[plane * n1:(plane + 1) * n1, :], x, preferred_element_type=F32).astype(o_ref.dtype)
        tiles = jnp.stack([y[:, j * cw:(j + 1) * cw] for j in range(nj)], axis=0)
        o_ref[plane // 2, plane % 2] = jnp.swapaxes(tiles, 0, 1)


FFT_COLS = 2048
FFT_S1_N2 = 16


def fft_stage1(z, w1, n1, n2):
    grp = z.shape[0]
    w = z.shape[3] // n2
    nj = _pick_tile(n2, FFT_S1_N2, 16)
    return pl.pallas_call(
        _fft_stage1_kernel, grid=(grp, n2 // nj),
        in_specs=[pl.BlockSpec((None, 2, n1, nj * w), lambda gi, j: (gi, 0, 0, j)),
                  pl.BlockSpec((4 * n1, 2 * n1), lambda gi, j: (0, 0))],
        out_specs=pl.BlockSpec((None, 2, 2, n1, nj, w), lambda gi, j: (gi, 0, 0, 0, j, 0)),
        out_shape=jax.ShapeDtypeStruct((grp, 2, 2, n1, n2, w), BF16),
        compiler_params=_cparams("parallel", "parallel"),
        name="fft_stage1",
    )(z, w1)


def _fft_mid_kernel(a_ref, f_ref, ss_ref, wf_ref, wi_ref, o_ref, *, scale):
    kb, n2, cw = a_ref.shape[1], a_ref.shape[2], a_ref.shape[3]
    nrm = lax.rsqrt(ss_ref[...] + EPS) * scale
    us = []
    for kk in range(kb):
        wf = wf_ref[kk]
        t = jnp.dot(wf, a_ref[:, kk].reshape(2 * n2, cw), preferred_element_type=F32)
        g = jnp.dot(wf, f_ref[:, kk].reshape(2 * n2, cw), preferred_element_type=F32) * nrm
        tr, ti, gr, gi = t[:n2], t[n2:], g[:n2], g[n2:]
        p = jnp.concatenate([tr * gr - ti * gi, tr * gi + ti * gr], axis=0).astype(BF16)
        us.append(jnp.dot(wi_ref[kk], p, preferred_element_type=F32).astype(o_ref.dtype))
    for r in range(2):
        tiles = jnp.swapaxes(jnp.stack([u[r * n2:(r + 1) * n2] for u in us], axis=0), 0, 1)
        for j in range(n2):
            o_ref[r, :, j * cw:(j + 1) * cw] = tiles[j]


FFT_MID_K1 = 16


def fft_mid(a, af, ss, wf, wi, order, n1, n2):
    cw = a.shape[-1]
    kb = _pick_tile(n1, FFT_MID_K1, 16)
    kern = functools.partial(_fft_mid_kernel, scale=1.0 / (2.0 * n1 * n2))
    mat_spec = pl.BlockSpec((None, kb, 2 * n2, 2 * n2), lambda v, k: (v, k, 0, 0))
    return pl.pallas_call(
        kern, grid=(2, n1 // kb),
        in_specs=[pl.BlockSpec((None, 2, kb, n2, cw), lambda v, k: (v, 0, k, 0, 0)),
                  pl.BlockSpec((None, None, 2, kb, n2, cw), lambda v, k: (order, v, 0, k, 0, 0)),
                  pl.BlockSpec((1, cw), lambda v, k: (0, order)),
                  mat_spec, mat_spec],
        out_specs=pl.BlockSpec((None, 2, kb, n2 * cw), lambda v, k: (v, 0, k, 0)),
        out_shape=jax.ShapeDtypeStruct((2, 2, n1, n2 * cw), BF16),
        compiler_params=_cparams("parallel", "parallel"),
        name="fft_mid",
    )(a, af, ss, wf, wi)


def _fft_final_kernel(u_ref, w_ref, gate_ref, z_ref, bias_ref, o_ref):
    n1 = gate_ref.shape[1]
    cw = gate_ref.shape[2]
    u = u_ref[...].reshape(4 * n1, cw)
    y = jnp.dot(w_ref[...], u, preferred_element_type=F32).reshape(2, n1, cw)
    z = z_ref[...].astype(F32)
    o_ref[...] = (gate_ref[...].astype(F32) * (y + bias_ref[...] * z)).astype(o_ref.dtype)


def fft_final(u, wfin, gate, z, bias, n1, n2):
    cols = z.shape[2]
    cw = cols // n2
    tc = _pick_tile(cols, FFT_COLS, cw)
    return pl.pallas_call(
        _fft_final_kernel, grid=(cols // tc,),
        in_specs=[pl.BlockSpec((2, 2, n1, tc), lambda j: (0, 0, 0, j)),
                  pl.BlockSpec((2 * n1, 4 * n1), lambda j: (0, 0)),
                  pl.BlockSpec((2, n1, tc), lambda j: (0, 0, j)),
                  pl.BlockSpec((2, n1, tc), lambda j: (0, 0, j)),
                  pl.BlockSpec((1, tc), lambda j: (0, 0))],
        out_specs=pl.BlockSpec((2, n1, tc), lambda j: (0, 0, j)),
        out_shape=jax.ShapeDtypeStruct((2, n1, cols), BF16),
        compiler_params=_cparams("parallel"),
        name="fft_final",
    )(u, wfin, gate, z, jnp.tile(bias.reshape(1, cw), (1, tc // cw)))


def _hy_dense_kernel(z_ref, gate_ref, f_ref, ss_ref, bias_ref, wf_ref, wi_ref, o_ref):
    lc = z_ref.shape[1]
    cw = z_ref.shape[2]
    nrm = lax.rsqrt(ss_ref[...] + EPS) * (1.0 / (2.0 * lc))
    f1, f2 = f_ref[0].astype(F32), f_ref[1].astype(F32)
    filt = (((f1 + f2) * nrm).astype(BF16), ((f1 - f2) * nrm).astype(BF16))
    x = z_ref[...].reshape(2 * lc, cw)
    acc = None
    for v in (0, 1):
        wf = wf_ref[v]
        t = jnp.dot(wf, x, preferred_element_type=F32)
        g = jnp.dot(wf[:, :lc], filt[v], preferred_element_type=F32)
        tr, ti, gr, gi = t[:lc], t[lc:], g[:lc], g[lc:]
        p = jnp.concatenate([tr * gr - ti * gi, tr * gi + ti * gr], axis=0).astype(BF16)
        u = jnp.dot(wi_ref[v], p, preferred_element_type=F32)
        acc = u if acc is None else acc + u
    y = acc.reshape(2, lc, cw)
    o_ref[...] = (gate_ref[...].astype(F32) * (y + bias_ref[...] * z_ref[...].astype(F32))).astype(o_ref.dtype)


def hy_dense_conv(z_arr, gate_arr, f, ss, order, bias, wf, wi):
    lc, cw = z_arr.shape[1], z_arr.shape[2]
    return pl.pallas_call(
        _hy_dense_kernel, grid=(1,),
        in_specs=[pl.BlockSpec((2, lc, cw), lambda i: (0, 0, 0)),
                  pl.BlockSpec((2, lc, cw), lambda i: (0, 0, 0)),
                  pl.BlockSpec((None, 2, lc, cw), lambda i: (order, 0, 0, 0)),
                  pl.BlockSpec((1, cw), lambda i: (0, order)),
                  pl.BlockSpec((1, cw), lambda i: (0, 0)),
                  pl.BlockSpec((2, None, 2 * lc, 2 * lc), lambda i: (0, 0, 0, 0)),
                  pl.BlockSpec((2, None, 2 * lc, 2 * lc), lambda i: (0, 0, 0, 0))],
        out_specs=pl.BlockSpec((2, lc, cw), lambda i: (0, 0, 0)),
        out_shape=jax.ShapeDtypeStruct((2, lc, cw), BF16),
        compiler_params=_cparams("arbitrary"),
        name="hy_dense_conv",
    )(z_arr, gate_arr, f, ss, bias.reshape(1, cw), wf, wi)


def hyena_mixer(p, conv_w, conv_b, w1, b1, w2, b2, w3, sin_freq, bias, mats):
    b, l, c3 = p.shape
    assert b == 2, "batch elements are packed as the re/im parts of one complex signal"
    hy_w = c3 // 3
    if l <= 2 * FFT_N2:
        v, x1, x2 = short_conv(p, conv_w, conv_b, 3)
        f, ss = hy_filters(l, w1, b1, w2, b2, w3, sin_freq, hy_w)
        wf, wi = mats["dense"]
        z2 = hy_dense_conv(v, x1, f, ss, 0, bias[0], wf, wi)
        return hy_dense_conv(z2, x2, f, ss, 1, bias[1], wf, wi)
    n2 = FFT_N2
    n1 = l // n2
    w1d, w1f, wfin, wf, wi = mats["fft"]
    v, x1, x2 = short_conv(p, conv_w, conv_b, 3, col_n2=n2)
    f, ss = hy_filters(l, w1, b1, w2, b2, w3, sin_freq, hy_w, col_n2=n2)
    af = fft_stage1(f, w1f, n1, n2)
    a = fft_stage1(v[None], w1d, n1, n2)[0]
    z2 = fft_final(fft_mid(a, af, ss, wf, wi, 0, n1, n2), wfin, x1, v, bias[0], n1, n2)
    a = fft_stage1(z2[None], w1d, n1, n2)[0]
    y = fft_final(fft_mid(a, af, ss, wf, wi, 1, n1, n2), wfin, x2, z2, bias[1], n1, n2)
    return y.reshape(2, l, hy_w)


def hyena_mats(l_lat, l_ctx):
    n1 = l_lat // FFT_N2
    w1d, w1f, wfin = _stage1_mats(n1)
    wf, wi = _stage2_mats(n1, FFT_N2)
    mats = {"fft": (jnp.asarray(w1d, BF16), jnp.asarray(w1f, BF16), jnp.asarray(wfin, BF16), wf, wi)}
    mats["dense"] = _stage2_mats(1, l_ctx)
    return mats


def _ctx_attn_kernel(q_ref, k_ref, v_ref, o_ref):
    s = lax.dot_general(q_ref[...], k_ref[...], NT_DIMS, preferred_element_type=F32)
    m = jnp.max(s, axis=-1, keepdims=True)
    p = jnp.exp(s - m)
    l = jnp.sum(p, axis=-1, keepdims=True)
    o = jnp.dot(p.astype(BF16), v_ref[...], preferred_element_type=F32) / l
    o_ref[...] = o.astype(o_ref.dtype)


def ctx_attention(qa, q_off, ka, k_off, va, v_off, n_q_heads, group):
    b, _, lc, d = qa.shape
    return pl.pallas_call(
        _ctx_attn_kernel, grid=(b, n_q_heads),
        in_specs=[pl.BlockSpec((None, None, lc, d), lambda bi, h: (bi, q_off + h, 0, 0)),
                  pl.BlockSpec((None, None, lc, d), lambda bi, h: (bi, k_off + h // group, 0, 0)),
                  pl.BlockSpec((None, None, lc, d), lambda bi, h: (bi, v_off + h // group, 0, 0))],
        out_specs=pl.BlockSpec((None, lc, d), lambda bi, h: (bi, 0, h)),
        out_shape=jax.ShapeDtypeStruct((b, lc, n_q_heads * d), BF16),
        compiler_params=_cparams("parallel", "parallel"),
        name="ctx_attention",
    )(qa, ka, va)


NA_TILE_ROWS = 4
NA_KEY_ROWS = 12
NA_HEADS_PER_STEP = 2


def _na_geometry(n_rows):
    wr = min(NA_WIN_R, n_rows)
    nt = n_rows // NA_TILE_ROWS
    sigs = []
    for t in range(nt):
        rt = t * NA_TILE_ROWS
        w0 = int(np.clip(rt - wr // 2, 0, n_rows - NA_KEY_ROWS))
        r = rt + np.arange(NA_TILE_ROWS)
        r0 = np.clip(r - wr // 2, 0, n_rows - wr)
        sigs.append((w0 - rt, tuple((r0 - w0).tolist())))
    classes = (sigs[0], sigs[1], sigs[-1])
    for t, s in enumerate(sigs):
        assert s == classes[0 if t == 0 else (2 if t == nt - 1 else 1)], "tile does not match its bias class"
    return wr, nt, classes


def _na_bias_tables(rpb, n_rows):
    wr, _, classes = _na_geometry(n_rows)
    cols = np.arange(GRID_W)
    c0 = np.clip(cols - NA_WIN_C // 2, 0, GRID_W - NA_WIN_C)
    col_ok = (cols[None, :] >= c0[:, None]) & (cols[None, :] < c0[:, None] + NA_WIN_C)
    col_idx = np.clip(cols[None, :] - cols[:, None] + NA_WIN_C - 1, 0, 2 * NA_WIN_C - 2)
    col_sel = (col_idx[None] == np.arange(2 * NA_WIN_C - 1)[:, None, None]).astype(np.float32)
    row_sel, oks = [], []
    for off, rel_r0 in classes:
        qi = np.arange(NA_TILE_ROWS)[:, None]
        kw = np.arange(NA_KEY_ROWS)[None, :]
        r0 = np.asarray(rel_r0)[:, None]
        row_ok = (kw >= r0) & (kw < r0 + wr)
        row_idx = np.clip(kw + off - qi + NA_WIN_R - 1, 0, 2 * NA_WIN_R - 2)
        row_sel.append((row_idx[None] == np.arange(2 * NA_WIN_R - 1)[:, None, None]).astype(np.float32))
        oks.append(row_ok[:, None, :, None] & col_ok[None, :, None, :])
    vals = jnp.einsum("hab,saqk,bcd->hsqckd", rpb.astype(F32), jnp.asarray(np.stack(row_sel)),
                      jnp.asarray(col_sel), precision=lax.Precision.HIGHEST)
    tab = jnp.where(jnp.asarray(np.stack(oks))[None], vals, NEG_BIG)
    return tab.reshape(rpb.shape[0], 3, NA_TILE_ROWS * GRID_W, NA_KEY_ROWS * GRID_W)


def _na_kernel(q_ref, k_ref, v_ref, kc_ref, vc_ref, bias_ref, o_ref, *, n_rows, win_half):
    t = pl.program_id(2)
    w0 = jnp.clip(t * NA_TILE_ROWS - win_half, 0, n_rows - NA_KEY_ROWS)
    start = pl.multiple_of(w0 * GRID_W, GRID_W)
    outs = []
    for hh in range(q_ref.shape[0]):
        kwin = k_ref[hh, pl.ds(start, NA_KEY_ROWS * GRID_W), :]
        vwin = v_ref[hh, pl.ds(start, NA_KEY_ROWS * GRID_W), :]
        q = q_ref[hh]
        s_lat = lax.dot_general(q, kwin, NT_DIMS, preferred_element_type=F32) + bias_ref[hh]
        s_ctx = lax.dot_general(q, kc_ref[hh], NT_DIMS, preferred_element_type=F32)
        m = jnp.maximum(jnp.max(s_lat, axis=-1, keepdims=True), jnp.max(s_ctx, axis=-1, keepdims=True))
        p_lat = jnp.exp(s_lat - m)
        p_ctx = jnp.exp(s_ctx - m)
        l = jnp.sum(p_lat, axis=-1, keepdims=True) + jnp.sum(p_ctx, axis=-1, keepdims=True)
        o = (jnp.dot(p_lat.astype(BF16), vwin, preferred_element_type=F32)
             + jnp.dot(p_ctx.astype(BF16), vc_ref[hh], preferred_element_type=F32)) / l
        outs.append(o.astype(o_ref.dtype))
    o_ref[...] = jnp.concatenate(outs, axis=1)


def na_attention(pl_heads, pc_heads, rpb):
    b, h3, l, d = pl_heads.shape
    lc = pc_heads.shape[2]
    h = h3 // 3
    n_rows = l // GRID_W
    wr, nt, _ = _na_geometry(n_rows)
    bias = _na_bias_tables(rpb, n_rows)
    tq = NA_TILE_ROWS * GRID_W
    kw = NA_KEY_ROWS * GRID_W
    kern = functools.partial(_na_kernel, n_rows=n_rows, win_half=wr // 2)

    def cls(t):
        return jnp.where(t == 0, 0, jnp.where(t == nt - 1, 2, 1))

    hs = NA_HEADS_PER_STEP
    assert h % hs == 0
    hb = h // hs
    return pl.pallas_call(
        kern, grid=(b, hb, nt),
        in_specs=[pl.BlockSpec((None, hs, tq, d), lambda bi, hi, t: (bi, hi, t, 0)),
                  pl.BlockSpec((None, hs, l, d), lambda bi, hi, t: (bi, hb + hi, 0, 0)),
                  pl.BlockSpec((None, hs, l, d), lambda bi, hi, t: (bi, 2 * hb + hi, 0, 0)),
                  pl.BlockSpec((None, hs, lc, d), lambda bi, hi, t: (bi, hb + hi, 0, 0)),
                  pl.BlockSpec((None, hs, lc, d), lambda bi, hi, t: (bi, 2 * hb + hi, 0, 0)),
                  pl.BlockSpec((hs, None, tq, kw), lambda bi, hi, t: (hi, cls(t), 0, 0))],
        out_specs=pl.BlockSpec((None, tq, hs * d), lambda bi, hi, t: (bi, t, hi)),
        out_shape=jax.ShapeDtypeStruct((b, l, h * d), BF16),
        compiler_params=_cparams("parallel", "parallel", "parallel"),
        name="na_attention",
    )(pl_heads, pl_heads, pl_heads, pc_heads, pc_heads, bias)


def _ret_kernel(*refs, reverse, rope, finalize, n_heads):
    it = iter(refs)
    lg_ref, p_ref = next(it), next(it)
    cos_ref, sin_ref = (next(it), next(it)) if rope else (None, None)
    s0_ref = next(it)
    oprev_ref, gnw_ref = (next(it), next(it)) if finalize else (None, None)
    o_ref, sfin_ref, state_ref, mask_ref = next(it), next(it), next(it), next(it)

    i = pl.program_id(1)
    n_steps = pl.num_programs(1)
    ts = p_ref.shape[0]
    c = ts
    nc = 1

    @pl.when(i == 0)
    def _():
        state_ref[...] = s0_ref[...]
        jr = lax.broadcasted_iota(jnp.int32, (c, c), 0)
        jc = lax.broadcasted_iota(jnp.int32, (c, c), 1)
        rel = ((jc - jr) if reverse else (jr - jc)).astype(F32)
        for h in range(n_heads):
            mask_ref[h] = jnp.where(rel >= 0.0, jnp.exp(lg_ref[h] * jnp.maximum(rel, 0.0)), 0.0)

    dq = n_heads * RET_DK
    dv = n_heads * RET_DV
    x = p_ref[...]
    q = x[:, :dq].astype(F32)
    k = x[:, dq:2 * dq].astype(F32)
    v = x[:, 2 * dq:2 * dq + dv]
    if rope:
        q = q * cos_ref[...] + _swap32(q) * sin_ref[...]
        k = k * cos_ref[...] + _swap32(k) * sin_ref[...]
    jcol = lax.broadcasted_iota(jnp.int32, (c, 1), 0).astype(F32)
    order = range(nc - 1, -1, -1) if reverse else range(nc)
    blocks = [[None] * n_heads for _ in range(nc)]
    for h in range(n_heads):
        g = lg_ref[h]
        intra = mask_ref[h]
        if reverse:
            cross_f = jnp.exp(g * (float(c) - jcol))
            k_dec = jnp.exp(g * jcol)
        else:
            cross_f = jnp.exp(g * (jcol + 1.0))
            k_dec = jnp.exp(g * (float(c - 1) - jcol))
        chunk_decay = jnp.exp(g * float(c))
        state = state_ref[h]
        for ci in order:
            rows = slice(ci * c, (ci + 1) * c)
            qh = q[rows, h * RET_DK:(h + 1) * RET_DK].astype(BF16)
            kf = k[rows, h * RET_DK:(h + 1) * RET_DK]
            vh = v[rows, h * RET_DV:(h + 1) * RET_DV]
            scores = lax.dot_general(qh, kf.astype(BF16), NT_DIMS, preferred_element_type=F32) * intra
            inner = jnp.dot(scores.astype(BF16), vh, preferred_element_type=F32)
            cross = jnp.dot(qh, state.astype(BF16), preferred_element_type=F32) * cross_f
            blocks[ci][h] = inner + cross
            kv = lax.dot_general((kf * k_dec).astype(BF16), vh, TN_DIMS, preferred_element_type=F32)
            state = chunk_decay * state + kv
        state_ref[h] = state
    o = jnp.concatenate([jnp.concatenate(blocks[ci], axis=1) for ci in range(nc)], axis=0)
    if finalize:
        o = o + oprev_ref[...]
        gate = x[:, 2 * dq + dv:2 * dq + 2 * dv].astype(F32)
        normed = []
        for h in range(n_heads):
            oh = o[:, h * RET_DV:(h + 1) * RET_DV]
            normed.append(oh * lax.rsqrt(jnp.mean(oh * oh, axis=-1, keepdims=True) + EPS))
        o = jnp.concatenate(normed, axis=1) * gnw_ref[...] * _silu(gate)
    o_ref[...] = o.astype(o_ref.dtype)

    @pl.when(i == n_steps - 1)
    def _():
        sfin_ref[...] = state_ref[...]


def retention_pass(p, log_g, s0, reverse, rope_tables=None, o_prev=None, gn_w=None):
    b, l, w = p.shape
    n_heads = w // (2 * RET_DK + 2 * RET_DV)
    dq, dv = n_heads * RET_DK, n_heads * RET_DV
    ts = _pick_tile(l, 512, RET_CHUNK)
    n_steps = l // ts
    finalize = o_prev is not None
    rope = rope_tables is not None

    def tile(i):
        return (n_steps - 1 - i) if reverse else i

    in_specs = [pl.BlockSpec(memory_space=pltpu.SMEM),
                pl.BlockSpec((None, ts, w), lambda bi, i: (bi, tile(i), 0))]
    args = [log_g, p]
    if rope:
        in_specs += [pl.BlockSpec((ts, dq), lambda bi, i: (tile(i), 0))] * 2
        args += list(rope_tables)
    in_specs.append(pl.BlockSpec((None, n_heads, RET_DK, RET_DV), lambda bi, i: (bi, 0, 0, 0)))
    args.append(s0)
    if finalize:
        in_specs += [pl.BlockSpec((None, ts, dv), lambda bi, i: (bi, tile(i), 0)),
                     pl.BlockSpec((1, dv), lambda bi, i: (0, 0))]
        args += [o_prev, gn_w.reshape(1, dv)]
    kern = functools.partial(_ret_kernel, reverse=reverse, rope=rope, finalize=finalize, n_heads=n_heads)
    return pl.pallas_call(
        kern, grid=(b, n_steps),
        in_specs=in_specs,
        out_specs=[pl.BlockSpec((None, ts, dv), lambda bi, i: (bi, tile(i), 0)),
                   pl.BlockSpec((None, n_heads, RET_DK, RET_DV), lambda bi, i: (bi, 0, 0, 0))],
        out_shape=[jax.ShapeDtypeStruct((b, l, dv), BF16 if finalize else F32),
                   jax.ShapeDtypeStruct((b, n_heads, RET_DK, RET_DV), F32)],
        scratch_shapes=[pltpu.VMEM((n_heads, RET_DK, RET_DV), F32), pltpu.VMEM((n_heads, ts, ts), F32)],
        compiler_params=_cparams("parallel", "arbitrary"),
        name="retention_pass",
    )(*args)


def _ret_rope_tables(l, n_heads):
    half = RET_DK // 2
    inv = RET_ROPE_BASE ** (-jnp.linspace(0.0, 1.0, half, dtype=F32))
    ang = jnp.arange(l, dtype=F32)[:, None] * inv
    cos, sin = lax.optimization_barrier((jnp.cos(ang), jnp.sin(ang)))
    return (jnp.tile(jnp.concatenate([cos, cos], -1), (1, n_heads)),
            jnp.tile(jnp.concatenate([-sin, sin], -1), (1, n_heads)))


def retention_mixer(p_l, p_c, log_decay, gn_w, tables, with_ctx_out):
    b = p_l.shape[0]
    n_heads = p_l.shape[2] // (2 * RET_DK + 2 * RET_DV)
    log_g = -jnp.abs(log_decay.astype(F32))
    s0 = jnp.zeros((b, n_heads, RET_DK, RET_DV), F32)
    o_cf, s_fwd = retention_pass(p_c, log_g[0], s0, False)
    y_c, s_bwd = retention_pass(p_c, log_g[1], s0, True, o_prev=o_cf, gn_w=gn_w)
    o_lf, _ = retention_pass(p_l, log_g[0], s_fwd, False, rope_tables=tables)
    y_l, _ = retention_pass(p_l, log_g[1], s_bwd, True, rope_tables=tables, o_prev=o_lf, gn_w=gn_w)
    return y_l, (y_c if with_ctx_out else None)


def _gqa_prep_kernel(*refs, rope, n_q, n_kv, q_scale):
    if rope:
        p_ref, cos_ref, sin_ref, qw_ref, kw_ref, q_out, k_out, v_out = refs
    else:
        p_ref, qw_ref, kw_ref, q_out, k_out, v_out = refs
    x = p_ref[...]
    d = HEAD_DIM

    def norm_rope(xh, w):
        xh = xh.astype(F32)
        y = xh * lax.rsqrt(jnp.mean(xh * xh, axis=-1, keepdims=True) + EPS) * w
        if rope:
            y = y * cos_ref[...] + _swap32(y) * sin_ref[...]
        return y

    for h in range(n_q):
        q_out[h] = (norm_rope(x[:, h * d:(h + 1) * d], qw_ref[...]) * q_scale).astype(q_out.dtype)
    for h in range(n_kv):
        k_out[h] = norm_rope(x[:, (n_q + h) * d:(n_q + h + 1) * d], kw_ref[...]).astype(k_out.dtype)
        lane = lax.broadcasted_iota(jnp.int32, (x.shape[0], d), 1)
        ones_col = jnp.where(lane == 0, 1.0, 0.0).astype(v_out.dtype)
        v_out[h] = jnp.concatenate([x[:, (n_q + n_kv + h) * d:(n_q + n_kv + h + 1) * d], ones_col], axis=1)


def gqa_prep(p, qn_w, kn_w, n_q, n_kv, q_scale, rope_tables=None):
    b, l, w = p.shape
    d = HEAD_DIM
    t = _pick_tile(l, 512, 16)
    rope = rope_tables is not None
    in_specs = [pl.BlockSpec((None, t, w), lambda bi, i: (bi, i, 0))]
    args = [p]
    if rope:
        in_specs += [pl.BlockSpec((t, d), lambda bi, i: (i, 0))] * 2
        args += list(rope_tables)
    in_specs += [pl.BlockSpec((1, d), lambda bi, i: (0, 0))] * 2
    args += [qn_w.reshape(1, d), kn_w.reshape(1, d)]
    kern = functools.partial(_gqa_prep_kernel, rope=rope, n_q=n_q, n_kv=n_kv, q_scale=q_scale)
    return pl.pallas_call(
        kern, grid=(b, l // t),
        in_specs=in_specs,
        out_specs=[pl.BlockSpec((None, n_q, t, d), lambda bi, i: (bi, 0, i, 0)),
                   pl.BlockSpec((None, n_kv, t, d), lambda bi, i: (bi, 0, i, 0)),
                   pl.BlockSpec((None, n_kv, t, 2 * d), lambda bi, i: (bi, 0, i, 0))],
        out_shape=[jax.ShapeDtypeStruct((b, n_q, l, d), BF16),
                   jax.ShapeDtypeStruct((b, n_kv, l, d), BF16),
                   jax.ShapeDtypeStruct((b, n_kv, l, 2 * d), BF16)],
        compiler_params=_cparams("parallel", "parallel"),
        name="gqa_prep",
    )(*args)


def _axial_rope_tables(l):
    nf = HEAD_DIM // 4
    t = jnp.arange(l)
    inv = ROPE_BASE ** (-jnp.arange(nf, dtype=F32) / nf)
    ang_r = (t // GRID_W).astype(F32)[:, None] * inv
    ang_c = (t % GRID_W).astype(F32)[:, None] * inv
    cr, sr, cc, sc = lax.optimization_barrier((jnp.cos(ang_r), jnp.sin(ang_r), jnp.cos(ang_c), jnp.sin(ang_c)))
    return (jnp.concatenate([cr, cr, cc, cc], -1), jnp.concatenate([-sr, sr, -sc, sc], -1))


FLASH_SUB_ROWS = 256


def _flash_kernel(q_ref, k_ref, v_ref, o_ref, m_ref, acc_ref):
    j = pl.program_id(3)
    g, tq, d = q_ref.shape

    @pl.when(j == 0)
    def _():
        m_ref[...] = jnp.full(m_ref.shape, NEG_BIG, F32)
        acc_ref[...] = jnp.zeros(acc_ref.shape, F32)

    q = q_ref[...].reshape(g * tq, d)
    k = k_ref[...]
    v = v_ref[...]
    n_sub = (g * tq) // FLASH_SUB_ROWS
    s, p, alpha = [None] * n_sub, [None] * n_sub, [None] * n_sub
    for t in range(n_sub + 2):
        if t < n_sub:
            s[t] = lax.dot_general(q[t * FLASH_SUB_ROWS:(t + 1) * FLASH_SUB_ROWS], k, NT_DIMS,
                                   preferred_element_type=F32)
        u = t - 1
        if 0 <= u < n_sub:
            rows = slice(u * FLASH_SUB_ROWS, (u + 1) * FLASH_SUB_ROWS)
            m_prev = m_ref[rows]
            m_new = jnp.maximum(m_prev, jnp.max(s[u], axis=-1, keepdims=True))
            alpha[u] = jnp.exp2(m_prev - m_new)
            m_ref[rows] = m_new
            p[u] = jnp.exp2((s[u] - m_new).astype(BF16))
            s[u] = None
        w = t - 2
        if 0 <= w < n_sub:
            rows = slice(w * FLASH_SUB_ROWS, (w + 1) * FLASH_SUB_ROWS)
            acc_ref[rows] = alpha[w] * acc_ref[rows] + jnp.dot(p[w], v, preferred_element_type=F32)
            p[w] = None

    @pl.when(j == pl.num_programs(3) - 1)
    def _():
        acc = acc_ref[...]
        o = acc[:, :d] / acc[:, d:d + 1]
        o_ref[...] = jnp.concatenate([o[h * tq:(h + 1) * tq] for h in range(g)], axis=1).astype(o_ref.dtype)


def flash_gqa(q, k, v):
    b, hq, l, d = q.shape
    hkv, lk = k.shape[1], k.shape[2]
    g = hq // hkv
    tq = _pick_tile(l, 1024, 16)
    tk = _pick_tile(lk, 8320, LANES)
    assert (g * tq) % FLASH_SUB_ROWS == 0
    return pl.pallas_call(
        _flash_kernel, grid=(b, hkv, l // tq, lk // tk),
        in_specs=[pl.BlockSpec((None, g, tq, d), lambda bi, h, i, j: (bi, h, i, 0)),
                  pl.BlockSpec((None, None, tk, d), lambda bi, h, i, j: (bi, h, j, 0)),
                  pl.BlockSpec((None, None, tk, 2 * d), lambda bi, h, i, j: (bi, h, j, 0))],
        out_specs=pl.BlockSpec((None, tq, g * d), lambda bi, h, i, j: (bi, i, h)),
        out_shape=jax.ShapeDtypeStruct((b, l, hq * d), BF16),
        scratch_shapes=[pltpu.VMEM((g * tq, 1), F32), pltpu.VMEM((g * tq, 2 * d), F32)],
        compiler_params=_cparams("parallel", "parallel", "parallel", "arbitrary"),
        name="flash_gqa",
    )(q, k, v)


def gqa_mixer(p_l, p_c, qn_w, kn_w, tables, with_ctx_out):
    n_q = p_l.shape[2] // (2 * HEAD_DIM)
    n_kv = n_q // 2
    scale = HEAD_DIM ** -0.5
    q_l, k_l, v_l = gqa_prep(p_l, qn_w, kn_w, n_q, n_kv, scale * math.log2(math.e), rope_tables=tables)
    q_c, k_c, v_c = gqa_prep(p_c, qn_w, kn_w, n_q, n_kv, scale)
    y_l = flash_gqa(q_l, jnp.concatenate([k_l, k_c], axis=2), jnp.concatenate([v_l, v_c], axis=2))
    y_c = ctx_attention(q_c, 0, k_c, 0, v_c, 0, n_q, n_q // n_kv) if with_ctx_out else None
    return y_l, y_c


W_IN_GATE_ALIGN = 2048


def _prep_w_in(w_in, d_model):
    mix_w = d_model // N_BRANCH
    h = mix_w // HEAD_DIM
    sizes = (3 * mix_w, 3 * h * HEAD_DIM, 2 * h * RET_DK + 2 * h * RET_DV,
             (h + 2 * max(h // 2, 1)) * HEAD_DIM, N_BRANCH * d_model)
    scale = np.ones((sum(sizes),), np.float32)
    scale[sizes[0]:sizes[0] + h * HEAD_DIM] = HEAD_DIM ** -0.5
    k0 = sizes[0] + sizes[1] + h * RET_DK
    scale[k0:k0 + h * RET_DK] = RET_DK ** -0.5
    head = sum(sizes[:4])
    pad = -head % W_IN_GATE_ALIGN
    w = w_in * jnp.asarray(scale)
    w = jnp.concatenate([w[..., :head], jnp.zeros(w.shape[:2] + (pad,), w.dtype), w[..., head:]], axis=-1)
    offs = [0, sizes[0], sizes[0] + sizes[1], sizes[0] + sizes[1] + sizes[2], head + pad]
    return w.astype(BF16), list(zip(offs, sizes))


def kernel(x, c, ctx, c_ctx, ada_w, ada_b, norm1_w, norm2_w, w_in, hy_conv_w, hy_conv_b, hy_ffn_w1, hy_ffn_b1,
           hy_ffn_w2, hy_ffn_b2, hy_ffn_w3, hy_sin_freq, hy_bias, na_rpb, ret_log_decay, ret_gn_w, gqa_q_norm_w,
           gqa_k_norm_w, w_branch, w_out, ffn_w13, ffn_w2, final_norm_w):
    b, l, d = x.shape
    lc = ctx.shape[1]
    depth = ada_w.shape[0]
    ffn_hidden = ffn_w2.shape[1]
    n_ret_heads = (d // N_BRANCH) // RET_DV

    cs = jnp.zeros((8, d), F32).at[:b].set(c).at[b].set(c_ctx)
    mods = ada_mod(cs, ada_w, ada_b)
    ctx_row = b
    mats = hyena_mats(l, lc)
    ret_tables = _ret_rope_tables(l, n_ret_heads)
    gqa_tables = _axial_rope_tables(l)

    w_in_b, groups = _prep_w_in(w_in, d)
    wb = w_branch.astype(BF16)
    w_out_b = w_out.astype(BF16)
    w13_b = ffn_w13.astype(BF16)
    w2_b = ffn_w2.astype(BF16)

    x_l = x.reshape(b * l, d)
    x_c = ctx.reshape(b * lc, d)
    for layer in range(depth):
        last = layer == depth - 1
        mod = mods[layer]
        w_hy, w_na, w_rt, w_gq, w_gate = (WCols(w_in_b, layer, off, n) for off, n in groups)
        hy_params = (hy_conv_w[layer], hy_conv_b[layer], hy_ffn_w1[layer], hy_ffn_b1[layer], hy_ffn_w2[layer],
                     hy_ffn_b2[layer], hy_ffn_w3[layer], hy_sin_freq[layer], hy_bias[layer])
        wo = WCols(w_out_b, layer, 0, d)
        w1 = WCols(w13_b, layer, 0, ffn_hidden)
        w3 = WCols(w13_b, layer, ffn_hidden, ffn_hidden)
        w2 = WCols(w2_b, layer, 0, d)

        h_l = norm_mod(x_l, norm1_w[layer], mod, 0, 1, l, 0)
        h_c = norm_mod(x_c, norm1_w[layer], mod, 0, 1, None, ctx_row)

        na_l = matmul_heads(h_l, w_na, b)
        na_c = matmul_heads(h_c, w_na, b)
        rt_l = matmul(h_l, w_rt).reshape(b, l, -1)
        rt_c = matmul(h_c, w_rt).reshape(b, lc, -1)
        gq_l = matmul(h_l, w_gq).reshape(b, l, -1)
        gq_c = matmul(h_c, w_gq).reshape(b, lc, -1)
        hy_l = matmul(h_l, w_hy).reshape(b, l, -1)
        gate_l = matmul(h_l, w_gate)

        y_hy_l = hyena_mixer(hy_l, *hy_params, mats)
        y_na_l = na_attention(na_l, na_c, na_rpb[layer])
        y_rt_l, y_rt_c = retention_mixer(rt_l, rt_c, ret_log_decay[layer], ret_gn_w[layer], ret_tables, not last)
        y_gq_l, y_gq_c = gqa_mixer(gq_l, gq_c, gqa_q_norm_w[layer], gqa_k_norm_w[layer], gqa_tables, not last)

        m_l = merge_branches([y.reshape(b * l, -1) for y in (y_hy_l, y_na_l, y_rt_l, y_gq_l)], gate_l, wb, layer)
        x_l, h2 = matmul_residual_norm(m_l, wo, x_l, mod, 2, norm2_w[layer], mod, 3, 4, l, 0)
        x_l = matmul_residual(matmul_swiglu(h2, w1, w3), w2, x_l, mod, 5, l, 0)

        if not last:
            n_na = na_c.shape[1] // 3
            hy_c = matmul(h_c, w_hy).reshape(b, lc, -1)
            gate_c = matmul(h_c, w_gate)
            y_hy_c = hyena_mixer(hy_c, *hy_params, mats)
            y_na_c = ctx_attention(na_c, 0, na_c, n_na, na_c, 2 * n_na, n_na, 1)
            m_c = merge_branches([y.reshape(b * lc, -1) for y in (y_hy_c, y_na_c, y_rt_c, y_gq_c)], gate_c, wb,
                                 layer)
            x_c, h2c = matmul_residual_norm(m_c, wo, x_c, mod, 2, norm2_w[layer], mod, 3, 4, None, ctx_row)
            x_c = matmul_residual(matmul_swiglu(h2c, w1, w3), w2, x_c, mod, 5, None, ctx_row)

    return final_norm(x_l, final_norm_w).reshape(b, l, d)
```

```python
import functools
import math

import numpy as np
import jax
import jax.numpy as jnp
from jax import lax
from jax.experimental import pallas as pl
from jax.experimental.pallas import tpu as pltpu

F32 = jnp.float32
BF16 = jnp.bfloat16

EPS = 1e-6
GRID_W = 64
HEAD_DIM = 128
N_BRANCH = 4
ROPE_BASE = 10000.0
HY_BANDS = 16
HY_DECAY_TARGET = 1e-2
HY_FAST_PCT = 0.3
HY_SLOW_PCT = 1.5
NA_WIN_R = 8
NA_WIN_C = 16
RET_DK = 64
RET_DV = 128
RET_CHUNK = 128
RET_ROPE_BASE = 10000.0
NEG_BIG = -1e30

LANES = 128
FFT_N2 = 128
VMEM_LIMIT = 56 * 1024 * 1024

NT_DIMS = (((1,), (1,)), ((), ()))
TN_DIMS = (((0,), (0,)), ((), ()))


def _cparams(*sem):
    return pltpu.CompilerParams(dimension_semantics=sem, vmem_limit_bytes=VMEM_LIMIT)


def _pick_tile(n, cap, mult):
    best = None
    for t in range(mult, min(n, cap) + 1, mult):
        if n % t == 0:
            best = t
    assert best is not None, (n, cap, mult)
    return best


def _swap32(x):
    n = x.shape[-1]
    lane = lax.broadcasted_iota(jnp.int32, x.shape, x.ndim - 1)
    up = pltpu.roll(x, n - 32, x.ndim - 1)
    down = pltpu.roll(x, 32, x.ndim - 1)
    return jnp.where((lane % 64) < 32, up, down)


def _silu(x):
    return x * jax.nn.sigmoid(x)


def _ada_kernel(c_ref, w_ref, b_ref, o_ref):
    a = _silu(c_ref[...]).astype(BF16)
    o_ref[...] = jnp.dot(a, w_ref[...].astype(BF16), preferred_element_type=F32) + b_ref[...]


def ada_mod(cs, ada_w, ada_b):
    depth, d, n = ada_w.shape
    tn = _pick_tile(n, 1536, LANES)
    return pl.pallas_call(
        _ada_kernel,
        grid=(depth, n // tn),
        in_specs=[pl.BlockSpec((8, d), lambda l, j: (0, 0)),
                  pl.BlockSpec((None, d, tn), lambda l, j: (l, 0, j)),
                  pl.BlockSpec((None, 1, tn), lambda l, j: (l, 0, j))],
        out_specs=pl.BlockSpec((None, 8, tn), lambda l, j: (l, 0, j)),
        out_shape=jax.ShapeDtypeStruct((depth, 8, n), F32),
        compiler_params=_cparams("parallel", "parallel"),
        name="ada_mod",
    )(cs, ada_w, ada_b.reshape(depth, 1, n))


def _mod_row(row_base, tiles_per_batch, axis):
    if tiles_per_batch is None:
        return row_base
    return row_base + pl.program_id(axis) // tiles_per_batch


def _norm_mod_kernel(x_ref, w_ref, sh_ref, sc_ref, o_ref, *, row_base, tiles_per_batch):
    x = x_ref[...]
    y = x * lax.rsqrt(jnp.mean(x * x, axis=-1, keepdims=True) + EPS) * w_ref[...]
    row = _mod_row(row_base, tiles_per_batch, 0)
    sh = sh_ref[pl.ds(row, 1), :]
    sc = sc_ref[pl.ds(row, 1), :]
    o_ref[...] = (y * (1.0 + sc) + sh).astype(o_ref.dtype)


def _norm_kernel(x_ref, w_ref, o_ref):
    x = x_ref[...]
    y = x * lax.rsqrt(jnp.mean(x * x, axis=-1, keepdims=True) + EPS) * w_ref[...]
    o_ref[...] = y.astype(o_ref.dtype)


def norm_mod(x, w, mod, sh_chunk, sc_chunk, rows_per_batch, row_base):
    m, d = x.shape
    tm = _pick_tile(m if rows_per_batch is None else rows_per_batch, 512, 8)
    tpb = None if rows_per_batch is None else rows_per_batch // tm
    kern = functools.partial(_norm_mod_kernel, row_base=row_base, tiles_per_batch=tpb)
    return pl.pallas_call(
        kern, grid=(m // tm,),
        in_specs=[pl.BlockSpec((tm, d), lambda i: (i, 0)),
                  pl.BlockSpec((1, d), lambda i: (0, 0)),
                  pl.BlockSpec((8, d), lambda i: (0, sh_chunk)),
                  pl.BlockSpec((8, d), lambda i: (0, sc_chunk))],
        out_specs=pl.BlockSpec((tm, d), lambda i: (i, 0)),
        out_shape=jax.ShapeDtypeStruct((m, d), BF16),
        compiler_params=_cparams("parallel"),
        name="norm_mod",
    )(x, w.reshape(1, d), mod, mod)


def final_norm(x, w):
    m, d = x.shape
    tm = _pick_tile(m, 512, 8)
    return pl.pallas_call(
        _norm_kernel, grid=(m // tm,),
        in_specs=[pl.BlockSpec((tm, d), lambda i: (i, 0)), pl.BlockSpec((1, d), lambda i: (0, 0))],
        out_specs=pl.BlockSpec((tm, d), lambda i: (i, 0)),
        out_shape=jax.ShapeDtypeStruct((m, d), F32),
        compiler_params=_cparams("parallel"),
        name="final_norm",
    )(x, w.reshape(1, d))


def _mm_kernel(a_ref, w_ref, o_ref):
    o_ref[...] = jnp.dot(a_ref[...], w_ref[...], preferred_element_type=F32).astype(o_ref.dtype)


def _mm_heads_kernel(a_ref, w_ref, o_ref):
    acc = jnp.dot(a_ref[...], w_ref[...], preferred_element_type=F32)
    for h in range(o_ref.shape[0]):
        o_ref[h] = acc[:, h * HEAD_DIM:(h + 1) * HEAD_DIM].astype(o_ref.dtype)


def _mm_swiglu_kernel(a_ref, w1_ref, w3_ref, o_ref):
    a = a_ref[...]
    u = jnp.dot(a, w1_ref[...], preferred_element_type=F32)
    g = jnp.dot(a, w3_ref[...], preferred_element_type=F32)
    o_ref[...] = (_silu(u) * g).astype(o_ref.dtype)


def _mm_residual_norm_kernel(a_ref, w_ref, x_ref, g_ref, nw_ref, sh_ref, sc_ref, ox_ref, oh_ref, *,
                             row_base, tiles_per_batch):
    acc = jnp.dot(a_ref[...], w_ref[...], preferred_element_type=F32)
    row = _mod_row(row_base, tiles_per_batch, 0)
    xn = x_ref[...] + g_ref[pl.ds(row, 1), :] * acc
    ox_ref[...] = xn
    y = xn * lax.rsqrt(jnp.mean(xn * xn, axis=-1, keepdims=True) + EPS) * nw_ref[...]
    y = y * (1.0 + sc_ref[pl.ds(row, 1), :]) + sh_ref[pl.ds(row, 1), :]
    oh_ref[...] = y.astype(oh_ref.dtype)


class WCols:
    def __init__(self, arr, layer, off, n):
        self.arr, self.layer, self.off, self.n = arr, layer, off, n

    def tile(self, cap):
        tn = max(t for t in range(LANES, min(self.n, cap) + 1, LANES) if self.n % t == 0 and self.off % t == 0)
        return tn

    def spec(self, tn):
        layer, base, k = self.layer, self.off // tn, self.arr.shape[1]
        return pl.BlockSpec((None, k, tn), lambda j, i: (layer, 0, base + j))


def matmul(a, w, out_dtype=BF16):
    m, k = a.shape
    n = w.n
    tm = _pick_tile(m, 1024, 16)
    tn = w.tile(2048)
    return pl.pallas_call(
        _mm_kernel, grid=(n // tn, m // tm),
        in_specs=[pl.BlockSpec((tm, k), lambda j, i: (i, 0)), w.spec(tn)],
        out_specs=pl.BlockSpec((tm, tn), lambda j, i: (i, j)),
        out_shape=jax.ShapeDtypeStruct((m, n), out_dtype),
        compiler_params=_cparams("parallel", "parallel"),
        name="matmul",
    )(a, w.arr)


def matmul_heads(a, w, batch):
    m, k = a.shape
    n = w.n
    lb = m // batch
    tm = _pick_tile(lb, 1024, 16)
    tn = w.tile(1536)
    tpb = lb // tm
    nh = tn // HEAD_DIM
    return pl.pallas_call(
        _mm_heads_kernel, grid=(n // tn, m // tm),
        in_specs=[pl.BlockSpec((tm, k), lambda j, i: (i, 0)), w.spec(tn)],
        out_specs=pl.BlockSpec((None, nh, tm, HEAD_DIM), lambda j, i: (i // tpb, j, i % tpb, 0)),
        out_shape=jax.ShapeDtypeStruct((batch, n // HEAD_DIM, lb, HEAD_DIM), BF16),
        compiler_params=_cparams("parallel", "parallel"),
        name="matmul_heads",
    )(a, w.arr)


def matmul_swiglu(a, w1, w3):
    m, k = a.shape
    n = w1.n
    tm = _pick_tile(m, 1024, 16)
    tn = min(w1.tile(704), w3.tile(704))
    return pl.pallas_call(
        _mm_swiglu_kernel, grid=(n // tn, m // tm),
        in_specs=[pl.BlockSpec((tm, k), lambda j, i: (i, 0)), w1.spec(tn), w3.spec(tn)],
        out_specs=pl.BlockSpec((tm, tn), lambda j, i: (i, j)),
        out_shape=jax.ShapeDtypeStruct((m, n), BF16),
        compiler_params=_cparams("parallel", "parallel"),
        name="matmul_swiglu",
    )(a, w1.arr, w3.arr)


def _mm_residual_kernel(a_ref, w_ref, x_ref, g_ref, o_ref, *, row_base, tiles_per_batch):
    acc = jnp.dot(a_ref[...], w_ref[...], preferred_element_type=F32)
    row = _mod_row(row_base, tiles_per_batch, 1)
    o_ref[...] = x_ref[...] + g_ref[pl.ds(row, 1), :] * acc


def matmul_residual(a, w, x, mod, gate_chunk, rows_per_batch, row_base):
    m, k = a.shape
    n = w.n
    tm = _pick_tile(m if rows_per_batch is None else rows_per_batch, 1024, 16)
    tn = w.tile(512)
    tpb = None if rows_per_batch is None else rows_per_batch // tm
    kern = functools.partial(_mm_residual_kernel, row_base=row_base, tiles_per_batch=tpb)
    gblk = gate_chunk * (n // tn)
    return pl.pallas_call(
        kern, grid=(n // tn, m // tm),
        in_specs=[pl.BlockSpec((tm, k), lambda j, i: (i, 0)),
                  w.spec(tn),
                  pl.BlockSpec((tm, tn), lambda j, i: (i, j)),
                  pl.BlockSpec((8, tn), lambda j, i: (0, gblk + j))],
        out_specs=pl.BlockSpec((tm, tn), lambda j, i: (i, j)),
        out_shape=jax.ShapeDtypeStruct((m, n), F32),
        compiler_params=_cparams("parallel", "parallel"),
        name="matmul_residual",
    )(a, w.arr, x, mod)


def matmul_residual_norm(a, w, x, mod, gate_chunk, norm_w, sh_chunk, sc_chunk, rows_per_batch, row_base):
    m, k = a.shape
    n = w.n
    assert w.off == 0 and n == w.arr.shape[2]
    tm = _pick_tile(m if rows_per_batch is None else rows_per_batch, 512, 16)
    tpb = None if rows_per_batch is None else rows_per_batch // tm
    kern = functools.partial(_mm_residual_norm_kernel, row_base=row_base, tiles_per_batch=tpb)
    layer = w.layer
    row_spec = pl.BlockSpec((tm, n), lambda i: (i, 0))
    return pl.pallas_call(
        kern, grid=(m // tm,),
        in_specs=[pl.BlockSpec((tm, k), lambda i: (i, 0)),
                  pl.BlockSpec((None, k, n), lambda i: (layer, 0, 0)),
                  row_spec,
                  pl.BlockSpec((8, n), lambda i: (0, gate_chunk)),
                  pl.BlockSpec((1, n), lambda i: (0, 0)),
                  pl.BlockSpec((8, n), lambda i: (0, sh_chunk)),
                  pl.BlockSpec((8, n), lambda i: (0, sc_chunk))],
        out_specs=[row_spec, row_spec],
        out_shape=[jax.ShapeDtypeStruct((m, n), F32), jax.ShapeDtypeStruct((m, n), BF16)],
        compiler_params=_cparams("parallel"),
        name="matmul_residual_norm",
    )(a, w.arr, x, mod, norm_w.reshape(1, n), mod, mod)


def _merge_kernel(h_ref, y0, y1, y2, y3, g0, g1, g2, g3, wb_ref, o_ref):
    h = h_ref[...]
    acc = None
    for i, (y, g) in enumerate(((y0, g0), (y1, g1), (y2, g2), (y3, g3))):
        gate = 0.5 * jnp.tanh(0.5 * jnp.dot(h, g[...], preferred_element_type=F32)) + 0.5
        t = gate * jnp.dot(y[...], wb_ref[i], preferred_element_type=F32)
        acc = t if acc is None else acc + t
    o_ref[...] = acc.astype(o_ref.dtype)


def merge_branches(h, ys, w_gate, wb, layer):
    m, w = ys[0].shape
    k = h.shape[1]
    d = wb.shape[3]
    tm = _pick_tile(m, 512, 16)
    tn = w_gate.tile(512)
    nb = d // tn
    base = w_gate.off // tn
    y_spec = pl.BlockSpec((tm, w), lambda j, i: (i, 0))
    g_specs = [pl.BlockSpec((None, k, tn), functools.partial(lambda j, i, b: (layer, 0, base + b * nb + j), b=b))
               for b in range(N_BRANCH)]
    return pl.pallas_call(
        _merge_kernel, grid=(nb, m // tm),
        in_specs=[pl.BlockSpec((tm, k), lambda j, i: (i, 0))] + [y_spec] * N_BRANCH + g_specs
        + [pl.BlockSpec((None, N_BRANCH, w, tn), lambda j, i: (layer, 0, 0, j))],
        out_specs=pl.BlockSpec((tm, tn), lambda j, i: (i, j)),
        out_shape=jax.ShapeDtypeStruct((m, d), BF16),
        compiler_params=_cparams("parallel", "parallel"),
        name="merge_branches",
    )(h, *ys, w_gate.arr, w_gate.arr, w_gate.arr, w_gate.arr, wb)


FFT_ROWS = 16


def _store_cols(o_ref, lead, y, n2):
    cw = y.shape[1]
    tiles = jnp.swapaxes(y.reshape(y.shape[0] // n2, n2, cw), 0, 1)
    for j in range(n2):
        o_ref[lead + (slice(None), slice(j * cw, (j + 1) * cw))] = tiles[j]


def _short_conv_kernel(x_ref, prev_ref, next_ref, w_ref, b_ref, *o_refs, n_tiles, col_n2):
    i = pl.program_id(1)
    t = x_ref.shape[0]
    halo = prev_ref.shape[0]
    cw = x_ref.shape[1] // len(o_refs)
    row = lax.broadcasted_iota(jnp.int32, (t, cw), 0)
    for n, o_ref in enumerate(o_refs):
        cols = slice(n * cw, (n + 1) * cw)
        x = x_ref[:, cols].astype(F32)
        before = jnp.where(i > 0, prev_ref[:, cols].astype(F32)[halo - 1:halo, :], 0.0)
        after = jnp.where(i < n_tiles - 1, next_ref[:, cols].astype(F32)[0:1, :], 0.0)
        xm1 = jnp.where(row == 0, before, pltpu.roll(x, 1, 0))
        xp1 = jnp.where(row == t - 1, after, pltpu.roll(x, t - 1, 0))
        w = w_ref[:, cols]
        y = (w[0:1, :] * xm1 + w[1:2, :] * x + w[2:3, :] * xp1 + b_ref[:, cols]).astype(o_ref.dtype)
        if col_n2 is None:
            o_ref[...] = y
        else:
            _store_cols(o_ref, (), y, col_n2)


def short_conv(p, conv_w, conv_b, n_out, col_n2=None):
    b, l, c = p.shape
    cw = c // n_out
    halo = 16
    t = _pick_tile(l, 512, halo) if col_n2 is None else FFT_ROWS * col_n2
    n_tiles = l // t
    hb = t // halo
    kern = functools.partial(_short_conv_kernel, n_tiles=n_tiles, col_n2=col_n2)
    if col_n2 is None:
        out_specs = [pl.BlockSpec((None, t, cw), lambda bi, i: (bi, i, 0))] * n_out
        out_shape = [jax.ShapeDtypeStruct((b, l, cw), BF16)] * n_out
    else:
        out_specs = [pl.BlockSpec((None, FFT_ROWS, col_n2 * cw), lambda bi, i: (bi, i, 0))] * n_out
        out_shape = [jax.ShapeDtypeStruct((b, l // col_n2, col_n2 * cw), BF16)] * n_out
    return pl.pallas_call(
        kern, grid=(b, n_tiles),
        in_specs=[pl.BlockSpec((None, t, c), lambda bi, i: (bi, i, 0)),
                  pl.BlockSpec((None, halo, c), lambda bi, i: (bi, jnp.maximum(i * hb - 1, 0), 0)),
                  pl.BlockSpec((None, halo, c), lambda bi, i: (bi, jnp.minimum((i + 1) * hb, l // halo - 1), 0)),
                  pl.BlockSpec((8, c), lambda bi, i: (0, 0)),
                  pl.BlockSpec((1, c), lambda bi, i: (0, 0))],
        out_specs=out_specs, out_shape=out_shape,
        compiler_params=_cparams("parallel", "parallel"),
        name="short_conv",
    )(p, p, p, jnp.pad(conv_w, ((0, 8 - conv_w.shape[0]), (0, 0))), conv_b.reshape(1, c))


def _hy_filter_kernel(fv_ref, c2_ref, s2_ref, w1_ref, b1_ref, sf1_ref, w2_ref, b2_ref, sf2_ref, w3f_ref, w3b_ref,
                      dl_ref, f_ref, ss_ref, cb_ref, sb_ref, *, seq_len, col_n2):
    i = pl.program_id(0)
    tl = cb_ref.shape[0]
    hi = lax.Precision.HIGHEST
    step = 2.0 * math.pi / float(seq_len)
    r = lax.broadcasted_iota(jnp.int32, (tl, 1), 0)
    lane = lax.broadcasted_iota(jnp.int32, (1, LANES), 1)

    @pl.when(i == 0)
    def _():
        ang_r = (step * r.astype(F32)) * fv_ref[...]
        cb_ref[...] = jnp.cos(ang_r)
        sb_ref[...] = jnp.sin(ang_r)
        ss_ref[...] = jnp.zeros(ss_ref.shape, F32)

    base = (step * (i * tl).astype(F32)) * fv_ref[...]
    ca, sa = jnp.cos(base), jnp.sin(base)
    cb, sb = cb_ref[...], sb_ref[...]
    cos_j, sin_j = ca * cb - sa * sb, sa * cb + ca * sb
    cos_m, sin_m = c2_ref[...] * cos_j + s2_ref[...] * sin_j, s2_ref[...] * cos_j - c2_ref[...] * sin_j
    j = (i * tl + r).astype(F32)

    def mlp(t, cos_t, sin_t):
        t_norm = t / float(max(seq_len - 1, 1))
        feat = jnp.where(lane == 0, t_norm,
                         jnp.where(lane <= HY_BANDS, cos_t, jnp.where(lane <= 2 * HY_BANDS, -sin_t, 0.0)))
        z = jnp.sin(sf1_ref[...] * (jnp.dot(feat, w1_ref[...], precision=hi, preferred_element_type=F32)
                                    + b1_ref[...]))
        z = jnp.sin(sf2_ref[...] * (jnp.dot(z, w2_ref[...], precision=hi, preferred_element_type=F32)
                                    + b2_ref[...]))
        return z.astype(BF16), t_norm

    zf, tnf = mlp(j, cos_j, sin_j)
    zb, tnb = mlp(float(seq_len) - j, cos_m, sin_m)
    cw = w3f_ref.shape[1] // f_ref.shape[0]
    for o in range(f_ref.shape[0]):
        cols = slice(o * cw, (o + 1) * cw)
        hf = jnp.dot(zf, w3f_ref[:, cols], preferred_element_type=F32) * jnp.exp(-tnf * dl_ref[:, cols])
        hb = jnp.dot(zb, w3b_ref[:, cols], preferred_element_type=F32) * jnp.exp(-tnb * dl_ref[:, cols])
        hb = jnp.where(j > 0.0, hb, 0.0)
        ss_ref[:, cols] += jnp.sum(hf * hf, axis=0, keepdims=True) + jnp.sum(hb * hb, axis=0, keepdims=True)
        for s, h in enumerate((hf, hb)):
            if col_n2 is None:
                f_ref[o, s] = h.astype(f_ref.dtype)
            else:
                _store_cols(f_ref, (o, s), h.astype(f_ref.dtype), col_n2)


def hy_filters(seq_len, w1, b1, w2, b2, w3, sin_freq, hy_w, col_n2=None):
    emb, ffn = w1.shape
    cw = 2 * hy_w
    f = np.linspace(1e-4, HY_BANDS - 1, HY_BANDS)
    fv = np.zeros((1, LANES), np.float32)
    fv[0, 1:1 + HY_BANDS] = f
    fv[0, 1 + HY_BANDS:1 + 2 * HY_BANDS] = f
    c2 = np.cos(2.0 * np.pi * fv.astype(np.float64)).astype(np.float32)
    s2 = np.sin(2.0 * np.pi * fv.astype(np.float64)).astype(np.float32)
    w1p = jnp.pad(w1, ((0, LANES - emb), (0, 0)))
    w3r = w3.reshape(ffn, 2, 2, hy_w)
    w3f = w3r[:, :, 0, :].reshape(ffn, cw).astype(BF16)
    w3b = w3r[:, :, 1, :].reshape(ffn, cw).astype(BF16)
    deltas = np.abs(np.linspace(math.log(HY_DECAY_TARGET) / HY_SLOW_PCT,
                                math.log(HY_DECAY_TARGET) / HY_FAST_PCT, hy_w)).astype(np.float32)
    dl = jnp.asarray(np.tile(deltas, 2).reshape(1, cw))
    const = lambda i: (0, 0)
    if col_n2 is None:
        tl = _pick_tile(seq_len, 512, 16)
        f_spec = pl.BlockSpec((2, 2, tl, hy_w), lambda i: (0, 0, i, 0))
        f_shape = jax.ShapeDtypeStruct((2, 2, seq_len, hy_w), BF16)
    else:
        tl = FFT_ROWS * col_n2
        f_spec = pl.BlockSpec((2, 2, FFT_ROWS, col_n2 * hy_w), lambda i: (0, 0, i, 0))
        f_shape = jax.ShapeDtypeStruct((2, 2, seq_len // col_n2, col_n2 * hy_w), BF16)
    kern = functools.partial(_hy_filter_kernel, seq_len=seq_len, col_n2=col_n2)
    return pl.pallas_call(
        kern, grid=(seq_len // tl,),
        in_specs=[pl.BlockSpec((1, LANES), const), pl.BlockSpec((1, LANES), const), pl.BlockSpec((1, LANES), const),
                  pl.BlockSpec((LANES, ffn), const),
                  pl.BlockSpec((1, ffn), const), pl.BlockSpec((1, ffn), const),
                  pl.BlockSpec((ffn, ffn), const), pl.BlockSpec((1, ffn), const), pl.BlockSpec((1, ffn), const),
                  pl.BlockSpec((ffn, cw), const), pl.BlockSpec((ffn, cw), const), pl.BlockSpec((1, cw), const)],
        out_specs=[f_spec, pl.BlockSpec((1, cw), const)],
        out_shape=[f_shape, jax.ShapeDtypeStruct((1, cw), F32)],
        scratch_shapes=[pltpu.VMEM((tl, LANES), F32), pltpu.VMEM((tl, LANES), F32)],
        compiler_params=_cparams("arbitrary"),
        name="hy_filters",
    )(jnp.asarray(fv), jnp.asarray(c2), jnp.asarray(s2), w1p, b1.reshape(1, ffn), sin_freq[0].reshape(1, ffn), w2,
      b2.reshape(1, ffn), sin_freq[1].reshape(1, ffn), w3f, w3b, dl)


def _phase_mats(phase_num, denom, conj):
    ang = np.pi * (phase_num % (2 * denom)).astype(np.float64) / denom
    cr, ci = np.cos(ang), (np.sin(ang) if conj else -np.sin(ang))
    return cr, ci


def _stage1_mats(n1):
    k = np.arange(n1)[:, None]
    n = np.arange(n1)[None, :]
    w1d, w1f, wfin = [], [], []
    for v in (0, 1):
        cr, ci = _phase_mats(2 * k * n + v * n, n1, conj=False)
        w1d.append(np.block([[cr, -ci], [ci, cr]]))
        sgn = 1.0 if v == 0 else -1.0
        w1f.append(np.block([[cr, sgn * cr], [ci, sgn * ci]]))
        cri, cii = _phase_mats(2 * n.T * k.T + v * n.T, n1, conj=True)
        wfin.append(np.block([[cri, -cii], [cii, cri]]))
    return (np.concatenate(w1d, 0).astype(np.float32), np.concatenate(w1f, 0).astype(np.float32),
            np.concatenate(wfin, 1).astype(np.float32))


def _stage2_mats(n1, n2):
    l = n1 * n2
    k2 = np.arange(n2)[:, None]
    nn = np.arange(n2)[None, :]
    base = (2 * n1 * k2 * nn) % (2 * l)
    br, bi = np.cos(np.pi * base / l), -np.sin(np.pi * base / l)
    k1 = np.arange(n1)[None, :, None]
    v = np.arange(2)[:, None, None]
    tw = (2 * np.arange(n2)[None, None, :] * k1 + v * np.arange(n2)[None, None, :]) % (2 * l)
    tr, ti = np.cos(np.pi * tw / l), -np.sin(np.pi * tw / l)
    br, bi, tr, ti = (jnp.asarray(a, F32) for a in (br, bi, tr, ti))
    cr = br[None, None] * tr[:, :, None, :] - bi[None, None] * ti[:, :, None, :]
    ci = br[None, None] * ti[:, :, None, :] + bi[None, None] * tr[:, :, None, :]
    fwd = jnp.concatenate([jnp.concatenate([cr, -ci], -1), jnp.concatenate([ci, cr], -1)], -2)
    crt, cit = jnp.swapaxes(cr, -1, -2), -jnp.swapaxes(ci, -1, -2)
    inv = jnp.concatenate([jnp.concatenate([crt, -cit], -1), jnp.concatenate([cit, crt], -1)], -2)
    return fwd.astype(BF16), inv.astype(BF16)


def _fft_stage1_kernel(z_ref, w_ref, o_ref):
    n1 = z_ref.shape[1]
    nj, cw = o_ref.shape[3], o_ref.shape[4]
    x = z_ref[...].reshape(2 * n1, nj * cw)
    for plane in range(4):
        y = jnp.dot(w_ref[plane * n1:(plane + 1) * n1, :], x, preferred_element_type=F32).astype(o_ref.dtype)
        tiles = jnp.stack([y[:, j * cw:(j + 1) * cw] for j in range(nj)], axis=0)
        o_ref[plane // 2, plane % 2] = jnp.swapaxes(tiles, 0, 1)


FFT_COLS = 2048
FFT_S1_N2 = 16


def fft_stage1(z, w1, n1, n2):
    grp = z.shape[0]
    w = z.shape[3] // n2
    nj = _pick_tile(n2, FFT_S1_N2, 16)
    return pl.pallas_call(
        _fft_stage1_kernel, grid=(grp, n2 // nj),
        in_specs=[pl.BlockSpec((None, 2, n1, nj * w), lambda gi, j: (gi, 0, 0, j)),
                  pl.BlockSpec((4 * n1, 2 * n1), lambda gi, j: (0, 0))],
        out_specs=pl.BlockSpec((None, 2, 2, n1, nj, w), lambda gi, j: (gi, 0, 0, 0, j, 0)),
        out_shape=jax.ShapeDtypeStruct((grp, 2, 2, n1, n2, w), BF16),
        compiler_params=_cparams("parallel", "parallel"),
        name="fft_stage1",
    )(z, w1)


def _fft_mid_kernel(a_ref, f_ref, ss_ref, wf_ref, wi_ref, o_ref, *, scale):
    kb, n2, cw = a_ref.shape[1], a_ref.shape[2], a_ref.shape[3]
    nrm = lax.rsqrt(ss_ref[...] + EPS) * scale
    us = []
    for kk in range(kb):
        wf = wf_ref[kk]
        t = jnp.dot(wf, a_ref[:, kk].reshape(2 * n2, cw), preferred_element_type=F32)
        g = jnp.dot(wf, f_ref[:, kk].reshape(2 * n2, cw), preferred_element_type=F32) * nrm
        tr, ti, gr, gi = t[:n2], t[n2:], g[:n2], g[n2:]
        p = jnp.concatenate([tr * gr - ti * gi, tr * gi + ti * gr], axis=0).astype(BF16)
        us.append(jnp.dot(wi_ref[kk], p, preferred_element_type=F32).astype(o_ref.dtype))
    for r in range(2):
        tiles = jnp.swapaxes(jnp.stack([u[r * n2:(r + 1) * n2] for u in us], axis=0), 0, 1)
        for j in range(n2):
            o_ref[r, :, j * cw:(j + 1) * cw] = tiles[j]


FFT_MID_K1 = 16


def fft_mid(a, af, ss, wf, wi, order, n1, n2):
    cw = a.shape[-1]
    kb = _pick_tile(n1, FFT_MID_K1, 16)
    kern = functools.partial(_fft_mid_kernel, scale=1.0 / (2.0 * n1 * n2))
    mat_spec = pl.BlockSpec((None, kb, 2 * n2, 2 * n2), lambda v, k: (v, k, 0, 0))
    return pl.pallas_call(
        kern, grid=(2, n1 // kb),
        in_specs=[pl.BlockSpec((None, 2, kb, n2, cw), lambda v, k: (v, 0, k, 0, 0)),
                  pl.BlockSpec((None, None, 2, kb, n2, cw), lambda v, k: (order, v, 0, k, 0, 0)),
                  pl.BlockSpec((1, cw), lambda v, k: (0, order)),
                  mat_spec, mat_spec],
        out_specs=pl.BlockSpec((None, 2, kb, n2 * cw), lambda v, k: (v, 0, k, 0)),
        out_shape=jax.ShapeDtypeStruct((2, 2, n1, n2 * cw), BF16),
        compiler_params=_cparams("parallel", "parallel"),
        name="fft_mid",
    )(a, af, ss, wf, wi)


def _fft_final_kernel(u_ref, w_ref, gate_ref, z_ref, bias_ref, o_ref):
    n1 = gate_ref.shape[1]
    cw = gate_ref.shape[2]
    u = u_ref[...].reshape(4 * n1, cw)
    y = jnp.dot(w_ref[...], u, preferred_element_type=F32).reshape(2, n1, cw)
    z = z_ref[...].astype(F32)
    o_ref[...] = (gate_ref[...].astype(F32) * (y + bias_ref[...] * z)).astype(o_ref.dtype)


def fft_final(u, wfin, gate, z, bias, n1, n2):
    cols = z.shape[2]
    cw = cols // n2
    tc = _pick_tile(cols, FFT_COLS, cw)
    return pl.pallas_call(
        _fft_final_kernel, grid=(cols // tc,),
        in_specs=[pl.BlockSpec((2, 2, n1, tc), lambda j: (0, 0, 0, j)),
                  pl.BlockSpec((2 * n1, 4 * n1), lambda j: (0, 0)),
                  pl.BlockSpec((2, n1, tc), lambda j: (0, 0, j)),
                  pl.BlockSpec((2, n1, tc), lambda j: (0, 0, j)),
                  pl.BlockSpec((1, tc), lambda j: (0, 0))],
        out_specs=pl.BlockSpec((2, n1, tc), lambda j: (0, 0, j)),
        out_shape=jax.ShapeDtypeStruct((2, n1, cols), BF16),
        compiler_params=_cparams("parallel"),
        name="fft_final",
    )(u, wfin, gate, z, jnp.tile(bias.reshape(1, cw), (1, tc // cw)))


def _hy_dense_kernel(z_ref, gate_ref, f_ref, ss_ref, bias_ref, wf_ref, wi_ref, o_ref):
    lc = z_ref.shape[1]
    cw = z_ref.shape[2]
    nrm = lax.rsqrt(ss_ref[...] + EPS) * (1.0 / (2.0 * lc))
    f1, f2 = f_ref[0].astype(F32), f_ref[1].astype(F32)
    filt = (((f1 + f2) * nrm).astype(BF16), ((f1 - f2) * nrm).astype(BF16))
    x = z_ref[...].reshape(2 * lc, cw)
    acc = None
    for v in (0, 1):
        wf = wf_ref[v]
        t = jnp.dot(wf, x, preferred_element_type=F32)
        g = jnp.dot(wf[:, :lc], filt[v], preferred_element_type=F32)
        tr, ti, gr, gi = t[:lc], t[lc:], g[:lc], g[lc:]
        p = jnp.concatenate([tr * gr - ti * gi, tr * gi + ti * gr], axis=0).astype(BF16)
        u = jnp.dot(wi_ref[v], p, preferred_element_type=F32)
        acc = u if acc is None else acc + u
    y = acc.reshape(2, lc, cw)
    o_ref[...] = (gate_ref[...].astype(F32) * (y + bias_ref[...] * z_ref[...].astype(F32))).astype(o_ref.dtype)


def hy_dense_conv(z_arr, gate_arr, f, ss, order, bias, wf, wi):
    lc, cw = z_arr.shape[1], z_arr.shape[2]
    return pl.pallas_call(
        _hy_dense_kernel, grid=(1,),
        in_specs=[pl.BlockSpec((2, lc, cw), lambda i: (0, 0, 0)),
                  pl.BlockSpec((2, lc, cw), lambda i: (0, 0, 0)),
                  pl.BlockSpec((None, 2, lc, cw), lambda i: (order, 0, 0, 0)),
                  pl.BlockSpec((1, cw), lambda i: (0, order)),
                  pl.BlockSpec((1, cw), lambda i: (0, 0)),
                  pl.BlockSpec((2, None, 2 * lc, 2 * lc), lambda i: (0, 0, 0, 0)),
                  pl.BlockSpec((2, None, 2 * lc, 2 * lc), lambda i: (0, 0, 0, 0))],
        out_specs=pl.BlockSpec((2, lc, cw), lambda i: (0, 0, 0)),
        out_shape=jax.ShapeDtypeStruct((2, lc, cw), BF16),
        compiler_params=_cparams("arbitrary"),
        name="hy_dense_conv",
    )(z_arr, gate_arr, f, ss, bias.reshape(1, cw), wf, wi)


def hyena_mixer(p, conv_w, conv_b, w1, b1, w2, b2, w3, sin_freq, bias, mats):
    b, l, c3 = p.shape
    assert b == 2, "batch elements are packed as the re/im parts of one complex signal"
    hy_w = c3 // 3
    if l <= 2 * FFT_N2:
        v, x1, x2 = short_conv(p, conv_w, conv_b, 3)
        f, ss = hy_filters(l, w1, b1, w2, b2, w3, sin_freq, hy_w)
        wf, wi = mats["dense"]
        z2 = hy_dense_conv(v, x1, f, ss, 0, bias[0], wf, wi)
        return hy_dense_conv(z2, x2, f, ss, 1, bias[1], wf, wi)
    n2 = FFT_N2
    n1 = l // n2
    w1d, w1f, wfin, wf, wi = mats["fft"]
    v, x1, x2 = short_conv(p, conv_w, conv_b, 3, col_n2=n2)
    f, ss = hy_filters(l, w1, b1, w2, b2, w3, sin_freq, hy_w, col_n2=n2)
    af = fft_stage1(f, w1f, n1, n2)
    a = fft_stage1(v[None], w1d, n1, n2)[0]
    z2 = fft_final(fft_mid(a, af, ss, wf, wi, 0, n1, n2), wfin, x1, v, bias[0], n1, n2)
    a = fft_stage1(z2[None], w1d, n1, n2)[0]
    y = fft_final(fft_mid(a, af, ss, wf, wi, 1, n1, n2), wfin, x2, z2, bias[1], n1, n2)
    return y.reshape(2, l, hy_w)


def hyena_mats(l_lat, l_ctx):
    n1 = l_lat // FFT_N2
    w1d, w1f, wfin = _stage1_mats(n1)
    wf, wi = _stage2_mats(n1, FFT_N2)
    mats = {"fft": (jnp.asarray(w1d, BF16), jnp.asarray(w1f, BF16), jnp.asarray(wfin, BF16), wf, wi)}
    mats["dense"] = _stage2_mats(1, l_ctx)
    return mats


def _ctx_attn_kernel(q_ref, k_ref, v_ref, o_ref):
    s = lax.dot_general(q_ref[...], k_ref[...], NT_DIMS, preferred_element_type=F32)
    m = jnp.max(s, axis=-1, keepdims=True)
    p = jnp.exp(s - m)
    l = jnp.sum(p, axis=-1, keepdims=True)
    o = jnp.dot(p.astype(BF16), v_ref[...], preferred_element_type=F32) / l
    o_ref[...] = o.astype(o_ref.dtype)


def ctx_attention(qa, q_off, ka, k_off, va, v_off, n_q_heads, group):
    b, _, lc, d = qa.shape
    return pl.pallas_call(
        _ctx_attn_kernel, grid=(b, n_q_heads),
        in_specs=[pl.BlockSpec((None, None, lc, d), lambda bi, h: (bi, q_off + h, 0, 0)),
                  pl.BlockSpec((None, None, lc, d), lambda bi, h: (bi, k_off + h // group, 0, 0)),
                  pl.BlockSpec((None, None, lc, d), lambda bi, h: (bi, v_off + h // group, 0, 0))],
        out_specs=pl.BlockSpec((None, lc, d), lambda bi, h: (bi, 0, h)),
        out_shape=jax.ShapeDtypeStruct((b, lc, n_q_heads * d), BF16),
        compiler_params=_cparams("parallel", "parallel"),
        name="ctx_attention",
    )(qa, ka, va)


NA_TILE_ROWS = 4
NA_KEY_ROWS = 12
NA_HEADS_PER_STEP = 2


def _na_geometry(n_rows):
    wr = min(NA_WIN_R, n_rows)
    nt = n_rows // NA_TILE_ROWS
    sigs = []
    for t in range(nt):
        rt = t * NA_TILE_ROWS
        w0 = int(np.clip(rt - wr // 2, 0, n_rows - NA_KEY_ROWS))
        r = rt + np.arange(NA_TILE_ROWS)
        r0 = np.clip(r - wr // 2, 0, n_rows - wr)
        sigs.append((w0 - rt, tuple((r0 - w0).tolist())))
    classes = (sigs[0], sigs[1], sigs[-1])
    for t, s in enumerate(sigs):
        assert s == classes[0 if t == 0 else (2 if t == nt - 1 else 1)], "tile does not match its bias class"
    return wr, nt, classes


def _na_bias_tables(rpb, n_rows):
    wr, _, classes = _na_geometry(n_rows)
    cols = np.arange(GRID_W)
    c0 = np.clip(cols - NA_WIN_C // 2, 0, GRID_W - NA_WIN_C)
    col_ok = (cols[None, :] >= c0[:, None]) & (cols[None, :] < c0[:, None] + NA_WIN_C)
    col_idx = np.clip(cols[None, :] - cols[:, None] + NA_WIN_C - 1, 0, 2 * NA_WIN_C - 2)
    col_sel = (col_idx[None] == np.arange(2 * NA_WIN_C - 1)[:, None, None]).astype(np.float32)
    row_sel, oks = [], []
    for off, rel_r0 in classes:
        qi = np.arange(NA_TILE_ROWS)[:, None]
        kw = np.arange(NA_KEY_ROWS)[None, :]
        r0 = np.asarray(rel_r0)[:, None]
        row_ok = (kw >= r0) & (kw < r0 + wr)
        row_idx = np.clip(kw + off - qi + NA_WIN_R - 1, 0, 2 * NA_WIN_R - 2)
        row_sel.append((row_idx[None] == np.arange(2 * NA_WIN_R - 1)[:, None, None]).astype(np.float32))
        oks.append(row_ok[:, None, :, None] & col_ok[None, :, None, :])
    vals = jnp.einsum("hab,saqk,bcd->hsqckd", rpb.astype(F32), jnp.asarray(np.stack(row_sel)),
                      jnp.asarray(col_sel), precision=lax.Precision.HIGHEST)
    tab = jnp.where(jnp.asarray(np.stack(oks))[None], vals, NEG_BIG)
    return tab.reshape(rpb.shape[0], 3, NA_TILE_ROWS * GRID_W, NA_KEY_ROWS * GRID_W)


def _na_kernel(q_ref, k_ref, v_ref, kc_ref, vc_ref, bias_ref, o_ref, *, n_rows, win_half):
    t = pl.program_id(2)
    w0 = jnp.clip(t * NA_TILE_ROWS - win_half, 0, n_rows - NA_KEY_ROWS)
    start = pl.multiple_of(w0 * GRID_W, GRID_W)
    outs = []
    for hh in range(q_ref.shape[0]):
        kwin = k_ref[hh, pl.ds(start, NA_KEY_ROWS * GRID_W), :]
        vwin = v_ref[hh, pl.ds(start, NA_KEY_ROWS * GRID_W), :]
        q = q_ref[hh]
        s_lat = lax.dot_general(q, kwin, NT_DIMS, preferred_element_type=F32) + bias_ref[hh]
        s_ctx = lax.dot_general(q, kc_ref[hh], NT_DIMS, preferred_element_type=F32)
        m = jnp.maximum(jnp.max(s_lat, axis=-1, keepdims=True), jnp.max(s_ctx, axis=-1, keepdims=True))
        p_lat = jnp.exp(s_lat - m)
        p_ctx = jnp.exp(s_ctx - m)
        l = jnp.sum(p_lat, axis=-1, keepdims=True) + jnp.sum(p_ctx, axis=-1, keepdims=True)
        o = (jnp.dot(p_lat.astype(BF16), vwin, preferred_element_type=F32)
             + jnp.dot(p_ctx.astype(BF16), vc_ref[hh], preferred_element_type=F32)) / l
        outs.append(o.astype(o_ref.dtype))
    o_ref[...] = jnp.concatenate(outs, axis=1)


def na_attention(pl_heads, pc_heads, rpb):
    b, h3, l, d = pl_heads.shape
    lc = pc_heads.shape[2]
    h = h3 // 3
    n_rows = l // GRID_W
    wr, nt, _ = _na_geometry(n_rows)
    bias = _na_bias_tables(rpb, n_rows)
    tq = NA_TILE_ROWS * GRID_W
    kw = NA_KEY_ROWS * GRID_W
    kern = functools.partial(_na_kernel, n_rows=n_rows, win_half=wr // 2)

    def cls(t):
        return jnp.where(t == 0, 0, jnp.where(t == nt - 1, 2, 1))

    hs = NA_HEADS_PER_STEP
    assert h % hs == 0
    hb = h // hs
    return pl.pallas_call(
        kern, grid=(b, hb, nt),
        in_specs=[pl.BlockSpec((None, hs, tq, d), lambda bi, hi, t: (bi, hi, t, 0)),
                  pl.BlockSpec((None, hs, l, d), lambda bi, hi, t: (bi, hb + hi, 0, 0)),
                  pl.BlockSpec((None, hs, l, d), lambda bi, hi, t: (bi, 2 * hb + hi, 0, 0)),
                  pl.BlockSpec((None, hs, lc, d), lambda bi, hi, t: (bi, hb + hi, 0, 0)),
                  pl.BlockSpec((None, hs, lc, d), lambda bi, hi, t: (bi, 2 * hb + hi, 0, 0)),
                  pl.BlockSpec((hs, None, tq, kw), lambda bi, hi, t: (hi, cls(t), 0, 0))],
        out_specs=pl.BlockSpec((None, tq, hs * d), lambda bi, hi, t: (bi, t, hi)),
        out_shape=jax.ShapeDtypeStruct((b, l, h * d), BF16),
        compiler_params=_cparams("parallel", "parallel", "parallel"),
        name="na_attention",
    )(pl_heads, pl_heads, pl_heads, pc_heads, pc_heads, bias)


def _ret_kernel(*refs, reverse, rope, finalize, n_heads):
    it = iter(refs)
    lg_ref, p_ref = next(it), next(it)
    cos_ref, sin_ref = (next(it), next(it)) if rope else (None, None)
    s0_ref = next(it)
    oprev_ref, gnw_ref = (next(it), next(it)) if finalize else (None, None)
    o_ref, sfin_ref, state_ref, mask_ref = next(it), next(it), next(it), next(it)

    i = pl.program_id(1)
    n_steps = pl.num_programs(1)
    ts = p_ref.shape[0]
    c = ts
    nc = 1

    @pl.when(i == 0)
    def _():
        state_ref[...] = s0_ref[...]
        jr = lax.broadcasted_iota(jnp.int32, (c, c), 0)
        jc = lax.broadcasted_iota(jnp.int32, (c, c), 1)
        rel = ((jc - jr) if reverse else (jr - jc)).astype(F32)
        for h in range(n_heads):
            mask_ref[h] = jnp.where(rel >= 0.0, jnp.exp(lg_ref[h] * jnp.maximum(rel, 0.0)), 0.0)

    dq = n_heads * RET_DK
    dv = n_heads * RET_DV
    x = p_ref[...]
    q = x[:, :dq].astype(F32)
    k = x[:, dq:2 * dq].astype(F32)
    v = x[:, 2 * dq:2 * dq + dv]
    if rope:
        q = q * cos_ref[...] + _swap32(q) * sin_ref[...]
        k = k * cos_ref[...] + _swap32(k) * sin_ref[...]
    jcol = lax.broadcasted_iota(jnp.int32, (c, 1), 0).astype(F32)
    order = range(nc - 1, -1, -1) if reverse else range(nc)
    blocks = [[None] * n_heads for _ in range(nc)]
    for h in range(n_heads):
        g = lg_ref[h]
        intra = mask_ref[h]
        if reverse:
            cross_f = jnp.exp(g * (float(c) - jcol))
            k_dec = jnp.exp(g * jcol)
        else:
            cross_f = jnp.exp(g * (jcol + 1.0))
            k_dec = jnp.exp(g * (float(c - 1) - jcol))
        chunk_decay = jnp.exp(g * float(c))
        state = state_ref[h]
        for ci in order:
            rows = slice(ci * c, (ci + 1) * c)
            qh = q[rows, h * RET_DK:(h + 1) * RET_DK].astype(BF16)
            kf = k[rows, h * RET_DK:(h + 1) * RET_DK]
            vh = v[rows, h * RET_DV:(h + 1) * RET_DV]
            scores = lax.dot_general(qh, kf.astype(BF16), NT_DIMS, preferred_element_type=F32) * intra
            inner = jnp.dot(scores.astype(BF16), vh, preferred_element_type=F32)
            cross = jnp.dot(qh, state.astype(BF16), preferred_element_type=F32) * cross_f
            blocks[ci][h] = inner + cross
            kv = lax.dot_general((kf * k_dec).astype(BF16), vh, TN_DIMS, preferred_element_type=F32)
            state = chunk_decay * state + kv
        state_ref[h] = state
    o = jnp.concatenate([jnp.concatenate(blocks[ci], axis=1) for ci in range(nc)], axis=0)
    if finalize:
        o = o + oprev_ref[...]
        gate = x[:, 2 * dq + dv:2 * dq + 2 * dv].astype(F32)
        normed = []
        for h in range(n_heads):
            oh = o[:, h * RET_DV:(h + 1) * RET_DV]
            normed.append(oh * lax.rsqrt(jnp.mean(oh * oh, axis=-1, keepdims=True) + EPS))
        o = jnp.concatenate(normed, axis=1) * gnw_ref[...] * _silu(gate)
    o_ref[...] = o.astype(o_ref.dtype)

    @pl.when(i == n_steps - 1)
    def _():
        sfin_ref[...] = state_ref[...]


def retention_pass(p, log_g, s0, reverse, rope_tables=None, o_prev=None, gn_w=None):
    b, l, w = p.shape
    n_heads = w // (2 * RET_DK + 2 * RET_DV)
    dq, dv = n_heads * RET_DK, n_heads * RET_DV
    ts = _pick_tile(l, 512, RET_CHUNK)
    n_steps = l // ts
    finalize = o_prev is not None
    rope = rope_tables is not None

    def tile(i):
        return (n_steps - 1 - i) if reverse else i

    in_specs = [pl.BlockSpec(memory_space=pltpu.SMEM),
                pl.BlockSpec((None, ts, w), lambda bi, i: (bi, tile(i), 0))]
    args = [log_g, p]
    if rope:
        in_specs += [pl.BlockSpec((ts, dq), lambda bi, i: (tile(i), 0))] * 2
        args += list(rope_tables)
    in_specs.append(pl.BlockSpec((None, n_heads, RET_DK, RET_DV), lambda bi, i: (bi, 0, 0, 0)))
    args.append(s0)
    if finalize:
        in_specs += [pl.BlockSpec((None, ts, dv), lambda bi, i: (bi, tile(i), 0)),
                     pl.BlockSpec((1, dv), lambda bi, i: (0, 0))]
        args += [o_prev, gn_w.reshape(1, dv)]
    kern = functools.partial(_ret_kernel, reverse=reverse, rope=rope, finalize=finalize, n_heads=n_heads)
    return pl.pallas_call(
        kern, grid=(b, n_steps),
        in_specs=in_specs,
        out_specs=[pl.BlockSpec((None, ts, dv), lambda bi, i: (bi, tile(i), 0)),
                   pl.BlockSpec((None, n_heads, RET_DK, RET_DV), lambda bi, i: (bi, 0, 0, 0))],
        out_shape=[jax.ShapeDtypeStruct((b, l, dv), BF16 if finalize else F32),
                   jax.ShapeDtypeStruct((b, n_heads, RET_DK, RET_DV), F32)],
        scratch_shapes=[pltpu.VMEM((n_heads, RET_DK, RET_DV), F32), pltpu.VMEM((n_heads, ts, ts), F32)],
        compiler_params=_cparams("parallel", "arbitrary"),
        name="retention_pass",
    )(*args)


def _ret_rope_tables(l, n_heads):
    half = RET_DK // 2
    inv = RET_ROPE_BASE ** (-jnp.linspace(0.0, 1.0, half, dtype=F32))
    ang = jnp.arange(l, dtype=F32)[:, None] * inv
    cos, sin = lax.optimization_barrier((jnp.cos(ang), jnp.sin(ang)))
    return (jnp.tile(jnp.concatenate([cos, cos], -1), (1, n_heads)),
            jnp.tile(jnp.concatenate([-sin, sin], -1), (1, n_heads)))


def retention_mixer(p_l, p_c, log_decay, gn_w, tables, with_ctx_out):
    b = p_l.shape[0]
    n_heads = p_l.shape[2] // (2 * RET_DK + 2 * RET_DV)
    log_g = -jnp.abs(log_decay.astype(F32))
    s0 = jnp.zeros((b, n_heads, RET_DK, RET_DV), F32)
    o_cf, s_fwd = retention_pass(p_c, log_g[0], s0, False)
    y_c, s_bwd = retention_pass(p_c, log_g[1], s0, True, o_prev=o_cf, gn_w=gn_w)
    o_lf, _ = retention_pass(p_l, log_g[0], s_fwd, False, rope_tables=tables)
    y_l, _ = retention_pass(p_l, log_g[1], s_bwd, True, rope_tables=tables, o_prev=o_lf, gn_w=gn_w)
    return y_l, (y_c if with_ctx_out else None)


def _gqa_prep_kernel(*refs, rope, n_q, n_kv, q_scale):
    if rope:
        p_ref, cos_ref, sin_ref, qw_ref, kw_ref, q_out, k_out, v_out = refs
    else:
        p_ref, qw_ref, kw_ref, q_out, k_out, v_out = refs
    x = p_ref[...]
    d = HEAD_DIM

    def norm_rope(xh, w):
        xh = xh.astype(F32)
        y = xh * lax.rsqrt(jnp.mean(xh * xh, axis=-1, keepdims=True) + EPS) * w
        if rope:
            y = y * cos_ref[...] + _swap32(y) * sin_ref[...]
        return y

    for h in range(n_q):
        q_out[h] = (norm_rope(x[:, h * d:(h + 1) * d], qw_ref[...]) * q_scale).astype(q_out.dtype)
    for h in range(n_kv):
        k_out[h] = norm_rope(x[:, (n_q + h) * d:(n_q + h + 1) * d], kw_ref[...]).astype(k_out.dtype)
        lane = lax.broadcasted_iota(jnp.int32, (x.shape[0], d), 1)
        ones_col = jnp.where(lane == 0, 1.0, 0.0).astype(v_out.dtype)
        v_out[h] = jnp.concatenate([x[:, (n_q + n_kv + h) * d:(n_q + n_kv + h + 1) * d], ones_col], axis=1)


def gqa_prep(p, qn_w, kn_w, n_q, n_kv, q_scale, rope_tables=None):
    b, l, w = p.shape
    d = HEAD_DIM
    t = _pick_tile(l, 512, 16)
    rope = rope_tables is not None
    in_specs = [pl.BlockSpec((None, t, w), lambda bi, i: (bi, i, 0))]
    args = [p]
    if rope:
        in_specs += [pl.BlockSpec((t, d), lambda bi, i: (i, 0))] * 2
        args += list(rope_tables)
    in_specs += [pl.BlockSpec((1, d), lambda bi, i: (0, 0))] * 2
    args += [qn_w.reshape(1, d), kn_w.reshape(1, d)]
    kern = functools.partial(_gqa_prep_kernel, rope=rope, n_q=n_q, n_kv=n_kv, q_scale=q_scale)
    return pl.pallas_call(
        kern, grid=(b, l // t),
        in_specs=in_specs,
        out_specs=[pl.BlockSpec((None, n_q, t, d), lambda bi, i: (bi, 0, i, 0)),
                   pl.BlockSpec((None, n_kv, t, d), lambda bi, i: (bi, 0, i, 0)),
                   pl.BlockSpec((None, n_kv, t, 2 * d), lambda bi, i: (bi, 0, i, 0))],
        out_shape=[jax.ShapeDtypeStruct((b, n_q, l, d), BF16),
                   jax.ShapeDtypeStruct((b, n_kv, l, d), BF16),
                   jax.ShapeDtypeStruct((b, n_kv, l, 2 * d), BF16)],
        compiler_params=_cparams("parallel", "parallel"),
        name="gqa_prep",
    )(*args)


def _axial_rope_tables(l):
    nf = HEAD_DIM // 4
    t = jnp.arange(l)
    inv = ROPE_BASE ** (-jnp.arange(nf, dtype=F32) / nf)
    ang_r = (t // GRID_W).astype(F32)[:, None] * inv
    ang_c = (t % GRID_W).astype(F32)[:, None] * inv
    cr, sr, cc, sc = lax.optimization_barrier((jnp.cos(ang_r), jnp.sin(ang_r), jnp.cos(ang_c), jnp.sin(ang_c)))
    return (jnp.concatenate([cr, cr, cc, cc], -1), jnp.concatenate([-sr, sr, -sc, sc], -1))


FLASH_SUB_ROWS = 256


def _flash_kernel(q_ref, k_ref, v_ref, o_ref, m_ref, acc_ref):
    j = pl.program_id(3)
    g, tq, d = q_ref.shape

    @pl.when(j == 0)
    def _():
        m_ref[...] = jnp.full(m_ref.shape, NEG_BIG, F32)
        acc_ref[...] = jnp.zeros(acc_ref.shape, F32)

    q = q_ref[...].reshape(g * tq, d)
    k = k_ref[...]
    v = v_ref[...]
    n_sub = (g * tq) // FLASH_SUB_ROWS
    s, p, alpha = [None] * n_sub, [None] * n_sub, [None] * n_sub
    for t in range(n_sub + 2):
        if t < n_sub:
            s[t] = lax.dot_general(q[t * FLASH_SUB_ROWS:(t + 1) * FLASH_SUB_ROWS], k, NT_DIMS,
                                   preferred_element_type=F32)
        u = t - 1
        if 0 <= u < n_sub:
            rows = slice(u * FLASH_SUB_ROWS, (u + 1) * FLASH_SUB_ROWS)
            m_prev = m_ref[rows]
            m_new = jnp.maximum(m_prev, jnp.max(s[u], axis=-1, keepdims=True))
            alpha[u] = jnp.exp2(m_prev - m_new)
            m_ref[rows] = m_new
            p[u] = jnp.exp2((s[u] - m_new).astype(BF16))
            s[u] = None
        w = t - 2
        if 0 <= w < n_sub:
            rows = slice(w * FLASH_SUB_ROWS, (w + 1) * FLASH_SUB_ROWS)
            acc_ref[rows] = alpha[w] * acc_ref[rows] + jnp.dot(p[w], v, preferred_element_type=F32)
            p[w] = None

    @pl.when(j == pl.num_programs(3) - 1)
    def _():
        acc = acc_ref[...]
        o = acc[:, :d] / acc[:, d:d + 1]
        o_ref[...] = jnp.concatenate([o[h * tq:(h + 1) * tq] for h in range(g)], axis=1).astype(o_ref.dtype)


def flash_gqa(q, k, v):
    b, hq, l, d = q.shape
    hkv, lk = k.shape[1], k.shape[2]
    g = hq // hkv
    tq = _pick_tile(l, 1024, 16)
    tk = _pick_tile(lk, 8320, LANES)
    assert (g * tq) % FLASH_SUB_ROWS == 0
    return pl.pallas_call(
        _flash_kernel, grid=(b, hkv, l // tq, lk // tk),
        in_specs=[pl.BlockSpec((None, g, tq, d), lambda bi, h, i, j: (bi, h, i, 0)),
                  pl.BlockSpec((None, None, tk, d), lambda bi, h, i, j: (bi, h, j, 0)),
                  pl.BlockSpec((None, None, tk, 2 * d), lambda bi, h, i, j: (bi, h, j, 0))],
        out_specs=pl.BlockSpec((None, tq, g * d), lambda bi, h, i, j: (bi, i, h)),
        out_shape=jax.ShapeDtypeStruct((b, l, hq * d), BF16),
        scratch_shapes=[pltpu.VMEM((g * tq, 1), F32), pltpu.VMEM((g * tq, 2 * d), F32)],
        compiler_params=_cparams("parallel", "parallel", "parallel", "arbitrary"),
        name="flash_gqa",
    )(q, k, v)


def gqa_mixer(p_l, p_c, qn_w, kn_w, tables, with_ctx_out):
    n_q = p_l.shape[2] // (2 * HEAD_DIM)
    n_kv = n_q // 2
    scale = HEAD_DIM ** -0.5
    q_l, k_l, v_l = gqa_prep(p_l, qn_w, kn_w, n_q, n_kv, scale * math.log2(math.e), rope_tables=tables)
    q_c, k_c, v_c = gqa_prep(p_c, qn_w, kn_w, n_q, n_kv, scale)
    y_l = flash_gqa(q_l, jnp.concatenate([k_l, k_c], axis=2), jnp.concatenate([v_l, v_c], axis=2))
    y_c = ctx_attention(q_c, 0, k_c, 0, v_c, 0, n_q, n_q // n_kv) if with_ctx_out else None
    return y_l, y_c


W_IN_GATE_ALIGN = 2048


def _prep_w_in(w_in, d_model):
    mix_w = d_model // N_BRANCH
    h = mix_w // HEAD_DIM
    sizes = (3 * mix_w, 3 * h * HEAD_DIM, 2 * h * RET_DK + 2 * h * RET_DV,
             (h + 2 * max(h // 2, 1)) * HEAD_DIM, N_BRANCH * d_model)
    scale = np.ones((sum(sizes),), np.float32)
    scale[sizes[0]:sizes[0] + h * HEAD_DIM] = HEAD_DIM ** -0.5
    k0 = sizes[0] + sizes[1] + h * RET_DK
    scale[k0:k0 + h * RET_DK] = RET_DK ** -0.5
    head = sum(sizes[:4])
    pad = -head % W_IN_GATE_ALIGN
    w = w_in * jnp.asarray(scale)
    w = jnp.concatenate([w[..., :head], jnp.zeros(w.shape[:2] + (pad,), w.dtype), w[..., head:]], axis=-1)
    offs = [0, sizes[0], sizes[0] + sizes[1], sizes[0] + sizes[1] + sizes[2], head + pad]
    return w.astype(BF16), list(zip(offs, sizes))


def kernel(x, c, ctx, c_ctx, ada_w, ada_b, norm1_w, norm2_w, w_in, hy_conv_w, hy_conv_b, hy_ffn_w1, hy_ffn_b1,
           hy_ffn_w2, hy_ffn_b2, hy_ffn_w3, hy_sin_freq, hy_bias, na_rpb, ret_log_decay, ret_gn_w, gqa_q_norm_w,
           gqa_k_norm_w, w_branch, w_out, ffn_w13, ffn_w2, final_norm_w):
    b, l, d = x.shape
    lc = ctx.shape[1]
    depth = ada_w.shape[0]
    ffn_hidden = ffn_w2.shape[1]
    n_ret_heads = (d // N_BRANCH) // RET_DV

    cs = jnp.zeros((8, d), F32).at[:b].set(c).at[b].set(c_ctx)
    mods = ada_mod(cs, ada_w, ada_b)
    ctx_row = b
    mats = hyena_mats(l, lc)
    ret_tables = _ret_rope_tables(l, n_ret_heads)
    gqa_tables = _axial_rope_tables(l)

    w_in_b, groups = _prep_w_in(w_in, d)
    wb = w_branch.astype(BF16)
    w_out_b = w_out.astype(BF16)
    w13_b = ffn_w13.astype(BF16)
    w2_b = ffn_w2.astype(BF16)

    x_l = x.reshape(b * l, d)
    x_c = ctx.reshape(b * lc, d)
    for layer in range(depth):
        last = layer == depth - 1
        mod = mods[layer]
        w_hy, w_na, w_rt, w_gq, w_gate = (WCols(w_in_b, layer, off, n) for off, n in groups)
        hy_params = (hy_conv_w[layer], hy_conv_b[layer], hy_ffn_w1[layer], hy_ffn_b1[layer], hy_ffn_w2[layer],
                     hy_ffn_b2[layer], hy_ffn_w3[layer], hy_sin_freq[layer], hy_bias[layer])
        wo = WCols(w_out_b, layer, 0, d)
        w1 = WCols(w13_b, layer, 0, ffn_hidden)
        w3 = WCols(w13_b, layer, ffn_hidden, ffn_hidden)
        w2 = WCols(w2_b, layer, 0, d)

        h_l = norm_mod(x_l, norm1_w[layer], mod, 0, 1, l, 0)
        h_c = norm_mod(x_c, norm1_w[layer], mod, 0, 1, None, ctx_row)

        na_l = matmul_heads(h_l, w_na, b)
        na_c = matmul_heads(h_c, w_na, b)
        rt_l = matmul(h_l, w_rt).reshape(b, l, -1)
        rt_c = matmul(h_c, w_rt).reshape(b, lc, -1)
        gq_l = matmul(h_l, w_gq).reshape(b, l, -1)
        gq_c = matmul(h_c, w_gq).reshape(b, lc, -1)
        hy_l = matmul(h_l, w_hy).reshape(b, l, -1)

        y_hy_l = hyena_mixer(hy_l, *hy_params, mats)
        y_na_l = na_attention(na_l, na_c, na_rpb[layer])
        y_rt_l, y_rt_c = retention_mixer(rt_l, rt_c, ret_log_decay[layer], ret_gn_w[layer], ret_tables, not last)
        y_gq_l, y_gq_c = gqa_mixer(gq_l, gq_c, gqa_q_norm_w[layer], gqa_k_norm_w[layer], gqa_tables, not last)

        m_l = merge_branches(h_l, [y.reshape(b * l, -1) for y in (y_hy_l, y_na_l, y_rt_l, y_gq_l)], w_gate, wb,
                             layer)
        x_l, h2 = matmul_residual_norm(m_l, wo, x_l, mod, 2, norm2_w[layer], 3, 4, l, 0)
        x_l = matmul_residual(matmul_swiglu(h2, w1, w3), w2, x_l, mod, 5, l, 0)

        if not last:
            n_na = na_c.shape[1] // 3
            hy_c = matmul(h_c, w_hy).reshape(b, lc, -1)
            y_hy_c = hyena_mixer(hy_c, *hy_params, mats)
            y_na_c = ctx_attention(na_c, 0, na_c, n_na, na_c, 2 * n_na, n_na, 1)
            m_c = merge_branches(h_c, [y.reshape(b * lc, -1) for y in (y_hy_c, y_na_c, y_rt_c, y_gq_c)], w_gate,
                                 wb, layer)
            x_c, h2c = matmul_residual_norm(m_c, wo, x_c, mod, 2, norm2_w[layer], 3, 4, None, ctx_row)
            x_c = matmul_residual(matmul_swiglu(h2c, w1, w3), w2, x_c, mod, 5, None, ctx_row)

    return final_norm(x_l, final_norm_w).reshape(b, l, d)
```
